```python
import jax, jax.numpy as jnp
from jax import lax
import numpy as np

D_MODEL = 2048
BATCH = 4
SEQ = 2048
DEPTH = 2
DEC_BATCH = 128
DEC_SEQ = 4
PAST_LEN = 16384
PAGE_SIZE = 128

N_MIXERS = 2
N_POOL_LAYERS = (DEPTH + 1) // 2
N_RET_LAYERS = DEPTH // 2
POOL_WINDOWS = (2, 4, 8, 16)
POOL_GROUPS = len(POOL_WINDOWS)
POOL_GW = D_MODEL // POOL_GROUPS
POOL_BUF = max(POOL_WINDOWS) - 1
RET_HEADS = 8
RET_DK = D_MODEL // RET_HEADS
RET_DV = 2 * RET_DK
RET_QK = RET_HEADS * RET_DK
RET_V = RET_HEADS * RET_DV
RET_CHUNK = 128
ROPE_BASE = 10000.0
FFN_DIM = 5632
CONV_W = 3
PLE_DIM = 256
EPS = 1e-6

kernel_name = "hybrid_pool_retention_decoder_step"

F32 = jnp.float32


def _rmsnorm(x, g):
    xf = x.astype(F32)
    y = xf * lax.rsqrt(jnp.mean(xf * xf, axis=-1, keepdims=True) + EPS)
    return (y * g.astype(F32)).astype(x.dtype)


def _pool_mixer(h, buf, pos0, w, scale):
    B, T, D = h.shape
    h_ext = jnp.concatenate([buf.astype(h.dtype), h], axis=1)
    cs = jnp.cumsum(h_ext.astype(F32), axis=1)
    cs = jnp.concatenate([jnp.zeros((B, 1, D), F32), cs], axis=1)
    pos = pos0 + jnp.arange(T)
    P = POOL_BUF
    outs = []
    for gi, win in enumerate(POOL_WINDOWS):
        sl = slice(gi * POOL_GW, (gi + 1) * POOL_GW)
        hi = cs[:, P + 1:P + 1 + T, sl]
        lo = cs[:, P + 1 - win:P + 1 - win + T, sl]
        cnt = jnp.minimum(pos + 1, win).astype(F32)[None, :, None]
        d = (hi - lo) / cnt - h[..., sl].astype(F32)
        outs.append(jnp.einsum('btc,cd->btd', d.astype(h.dtype), w[gi]))
    y = jnp.concatenate(outs, axis=-1) * scale
    return y, h_ext[:, -P:]


def _rotary(x, pos):
    half = RET_DK // 2
    inv = ROPE_BASE ** (-jnp.arange(half, dtype=F32) / half)
    ang = pos.astype(F32)[:, None] * inv[None, :]
    cos, sin = jnp.cos(ang), jnp.sin(ang)
    x1, x2 = x[..., :half], x[..., half:]
    return jnp.concatenate([x1 * cos - x2 * sin, x1 * sin + x2 * cos], axis=-1)


def _retention_decays(C):
    log_g = jnp.log1p(-(2.0 ** (-5.0 - jnp.arange(RET_HEADS, dtype=F32))))
    n = jnp.arange(C, dtype=F32)
    diff = n[:, None] - n[None, :]
    intra = jnp.where(diff >= 0, jnp.exp(log_g[:, None, None] * jnp.maximum(diff, 0.0)), 0.0)
    cross = jnp.exp(log_g[:, None] * (n + 1.0))
    kdec = jnp.exp(log_g[:, None] * (C - 1.0 - n))
    sdec = jnp.exp(log_g * C)
    return intra, cross, kdec, sdec


def _retention(h, s0, pos0, w_in, w_out):
    B, T, _ = h.shape
    proj = jnp.einsum('btd,de->bte', h, w_in)
    q = proj[..., :RET_QK]
    k = proj[..., RET_QK:2 * RET_QK]
    v = proj[..., 2 * RET_QK:2 * RET_QK + RET_V]
    g = proj[..., 2 * RET_QK + RET_V:]

    def heads(t, dh):
        return t.reshape(B, T, RET_HEADS, dh).transpose(0, 2, 1, 3).astype(F32)

    pos = pos0 + jnp.arange(T)
    q = _rotary(heads(q, RET_DK), pos)
    k = _rotary(heads(k, RET_DK), pos) * (RET_DK ** -0.5)
    v = heads(v, RET_DV)
    C = RET_CHUNK if T % RET_CHUNK == 0 else T
    nc = T // C
    intra, cross, kdec, sdec = _retention_decays(C)

    def to_chunks(t):
        return jnp.moveaxis(t.reshape(B, RET_HEADS, nc, C, t.shape[-1]), 2, 0)

    def step(S, qkv):
        qc, kc, vc = qkv
        sc = jnp.einsum('bhnk,bhmk->bhnm', qc, kc) * intra
        o = (jnp.einsum('bhnm,bhmv->bhnv', sc, vc)
             + jnp.einsum('bhnk,bhkv->bhnv', qc, S) * cross[None, :, :, None])
        S = sdec[None, :, None, None] * S + jnp.einsum('bhmk,bhmv->bhkv', kc * kdec[None, :, :, None], vc)
        return S, o

    s_new, o = lax.scan(step, s0.astype(F32), (to_chunks(q), to_chunks(k), to_chunks(v)))
    o = jnp.moveaxis(o, 0, 2).reshape(B, RET_HEADS, T, RET_DV)
    mu = jnp.mean(o, axis=-1, keepdims=True)
    var = jnp.mean(jnp.square(o - mu), axis=-1, keepdims=True)
    o = (o - mu) * lax.rsqrt(var + EPS)
    o = o.transpose(0, 2, 1, 3).reshape(B, T, RET_V)
    y = jnp.einsum('btv,vd->btd', (jax.nn.silu(g.astype(F32)) * o).astype(h.dtype), w_out)
    return y, s_new


def _conv_ffn(h, buf, w_up, cw, cb, w_down):
    T = h.shape[1]
    u = jnp.einsum('btd,df->btf', h, w_up)
    u_ext = jnp.concatenate([buf.astype(u.dtype), u], axis=1)
    c = cw[0] * u_ext[:, :T] + cw[1] * u_ext[:, 1:T + 1] + cw[2] * u_ext[:, 2:T + 2] + cb
    gate, up = c[..., :FFN_DIM], c[..., FFN_DIM:]
    y = jnp.einsum('btf,fd->btd', jax.nn.silu(gate) * up, w_down)
    return y, u_ext[:, -(CONV_W - 1):]


def _trunk(x, p, pool_bufs, ret_states, conv_bufs, pos0, norm_mix, norm_ffn, norm_ple, norm_final,
           pool_w, pool_scale, ret_w_in, ret_w_out, ffn_w_up, ffn_conv_w, ffn_conv_b, ffn_w_down,
           ple_w_proj, ple_w_gate):
    new_pool, new_ret, new_conv = [], [], []
    for i in range(DEPTH):
        h = _rmsnorm(x, norm_mix[i])
        j = i // N_MIXERS
        if i % N_MIXERS == 0:
            y, nb = _pool_mixer(h, pool_bufs[j], pos0, pool_w[j], pool_scale[j])
            new_pool.append(nb)
        else:
            y, ns = _retention(h, ret_states[j], pos0, ret_w_in[j], ret_w_out[j])
            new_ret.append(ns)
        x = x + y
        h = _rmsnorm(x, norm_ffn[i])
        y, nc = _conv_ffn(h, conv_bufs[i], ffn_w_up[i], ffn_conv_w[i], ffn_conv_b[i], ffn_w_down[i])
        new_conv.append(nc)
        x = x + y
        gate = jax.nn.sigmoid(jnp.einsum('btd,de->bte', _rmsnorm(x, norm_ple[i]), ple_w_gate[i]).astype(F32))
        emb = jnp.einsum('btp,pd->btd', p[i].astype(x.dtype), ple_w_proj[i])
        x = x + (gate * emb.astype(F32)).astype(x.dtype)
    return _rmsnorm(x, norm_final), jnp.stack(new_pool), jnp.stack(new_ret), jnp.stack(new_conv)


def setup_inputs(seed: int = 0) -> dict:
    key = jax.random.key(seed)
    ks = jax.random.split(key, 24)
    nrm = jax.random.normal
    F2 = 2 * FFN_DIM
    return {
        "x_prompt": nrm(ks[0], (BATCH, SEQ, D_MODEL), F32),
        "x_sample": nrm(ks[1], (DEC_BATCH, DEC_SEQ, D_MODEL), F32),
        "p_prompt": nrm(ks[2], (DEPTH, BATCH, SEQ, PLE_DIM), F32),
        "p_sample": nrm(ks[3], (DEPTH, DEC_BATCH, DEC_SEQ, PLE_DIM), F32),
        "state_pool": nrm(ks[4], (N_POOL_LAYERS, DEC_BATCH, POOL_BUF, D_MODEL), F32),
        "state_ret": 0.1 * nrm(ks[5], (N_RET_LAYERS, DEC_BATCH, RET_HEADS, RET_DK, RET_DV), F32),
        "state_conv": nrm(ks[6], (DEPTH, DEC_BATCH, CONV_W - 1, F2), F32),
        "norm_mix": 1.0 + 0.02 * nrm(ks[7], (DEPTH, D_MODEL), F32),
        "norm_ffn": 1.0 + 0.02 * nrm(ks[8], (DEPTH, D_MODEL), F32),
        "norm_ple": 1.0 + 0.02 * nrm(ks[9], (DEPTH, D_MODEL), F32),
        "norm_final": 1.0 + 0.02 * nrm(ks[10], (D_MODEL,), F32),
        "pool_w": nrm(ks[11], (N_POOL_LAYERS, POOL_GROUPS, POOL_GW, POOL_GW), F32) * POOL_GW ** -0.5,
        "pool_scale": 0.5 + 0.05 * nrm(ks[12], (N_POOL_LAYERS, D_MODEL), F32),
        "ret_w_in": nrm(ks[13], (N_RET_LAYERS, D_MODEL, 2 * RET_QK + 2 * RET_V), F32) * D_MODEL ** -0.5,
        "ret_w_out": nrm(ks[14], (N_RET_LAYERS, RET_V, D_MODEL), F32) * RET_V ** -0.5,
        "ffn_w_up": nrm(ks[15], (DEPTH, D_MODEL, F2), F32) * D_MODEL ** -0.5,
        "ffn_conv_w": 0.5 * nrm(ks[16], (DEPTH, CONV_W, F2), F32),
        "ffn_conv_b": 0.01 * nrm(ks[17], (DEPTH, F2), F32),
        "ffn_w_down": nrm(ks[18], (DEPTH, FFN_DIM, D_MODEL), F32) * FFN_DIM ** -0.5,
        "ple_w_proj": nrm(ks[19], (DEPTH, PLE_DIM, D_MODEL), F32) * PLE_DIM ** -0.5,
        "ple_w_gate": nrm(ks[20], (DEPTH, D_MODEL, D_MODEL), F32) * D_MODEL ** -0.5,
    }


def reference(x_prompt, x_sample, p_prompt, p_sample, state_pool, state_ret, state_conv,
              norm_mix, norm_ffn, norm_ple, norm_final, pool_w, pool_scale, ret_w_in, ret_w_out,
              ffn_w_up, ffn_conv_w, ffn_conv_b, ffn_w_down, ple_w_proj, ple_w_gate):
    dt = x_prompt.dtype
    zero_pool = jnp.zeros((N_POOL_LAYERS, BATCH, POOL_BUF, D_MODEL), dt)
    zero_ret = jnp.zeros((N_RET_LAYERS, BATCH, RET_HEADS, RET_DK, RET_DV), F32)
    zero_conv = jnp.zeros((DEPTH, BATCH, CONV_W - 1, 2 * FFN_DIM), dt)
    y_prompt, pool_p, ret_p, conv_p = _trunk(
        x_prompt, p_prompt, zero_pool, zero_ret, zero_conv, 0,
        norm_mix, norm_ffn, norm_ple, norm_final, pool_w, pool_scale, ret_w_in, ret_w_out,
        ffn_w_up, ffn_conv_w, ffn_conv_b, ffn_w_down, ple_w_proj, ple_w_gate)
    y_sample, pool_s, ret_s, conv_s = _trunk(
        x_sample, p_sample, state_pool, state_ret, state_conv, PAST_LEN,
        norm_mix, norm_ffn, norm_ple, norm_final, pool_w, pool_scale, ret_w_in, ret_w_out,
        ffn_w_up, ffn_conv_w, ffn_conv_b, ffn_w_down, ple_w_proj, ple_w_gate)
    return (y_prompt, y_sample, pool_p, pool_s, ret_p, ret_s, conv_p, conv_s)
```

```python
import functools
import math

import jax
import jax.numpy as jnp
from jax import lax
from jax.experimental import pallas as pl
from jax.experimental.pallas import tpu as pltpu

F32 = jnp.float32
BF16 = jnp.bfloat16

EPS = 1e-6
POOL_WINDOWS = (2, 4, 8, 16)
POOL_BUF = max(POOL_WINDOWS) - 1
RET_HEADS = 8
RET_CHUNK = 128
ROPE_BASE = 10000.0
CONV_W = 3
PAST_LEN = 16384

TOKEN_TILE = 512
FFN_CHUNK = 512
PROJ_CHUNK = 1024
OUT_CHUNK = 512
SUBLANES = 8
VMEM_LIMIT = 56 * 1024 * 1024


def _params(n_axes):
    return pltpu.CompilerParams(
        dimension_semantics=("arbitrary",) * n_axes, vmem_limit_bytes=VMEM_LIMIT)


def _rmsnorm(x, g):
    return x * lax.rsqrt(jnp.mean(x * x, axis=-1, keepdims=True) + EPS) * g


def _shift_rows(a, rows):
    return pltpu.roll(a, rows, axis=0)


def _pool_kernel(*refs, shift, pos0, tiles_per_seq, from_state):
    if from_state:
        x_ref, halo_ref, g_ref, w_ref, sc_ref, o_ref, h_ref, hs_ref, ext_ref = refs
    else:
        x_ref, g_ref, w_ref, sc_ref, o_ref, h_ref, hs_ref, ext_ref = refs
    tm, d = x_ref.shape
    gw = o_ref.shape[1]
    halo = (POOL_BUF + 1) * shift
    j = pl.program_id(0) % tiles_per_seq
    grp = pl.program_id(1)

    @pl.when(grp == 0)
    def _():
        h = _rmsnorm(x_ref[...], g_ref[...])
        hs_ref[...] = h
        h_ref[...] = h if from_state else h[tm - halo:, :]

    row = lax.broadcasted_iota(jnp.int32, (tm, 1), 0)
    if shift == 1:
        step = j * tm + row
    else:
        step = jnp.zeros_like(row)
        for k in range(1, tm // shift):
            step = step + (row >= k * shift).astype(jnp.int32)
    pos = pos0 + step

    for gi, win in enumerate(POOL_WINDOWS):
        @pl.when(grp == gi)
        def _(gi=gi, win=win):
            cols = slice(gi * gw, (gi + 1) * gw)
            ext = ext_ref.at[0 if from_state else gi]
            h = hs_ref[:, cols]
            if from_state:
                ext[0:shift, :] = jnp.zeros((shift, gw), F32)
                ext[shift:halo, :] = halo_ref[...]
            else:
                @pl.when(j == 0)
                def _():
                    ext[0:halo, :] = jnp.zeros((halo, gw), F32)

                @pl.when(j != 0)
                def _():
                    ext[0:halo, :] = ext[tm:tm + halo, :]
            ext[halo:halo + tm, :] = h
            a = ext[...]
            span = 1
            while span < win:
                a = a + _shift_rows(a, span * shift)
                span *= 2
            cnt = jnp.minimum(pos + 1, win).astype(F32)
            dlt = a[halo:halo + tm, :] / cnt - h
            y = jnp.dot(dlt.astype(BF16), w_ref[...], preferred_element_type=F32)
            o_ref[...] = x_ref[:, cols] + y * sc_ref[...]


def _pool_mixer(x, halo, g, w, scale, *, shift, pos0, rows_per_seq):
    n, d = x.shape
    ng = len(POOL_WINDOWS)
    gw = d // ng
    tm = min(TOKEN_TILE, n)
    from_state = halo is not None
    tiles_per_seq = max(rows_per_seq // tm, 1)
    halo_rows = (POOL_BUF + 1) * shift
    in_specs = [pl.BlockSpec((tm, d), lambda i, gi: (i, 0))]
    args = [x]
    if from_state:
        in_specs.append(pl.BlockSpec((halo.shape[0], gw), lambda i, gi: (0, gi)))
        args.append(halo)
    in_specs += [pl.BlockSpec((1, d), lambda i, gi: (0, 0)),
                 pl.BlockSpec((None, gw, gw), lambda i, gi: (gi, 0, 0)),
                 pl.BlockSpec((1, gw), lambda i, gi: (0, gi))]
    args += [g, w, scale]
    if from_state:
        h_shape = jax.ShapeDtypeStruct((n, d), F32)
        h_spec = pl.BlockSpec((tm, d), lambda i, gi: (i, 0))
    else:
        h_shape = jax.ShapeDtypeStruct((n // rows_per_seq, halo_rows, d), F32)
        h_spec = pl.BlockSpec((None, halo_rows, d), lambda i, gi: (i // tiles_per_seq, 0, 0))
    return pl.pallas_call(
        functools.partial(_pool_kernel, shift=shift, pos0=pos0, tiles_per_seq=tiles_per_seq,
                          from_state=from_state),
        grid=(n // tm, ng),
        in_specs=in_specs,
        out_specs=[pl.BlockSpec((tm, gw), lambda i, gi: (i, gi)), h_spec],
        out_shape=[jax.ShapeDtypeStruct((n, d), F32), h_shape],
        scratch_shapes=[pltpu.VMEM((tm, d), F32),
                        pltpu.VMEM((1 if from_state else ng, halo_rows + tm, gw), F32)],
        compiler_params=_params(2),
        name="pool_mixer_state" if from_state else "pool_mixer",
    )(*args)


def _ffn_kernel(*refs, shift, tiles_per_seq, from_state):
    if from_state:
        (x_ref, g_ref, sg_ref, su_ref, wg_ref, wu_ref, cwg_ref, cwu_ref, cbg_ref, cbu_ref, wd_ref,
         o_ref, tg_ref, tu_ref, hs_ref, ubuf_ref) = refs
        carry_g = carry_u = None
    else:
        (x_ref, g_ref, wg_ref, wu_ref, cwg_ref, cwu_ref, cbg_ref, cbu_ref, wd_ref,
         o_ref, tg_ref, tu_ref, hs_ref, ubuf_ref, carry_g, carry_u) = refs
        sg_ref = su_ref = None
    tm, d = x_ref.shape
    fn = wg_ref.shape[1]
    halo = max((CONV_W - 1) * shift, SUBLANES)
    i = pl.program_id(0)
    f = pl.program_id(1)
    j = i % tiles_per_seq

    @pl.when(f == 0)
    def _():
        x = x_ref[...]
        hs_ref[...] = _rmsnorm(x, g_ref[...]).astype(BF16)
        o_ref[...] = x

    hs = hs_ref[...]

    def conv_branch(w_ref, cw_ref, cb_ref, state_ref, carry_ref, tail_ref):
        u = jnp.dot(hs, w_ref[...], preferred_element_type=F32)
        if from_state:
            ubuf_ref[0:halo, :] = state_ref[...]
        else:
            @pl.when(j == 0)
            def _():
                ubuf_ref[0:halo, :] = jnp.zeros((halo, fn), F32)

            @pl.when(j != 0)
            def _():
                ubuf_ref[0:halo, :] = carry_ref[f]
            carry_ref[f] = u[tm - halo:, :]
        ubuf_ref[halo:halo + tm, :] = u
        tail_ref[...] = u[tm - halo:, :]
        ext = ubuf_ref[...]
        if shift % SUBLANES == 0:
            u1 = ext[halo - shift:halo - shift + tm, :]
            u2 = ext[halo - 2 * shift:halo - 2 * shift + tm, :]
        else:
            u1 = _shift_rows(ext, shift)[halo:halo + tm, :]
            u2 = _shift_rows(ext, 2 * shift)[halo:halo + tm, :]
        cw = cw_ref[...]
        return cw[0:1, :] * u2 + cw[1:2, :] * u1 + cw[2:3, :] * u + cb_ref[...]

    cg = conv_branch(wg_ref, cwg_ref, cbg_ref, sg_ref, carry_g, tg_ref)
    cu = conv_branch(wu_ref, cwu_ref, cbu_ref, su_ref, carry_u, tu_ref)
    act = (cg / (1.0 + jnp.exp(-cg))) * cu
    o_ref[...] += jnp.dot(act.astype(BF16), wd_ref[...], preferred_element_type=F32)


def _conv_ffn(x, g, state, w_up, cw, cb, w_down, *, shift, rows_per_seq):
    n, d = x.shape
    f2 = w_up.shape[1]
    ffn = f2 // 2
    fn = FFN_CHUNK
    nf = ffn // fn
    tm = min(TOKEN_TILE, n)
    n_tiles = n // tm
    tiles_per_seq = max(rows_per_seq // tm, 1)
    halo = max((CONV_W - 1) * shift, SUBLANES)
    from_state = state is not None

    tile = pl.BlockSpec((tm, d), lambda i, f: (i, 0))
    gate_cols = lambda rows: pl.BlockSpec((rows, fn), lambda i, f: (0, f))
    up_cols = lambda rows: pl.BlockSpec((rows, fn), lambda i, f: (0, f + nf))
    in_specs = [tile, pl.BlockSpec((1, d), lambda i, f: (0, 0))]
    args = [x, g]
    if from_state:
        in_specs += [gate_cols(halo), up_cols(halo)]
        args += [state, state]
    in_specs += [gate_cols(d), up_cols(d), gate_cols(CONV_W), up_cols(CONV_W), gate_cols(1), up_cols(1),
                 pl.BlockSpec((fn, d), lambda i, f: (f, 0))]
    args += [w_up, w_up, cw, cw, cb, cb, w_down]
    tail_spec = pl.BlockSpec((None, halo, fn), lambda i, f: (i, 0, f))
    tail_shape = jax.ShapeDtypeStruct((n_tiles, halo, ffn), F32)
    scratch = [pltpu.VMEM((tm, d), BF16), pltpu.VMEM((halo + tm, fn), F32)]
    if not from_state:
        scratch += [pltpu.VMEM((nf, halo, fn), F32), pltpu.VMEM((nf, halo, fn), F32)]
    return pl.pallas_call(
        functools.partial(_ffn_kernel, shift=shift, tiles_per_seq=tiles_per_seq, from_state=from_state),
        grid=(n_tiles, nf),
        in_specs=in_specs,
        out_specs=[tile, tail_spec, tail_spec],
        out_shape=[jax.ShapeDtypeStruct((n, d), F32), tail_shape, tail_shape],
        scratch_shapes=scratch,
        compiler_params=_params(2),
        name="conv_ffn_state" if from_state else "conv_ffn",
    )(*args)


def _ple_kernel(*refs, final):
    if final:
        x_ref, p_ref, g_ref, wg_ref, wp_ref, gf_ref, o_ref = refs
    else:
        x_ref, p_ref, g_ref, wg_ref, wp_ref, o_ref = refs
    x = x_ref[...]
    hn = _rmsnorm(x, g_ref[...]).astype(BF16)
    gate = jax.nn.sigmoid(jnp.dot(hn, wg_ref[...], preferred_element_type=F32))
    emb = jnp.dot(p_ref[...].astype(BF16), wp_ref[...], preferred_element_type=F32)
    y = x + gate * emb
    if final:
        y = _rmsnorm(y, gf_ref[...])
    o_ref[...] = y


def _ple(x, p, g, w_gate, w_proj, g_final):
    n, d = x.shape
    tm = min(TOKEN_TILE, n)
    final = g_final is not None
    const = lambda a: pl.BlockSpec(a.shape, lambda i: (0, 0))
    in_specs = [pl.BlockSpec((tm, d), lambda i: (i, 0)), pl.BlockSpec((tm, p.shape[1]), lambda i: (i, 0)),
                const(g), const(w_gate), const(w_proj)]
    args = [x, p, g, w_gate, w_proj]
    if final:
        in_specs.append(const(g_final))
        args.append(g_final)
    return pl.pallas_call(
        functools.partial(_ple_kernel, final=final),
        grid=(n // tm,),
        in_specs=in_specs,
        out_specs=pl.BlockSpec((tm, d), lambda i: (i, 0)),
        out_shape=jax.ShapeDtypeStruct((n, d), F32),
        compiler_params=_params(1),
        name="ple_final" if final else "ple",
    )(*args)


def _ret_in_kernel(x_ref, g_ref, w_ref, cos_ref, sin_ref, o_ref, hs_ref, *, qk_chunks, dk):
    c = pl.program_id(1)

    @pl.when(c == 0)
    def _():
        hs_ref[...] = _rmsnorm(x_ref[...], g_ref[...]).astype(BF16)

    proj = jnp.dot(hs_ref[...], w_ref[...], preferred_element_type=F32)

    @pl.when(c >= 2 * qk_chunks)
    def _():
        o_ref[...] = proj.astype(o_ref.dtype)

    @pl.when(c < 2 * qk_chunks)
    def _():
        scale = jnp.where(c >= qk_chunks, dk ** -0.5, 1.0).astype(F32)
        cos = cos_ref[...]
        sin = sin_ref[...]
        half = dk // 2
        for hh in range(proj.shape[1] // dk):
            x1 = proj[:, hh * dk:hh * dk + half]
            x2 = proj[:, hh * dk + half:(hh + 1) * dk]
            o_ref[:, hh * dk:hh * dk + half] = ((x1 * cos - x2 * sin) * scale).astype(o_ref.dtype)
            o_ref[:, hh * dk + half:(hh + 1) * dk] = ((x1 * sin + x2 * cos) * scale).astype(o_ref.dtype)


def _ret_in(x, g, w_in, cos, sin, *, qk_width, dk, out_dtype):
    n, d = x.shape
    e = w_in.shape[1]
    tm = min(TOKEN_TILE, n)
    nc = PROJ_CHUNK
    return pl.pallas_call(
        functools.partial(_ret_in_kernel, qk_chunks=qk_width // nc, dk=dk),
        grid=(n // tm, e // nc),
        in_specs=[pl.BlockSpec((tm, d), lambda i, c: (i, 0)),
                  pl.BlockSpec((1, d), lambda i, c: (0, 0)),
                  pl.BlockSpec((d, nc), lambda i, c: (0, c)),
                  pl.BlockSpec((tm, dk // 2), lambda i, c: (i, 0)),
                  pl.BlockSpec((tm, dk // 2), lambda i, c: (i, 0))],
        out_specs=pl.BlockSpec((tm, nc), lambda i, c: (i, c)),
        out_shape=jax.ShapeDtypeStruct((n, e), out_dtype),
        scratch_shapes=[pltpu.VMEM((tm, d), BF16)],
        compiler_params=_params(2),
        name="ret_in",
    )(x, g, w_in, cos, sin)


def _groupnorm_gate(o, gate):
    mu = jnp.mean(o, axis=-1, keepdims=True)
    var = jnp.mean(jnp.square(o - mu), axis=-1, keepdims=True)
    on = (o - mu) * lax.rsqrt(var + EPS)
    gf = gate.astype(F32)
    return (gf / (1.0 + jnp.exp(-gf))) * on


def _ret_prompt_kernel(q_ref, k_ref, v_ref, g_ref, intra_ref, cross_ref, kdec_ref, sdec_ref,
                       o_ref, s_out_ref, s_ref, *, chunk):
    t = q_ref.shape[0]
    s_ref[...] = jnp.zeros(s_ref.shape, F32)
    intra = intra_ref[...]
    cross = cross_ref[...]
    kdec = kdec_ref[...]
    sdec = sdec_ref[...]

    def body(c, carry):
        rows = pl.ds(pl.multiple_of(c * chunk, chunk), chunk)
        q = q_ref[rows, :]
        k = k_ref[rows, :]
        v = v_ref[rows, :]
        s = s_ref[...]
        sc = lax.dot_general(q, k, (((1,), (1,)), ((), ())), preferred_element_type=F32) * intra
        o = (jnp.dot(sc.astype(BF16), v, preferred_element_type=F32)
             + jnp.dot(q, s.astype(BF16), preferred_element_type=F32) * cross)
        kd = (k.astype(F32) * kdec).astype(BF16)
        s_ref[...] = sdec * s + lax.dot_general(kd, v, (((0,), (0,)), ((), ())),
                                                preferred_element_type=F32)
        o_ref[rows, :] = _groupnorm_gate(o, g_ref[rows, :]).astype(o_ref.dtype)
        return carry

    lax.fori_loop(0, t // chunk, body, 0)
    s_out_ref[...] = s_ref[...]


def _ret_prompt(proj, consts, *, batch, seq, heads, dk, dv):
    intra, cross, kdec, sdec = consts
    chunk = intra.shape[1]
    qk = heads * dk
    k_off = qk // dk
    v_off = 2 * qk // dv
    g_off = v_off + heads
    head_const = lambda a: pl.BlockSpec((None,) + a.shape[1:], lambda b, h: (h, 0, 0))
    return pl.pallas_call(
        functools.partial(_ret_prompt_kernel, chunk=chunk),
        grid=(batch, heads),
        in_specs=[pl.BlockSpec((seq, dk), lambda b, h: (b, h)),
                  pl.BlockSpec((seq, dk), lambda b, h: (b, k_off + h)),
                  pl.BlockSpec((seq, dv), lambda b, h: (b, v_off + h)),
                  pl.BlockSpec((seq, dv), lambda b, h: (b, g_off + h)),
                  head_const(intra), head_const(cross), head_const(kdec), head_const(sdec)],
        out_specs=[pl.BlockSpec((seq, dv), lambda b, h: (b, h)),
                   pl.BlockSpec((None, None, dk, dv), lambda b, h: (b, h, 0, 0))],
        out_shape=[jax.ShapeDtypeStruct((batch * seq, heads * dv), BF16),
                   jax.ShapeDtypeStruct((batch, heads, dk, dv), F32)],
        scratch_shapes=[pltpu.VMEM((dk, dv), F32)],
        compiler_params=_params(2),
        name="retention_prompt",
    )(proj, proj, proj, proj, intra, cross, kdec, sdec)


def _ret_sample_kernel(p_ref, s_ref, intra_ref, cross_ref, kdec_ref, sdec_ref, o_ref, s_out_ref,
                       *, heads, dk, dv):
    qk = heads * dk
    for h in range(heads):
        q = p_ref[:, h * dk:(h + 1) * dk].astype(BF16)
        kf = p_ref[:, qk + h * dk:qk + (h + 1) * dk]
        v = p_ref[:, 2 * qk + h * dv:2 * qk + (h + 1) * dv].astype(BF16)
        gate = p_ref[:, 2 * qk + heads * dv + h * dv:2 * qk + heads * dv + (h + 1) * dv]
        s = s_ref[h]
        sc = lax.dot_general(q, kf.astype(BF16), (((1,), (1,)), ((), ())),
                             preferred_element_type=F32) * intra_ref[h]
        o = (jnp.dot(sc.astype(BF16), v, preferred_element_type=F32)
             + jnp.dot(q, s.astype(BF16), preferred_element_type=F32) * cross_ref[h])
        kd = (kf * kdec_ref[h]).astype(BF16)
        s_out_ref[h] = sdec_ref[h] * s + lax.dot_general(kd, v, (((0,), (0,)), ((), ())),
                                                         preferred_element_type=F32)
        o_ref[:, h * dv:(h + 1) * dv] = _groupnorm_gate(o, gate).astype(o_ref.dtype)


def _ret_sample(proj, state, consts, *, heads, dk, dv):
    intra, cross, kdec, sdec = consts
    b, tp, e = proj.shape
    const = lambda a: pl.BlockSpec(a.shape, lambda i: (0,) * a.ndim)
    return pl.pallas_call(
        functools.partial(_ret_sample_kernel, heads=heads, dk=dk, dv=dv),
        grid=(b,),
        in_specs=[pl.BlockSpec((None, tp, e), lambda i: (i, 0, 0)),
                  pl.BlockSpec((None, heads, dk, dv), lambda i: (i, 0, 0, 0)),
                  const(intra), const(cross), const(kdec), const(sdec)],
        out_specs=[pl.BlockSpec((None, tp, heads * dv), lambda i: (i, 0, 0)),
                   pl.BlockSpec((None, heads, dk, dv), lambda i: (i, 0, 0, 0))],
        out_shape=[jax.ShapeDtypeStruct((b, tp, heads * dv), F32),
                   jax.ShapeDtypeStruct((b, heads, dk, dv), F32)],
        compiler_params=_params(1),
        name="retention_sample",
    )(proj, state, intra, cross, kdec, sdec)


def _ret_out_kernel(x_ref, a_ref, w_ref, o_ref):
    o_ref[...] = x_ref[...] + jnp.dot(a_ref[...].astype(BF16), w_ref[...], preferred_element_type=F32)


def _ret_out(x, a, w_out):
    n, d = x.shape
    v = a.shape[1]
    tm = min(TOKEN_TILE, n)
    nc = OUT_CHUNK
    return pl.pallas_call(
        _ret_out_kernel,
        grid=(n // tm, d // nc),
        in_specs=[pl.BlockSpec((tm, nc), lambda i, c: (i, c)),
                  pl.BlockSpec((tm, v), lambda i, c: (i, 0)),
                  pl.BlockSpec((v, nc), lambda i, c: (0, c))],
        out_specs=pl.BlockSpec((tm, nc), lambda i, c: (i, c)),
        out_shape=jax.ShapeDtypeStruct((n, d), F32),
        compiler_params=_params(2),
        name="ret_out",
    )(x, a, w_out)


def _decay_consts(chunk, rows, heads, dk, dv):
    log_g = jnp.log1p(-(2.0 ** (-5.0 - jnp.arange(heads, dtype=F32))))
    n = jnp.arange(rows, dtype=F32)
    live = n < chunk
    diff = n[:, None] - n[None, :]
    intra = jnp.where((diff >= 0) & live[:, None] & live[None, :],
                      jnp.exp(log_g[:, None, None] * jnp.maximum(diff, 0.0)), 0.0)
    cross = jnp.where(live, jnp.exp(log_g[:, None] * (n + 1.0)), 0.0)
    kdec = jnp.where(live, jnp.exp(log_g[:, None] * (chunk - 1.0 - n)), 0.0)
    sdec = jnp.exp(log_g * chunk)
    return (intra,
            jnp.broadcast_to(cross[:, :, None], (heads, rows, dv)),
            jnp.broadcast_to(kdec[:, :, None], (heads, rows, dk)),
            jnp.broadcast_to(sdec[:, None, None], (heads, 1, dv)))


def _rotary_tables(pos, dk):
    half = dk // 2
    inv = ROPE_BASE ** (-jnp.arange(half, dtype=F32) / half)
    ang = pos.astype(F32)[:, None] * inv[None, :]
    return jnp.cos(ang), jnp.sin(ang)


def kernel(x_prompt, x_sample, p_prompt, p_sample, state_pool, state_ret, state_conv, norm_mix, norm_ffn,
           norm_ple, norm_final, pool_w, pool_scale, ret_w_in, ret_w_out, ffn_w_up, ffn_conv_w,
           ffn_conv_b, ffn_w_down, ple_w_proj, ple_w_gate):
    b, t, d = x_prompt.shape
    bs, ts, _ = x_sample.shape
    depth = norm_mix.shape[0]
    heads = RET_HEADS
    dk = d // heads
    dv = 2 * dk
    qk = heads * dk
    f2 = ffn_w_up.shape[2]
    ffn = f2 // 2
    tp = SUBLANES
    assert ts <= tp and ts >= CONV_W - 1 and PAST_LEN >= POOL_BUF

    row2 = lambda a: a.reshape(1, -1)
    t_major = lambda a: jnp.swapaxes(a, 0, 1).reshape(-1, a.shape[-1])
    b_major = lambda a: jnp.swapaxes(a.reshape(-1, bs, a.shape[-1]), 0, 1)

    xp = x_prompt.reshape(b * t, d)
    xs = t_major(x_sample)

    cos_p, sin_p = _rotary_tables(jnp.tile(jnp.arange(t), b), dk)
    cos_s, sin_s = _rotary_tables(PAST_LEN + jnp.repeat(jnp.arange(ts), bs), dk)
    chunk_p = RET_CHUNK if t % RET_CHUNK == 0 else t
    consts_p = _decay_consts(chunk_p, chunk_p, heads, dk, dv)
    consts_s = _decay_consts(ts, tp, heads, dk, dv)

    new_pool_p, new_pool_s, new_ret_p, new_ret_s, new_conv_p, new_conv_s = [], [], [], [], [], []
    tiles_per_seq = t // TOKEN_TILE
    for i in range(depth):
        jm = i // 2
        if i % 2 == 0:
            w = pool_w[jm].astype(BF16)
            g = row2(norm_mix[i])
            sc = row2(pool_scale[jm])
            xp, hl = _pool_mixer(xp, None, g, w, sc, shift=1, pos0=0, rows_per_seq=t)
            new_pool_p.append(hl[:, -POOL_BUF:, :])
            halo_s = t_major(state_pool[jm])
            xs, h_s = _pool_mixer(xs, halo_s, g, w, sc, shift=bs, pos0=PAST_LEN, rows_per_seq=ts * bs)
            new_pool_s.append(jnp.concatenate([state_pool[jm], b_major(h_s)], axis=1)[:, -POOL_BUF:, :])
        else:
            w_in = ret_w_in[jm].astype(BF16)
            w_out = ret_w_out[jm].astype(BF16)
            g = row2(norm_mix[i])
            proj_p = _ret_in(xp, g, w_in, cos_p, sin_p, qk_width=qk, dk=dk, out_dtype=BF16)
            a_p, s_p = _ret_prompt(proj_p, consts_p, batch=b, seq=t, heads=heads, dk=dk, dv=dv)
            xp = _ret_out(xp, a_p, w_out)
            new_ret_p.append(s_p)
            proj_s = _ret_in(xs, g, w_in, cos_s, sin_s, qk_width=qk, dk=dk, out_dtype=F32)
            proj_s = jnp.pad(b_major(proj_s), ((0, 0), (0, tp - ts), (0, 0)))
            a_s, s_s = _ret_sample(proj_s, state_ret[jm], consts_s, heads=heads, dk=dk, dv=dv)
            xs = _ret_out(xs, t_major(a_s[:, :ts, :]), w_out)
            new_ret_s.append(s_s)

        w_up = ffn_w_up[i].astype(BF16)
        w_down = ffn_w_down[i].astype(BF16)
        g = row2(norm_ffn[i])
        cw = ffn_conv_w[i]
        cb = row2(ffn_conv_b[i])
        xp, tg, tu = _conv_ffn(xp, g, None, w_up, cw, cb, w_down, shift=1, rows_per_seq=t)
        last = jnp.concatenate([tg, tu], axis=-1)[tiles_per_seq - 1::tiles_per_seq]
        new_conv_p.append(last[:, -(CONV_W - 1):, :])
        st = t_major(state_conv[i])
        xs, tg, tu = _conv_ffn(xs, g, st, w_up, cw, cb, w_down, shift=bs, rows_per_seq=ts * bs)
        new_conv_s.append(b_major(jnp.concatenate([tg[0], tu[0]], axis=-1)))

        g = row2(norm_ple[i])
        w_gate = ple_w_gate[i].astype(BF16)
        w_proj = ple_w_proj[i].astype(BF16)
        g_final = row2(norm_final) if i == depth - 1 else None
        xp = _ple(xp, p_prompt[i].reshape(b * t, -1), g, w_gate, w_proj, g_final)
        xs = _ple(xs, t_major(p_sample[i]), g, w_gate, w_proj, g_final)

    return (xp.reshape(b, t, d), b_major(xs),
            jnp.stack(new_pool_p), jnp.stack(new_pool_s),
            jnp.stack(new_ret_p), jnp.stack(new_ret_s),
            jnp.stack(new_conv_p), jnp.stack(new_conv_s))
```

```python
import functools

import jax
import jax.numpy as jnp
import numpy as np
from jax import lax
from jax.experimental import pallas as pl
from jax.experimental.pallas import tpu as pltpu

F32 = jnp.float32
BF16 = jnp.bfloat16

EPS = 1e-6
POOL_WINDOWS = (2, 4, 8, 16)
POOL_BUF = max(POOL_WINDOWS) - 1
RET_HEADS = 8
RET_CHUNK = 128
ROPE_BASE = 10000.0
CONV_W = 3
PAST_LEN = 16384

TOKEN_TILE = 512
WIDE_TILE = 1024
FFN_STEP = 512 * 512
PROJ_STEP = 512 * 1024
SUBLANES = 8
VMEM_LIMIT = 60 * 1024 * 1024


def _params(n_axes):
    return pltpu.CompilerParams(
        dimension_semantics=("arbitrary",) * n_axes, vmem_limit_bytes=VMEM_LIMIT)


def _rmsnorm(x, g):
    return x * lax.rsqrt(jnp.mean(x * x, axis=-1, keepdims=True) + EPS) * g


def _shift_rows(a, rows):
    return pltpu.roll(a, rows, axis=0)


def _pool_kernel(*refs, shift, pos0, tiles_per_seq, from_state):
    if from_state:
        x_ref, halo_ref, g_ref, w_ref, sc_ref, o_ref, h_ref, hs_ref, ext_ref = refs
    else:
        x_ref, g_ref, w_ref, sc_ref, o_ref, h_ref, hs_ref, ext_ref = refs
    tm, d = x_ref.shape
    gw = o_ref.shape[1]
    halo = (POOL_BUF + 1) * shift
    j = pl.program_id(0) % tiles_per_seq
    grp = pl.program_id(1)

    @pl.when(grp == 0)
    def _():
        h = _rmsnorm(x_ref[...], g_ref[...])
        hs_ref[...] = h
        h_ref[...] = h if from_state else h[tm - halo:, :]

    row = lax.broadcasted_iota(jnp.int32, (tm, 1), 0)
    if shift == 1:
        step = j * tm + row
    else:
        step = jnp.zeros_like(row)
        for k in range(1, tm // shift):
            step = step + (row >= k * shift).astype(jnp.int32)
    pos = pos0 + step

    for gi, win in enumerate(POOL_WINDOWS):
        @pl.when(grp == gi)
        def _(gi=gi, win=win):
            cols = slice(gi * gw, (gi + 1) * gw)
            ext = ext_ref.at[0 if from_state else gi]
            h = hs_ref[:, cols]
            if from_state:
                ext[0:shift, :] = jnp.zeros((shift, gw), F32)
                ext[shift:halo, :] = halo_ref[...]
            else:
                @pl.when(j == 0)
                def _():
                    ext[0:halo, :] = jnp.zeros((halo, gw), F32)

                @pl.when(j != 0)
                def _():
                    ext[0:halo, :] = ext[tm:tm + halo, :]
            ext[halo:halo + tm, :] = h
            a = ext[...]
            span = 1
            while span < win:
                a = a + _shift_rows(a, span * shift)
                span *= 2
            cnt = jnp.minimum(pos + 1, win).astype(F32)
            dlt = a[halo:halo + tm, :] / cnt - h
            y = jnp.dot(dlt.astype(BF16), w_ref[...], preferred_element_type=F32)
            o_ref[...] = x_ref[:, cols] + y * sc_ref[...]


def _pool_mixer(x, halo, g, w, scale, *, shift, pos0, rows_per_seq):
    n, d = x.shape
    ng = len(POOL_WINDOWS)
    gw = d // ng
    tm = min(TOKEN_TILE, n)
    from_state = halo is not None
    tiles_per_seq = max(rows_per_seq // tm, 1)
    halo_rows = (POOL_BUF + 1) * shift
    in_specs = [pl.BlockSpec((tm, d), lambda i, gi: (i, 0))]
    args = [x]
    if from_state:
        in_specs.append(pl.BlockSpec((halo.shape[0], gw), lambda i, gi: (0, gi)))
        args.append(halo)
    in_specs += [pl.BlockSpec((1, d), lambda i, gi: (0, 0)),
                 pl.BlockSpec((None, gw, gw), lambda i, gi: (gi, 0, 0)),
                 pl.BlockSpec((1, gw), lambda i, gi: (0, gi))]
    args += [g, w, scale]
    if from_state:
        h_shape = jax.ShapeDtypeStruct((n, d), F32)
        h_spec = pl.BlockSpec((tm, d), lambda i, gi: (i, 0))
    else:
        h_shape = jax.ShapeDtypeStruct((n // rows_per_seq, halo_rows, d), F32)
        h_spec = pl.BlockSpec((None, halo_rows, d), lambda i, gi: (i // tiles_per_seq, 0, 0))
    return pl.pallas_call(
        functools.partial(_pool_kernel, shift=shift, pos0=pos0, tiles_per_seq=tiles_per_seq,
                          from_state=from_state),
        grid=(n // tm, ng),
        in_specs=in_specs,
        out_specs=[pl.BlockSpec((tm, gw), lambda i, gi: (i, gi)), h_spec],
        out_shape=[jax.ShapeDtypeStruct((n, d), F32), h_shape],
        scratch_shapes=[pltpu.VMEM((tm, d), F32),
                        pltpu.VMEM((1 if from_state else ng, halo_rows + tm, gw), F32)],
        compiler_params=_params(2),
        name="pool_mixer_state" if from_state else "pool_mixer",
    )(*args)


def _ffn_kernel(*refs, shift, tiles_per_seq, from_state):
    if from_state:
        (x_ref, g_ref, sg_ref, su_ref, wg_ref, wu_ref, cwg_ref, cwu_ref, cbg_ref, cbu_ref, wd_ref,
         o_ref, tg_ref, tu_ref, hs_ref) = refs
        carry_g = carry_u = None
    else:
        (x_ref, g_ref, wg_ref, wu_ref, cwg_ref, cwu_ref, cbg_ref, cbu_ref, wd_ref,
         o_ref, tg_ref, tu_ref, hs_ref, carry_g, carry_u) = refs
        sg_ref = su_ref = None
    tm, d = x_ref.shape
    halo = max((CONV_W - 1) * shift, SUBLANES)
    i = pl.program_id(0)
    f = pl.program_id(1)
    first_of_seq = (i % tiles_per_seq) == 0

    @pl.when(f == 0)
    def _():
        x = x_ref[...]
        hs_ref[...] = _rmsnorm(x, g_ref[...]).astype(BF16)
        o_ref[...] = x

    hs = hs_ref[...]

    def conv_branch(w_ref, cw_ref, cb_ref, state_ref, carry_ref, tail_ref):
        u = jnp.dot(hs, w_ref[...], preferred_element_type=F32)
        if from_state:
            prev = state_ref[...]
        else:
            prev = jnp.where(first_of_seq, 0.0, carry_ref[f])
            carry_ref[f] = u[tm - halo:, :]
        tail_ref[...] = u[tm - halo:, :]
        ext = jnp.concatenate([prev, u], axis=0)
        if shift % SUBLANES == 0:
            u1 = ext[halo - shift:halo - shift + tm, :]
            u2 = ext[halo - 2 * shift:halo - 2 * shift + tm, :]
        else:
            u1 = _shift_rows(ext, shift)[halo:halo + tm, :]
            u2 = _shift_rows(ext, 2 * shift)[halo:halo + tm, :]
        cw = cw_ref[...]
        return cw[0:1, :] * u2 + cw[1:2, :] * u1 + cw[2:3, :] * u + cb_ref[...]

    cg = conv_branch(wg_ref, cwg_ref, cbg_ref, sg_ref, carry_g, tg_ref)
    cu = conv_branch(wu_ref, cwu_ref, cbu_ref, su_ref, carry_u, tu_ref)
    act = ((cg / (1.0 + jnp.exp(-cg))) * cu).astype(BF16)
    o_ref[...] += jnp.dot(act, wd_ref[...], preferred_element_type=F32)


def _conv_ffn(x, g, state, w_up, cw, cb, w_down, *, shift, rows_per_seq):
    n, d = x.shape
    f2 = w_up.shape[1]
    ffn = f2 // 2
    tm = min(WIDE_TILE, n)
    fn = FFN_STEP // tm
    nf = ffn // fn
    n_tiles = n // tm
    tiles_per_seq = max(rows_per_seq // tm, 1)
    halo = max((CONV_W - 1) * shift, SUBLANES)
    from_state = state is not None

    tile = pl.BlockSpec((tm, d), lambda i, f: (i, 0))
    gate_cols = lambda rows: pl.BlockSpec((rows, fn), lambda i, f: (0, f))
    up_cols = lambda rows: pl.BlockSpec((rows, fn), lambda i, f: (0, f + nf))
    in_specs = [tile, pl.BlockSpec((1, d), lambda i, f: (0, 0))]
    args = [x, g]
    if from_state:
        in_specs += [gate_cols(halo), up_cols(halo)]
        args += [state, state]
    in_specs += [gate_cols(d), up_cols(d), gate_cols(CONV_W), up_cols(CONV_W), gate_cols(1), up_cols(1),
                 pl.BlockSpec((fn, d), lambda i, f: (f, 0))]
    args += [w_up, w_up, cw, cw, cb, cb, w_down]
    tail_spec = pl.BlockSpec((None, halo, fn), lambda i, f: (i, 0, f))
    tail_shape = jax.ShapeDtypeStruct((n_tiles, halo, ffn), F32)
    scratch = [pltpu.VMEM((tm, d), BF16)]
    if not from_state:
        scratch += [pltpu.VMEM((nf, halo, fn), F32), pltpu.VMEM((nf, halo, fn), F32)]
    return pl.pallas_call(
        functools.partial(_ffn_kernel, shift=shift, tiles_per_seq=tiles_per_seq, from_state=from_state),
        grid=(n_tiles, nf),
        in_specs=in_specs,
        out_specs=[tile, tail_spec, tail_spec],
        out_shape=[jax.ShapeDtypeStruct((n, d), F32), tail_shape, tail_shape],
        scratch_shapes=scratch,
        compiler_params=_params(2),
        name="conv_ffn_state" if from_state else "conv_ffn",
    )(*args)


def _ple_kernel(*refs, final):
    if final:
        x_ref, p_ref, g_ref, wg_ref, wp_ref, gf_ref, o_ref = refs
    else:
        x_ref, p_ref, g_ref, wg_ref, wp_ref, o_ref = refs
    x = x_ref[...]
    hn = _rmsnorm(x, g_ref[...]).astype(BF16)
    gate = jax.nn.sigmoid(jnp.dot(hn, wg_ref[...], preferred_element_type=F32))
    emb = jnp.dot(p_ref[...].astype(BF16), wp_ref[...], preferred_element_type=F32)
    y = x + gate * emb
    if final:
        y = _rmsnorm(y, gf_ref[...])
    o_ref[...] = y


def _ple(x, p, g, w_gate, w_proj, g_final):
    n, d = x.shape
    tm = min(TOKEN_TILE, n)
    final = g_final is not None
    const = lambda a: pl.BlockSpec(a.shape, lambda i: (0, 0))
    resident = lambda a: pl.BlockSpec(a.shape, lambda i: (0, 0), pipeline_mode=pl.Buffered(1))
    in_specs = [pl.BlockSpec((tm, d), lambda i: (i, 0)), pl.BlockSpec((tm, p.shape[1]), lambda i: (i, 0)),
                const(g), resident(w_gate), resident(w_proj)]
    args = [x, p, g, w_gate, w_proj]
    if final:
        in_specs.append(const(g_final))
        args.append(g_final)
    return pl.pallas_call(
        functools.partial(_ple_kernel, final=final),
        grid=(n // tm,),
        in_specs=in_specs,
        out_specs=pl.BlockSpec((tm, d), lambda i: (i, 0)),
        out_shape=jax.ShapeDtypeStruct((n, d), F32),
        compiler_params=_params(1),
        name="ple_final" if final else "ple",
    )(*args)


def _ret_in_kernel(x_ref, g_ref, w_ref, cos_ref, sin_ref, o_ref, hs_ref, *, dk):
    @pl.when(pl.program_id(1) == 0)
    def _():
        hs_ref[...] = _rmsnorm(x_ref[...], g_ref[...]).astype(BF16)

    proj = jnp.dot(hs_ref[...], w_ref[...], preferred_element_type=F32)
    cos = cos_ref[...]
    sin = sin_ref[...]
    half = dk // 2
    for hh in range(proj.shape[1] // dk):
        x1 = proj[:, hh * dk:hh * dk + half]
        x2 = proj[:, hh * dk + half:(hh + 1) * dk]
        o_ref[:, hh * dk:hh * dk + half] = (x1 * cos - x2 * sin).astype(o_ref.dtype)
        o_ref[:, hh * dk + half:(hh + 1) * dk] = (x1 * sin + x2 * cos).astype(o_ref.dtype)


def _ret_in(x, g, w_in, tables, *, qk_width, dk, rows_per_seq, out_dtype):
    n, d = x.shape
    e = w_in.shape[1]
    tm = min(WIDE_TILE, n)
    nc = PROJ_STEP // tm
    qc = qk_width // nc
    tps = rows_per_seq // tm

    def table_block(i, c):
        return (jnp.where(c < qc, 0, jnp.where(c < 2 * qc, tps, 2 * tps)) + i % tps, 0)

    return pl.pallas_call(
        functools.partial(_ret_in_kernel, dk=dk),
        grid=(n // tm, e // nc),
        in_specs=[pl.BlockSpec((tm, d), lambda i, c: (i, 0)),
                  pl.BlockSpec((1, d), lambda i, c: (0, 0)),
                  pl.BlockSpec((d, nc), lambda i, c: (0, c)),
                  pl.BlockSpec((tm, dk // 2), table_block),
                  pl.BlockSpec((tm, dk // 2), table_block)],
        out_specs=pl.BlockSpec((tm, nc), lambda i, c: (i, c)),
        out_shape=jax.ShapeDtypeStruct((n, e), out_dtype),
        scratch_shapes=[pltpu.VMEM((tm, d), BF16)],
        compiler_params=_params(2),
        name="ret_in",
    )(x, g, w_in, *tables)


def _groupnorm_gate(o, gate):
    mu = jnp.mean(o, axis=-1, keepdims=True)
    var = jnp.mean(jnp.square(o - mu), axis=-1, keepdims=True)
    on = (o - mu) * lax.rsqrt(var + EPS)
    gf = gate.astype(F32)
    return (gf / (1.0 + jnp.exp(-gf))) * on


def _ret_prompt_kernel(q_ref, k_ref, v_ref, g_ref, intra_ref, cross_ref, kdec_ref, sdec_ref,
                       o_ref, s_out_ref, s_ref, *, chunk):
    t = q_ref.shape[0]
    s_ref[...] = jnp.zeros(s_ref.shape, F32)
    intra = intra_ref[...]
    cross = cross_ref[...]
    kdec = kdec_ref[...]
    sdec = sdec_ref[...]

    def body(c, carry):
        rows = pl.ds(pl.multiple_of(c * chunk, chunk), chunk)
        q = q_ref[rows, :]
        k = k_ref[rows, :]
        v = v_ref[rows, :]
        s = s_ref[...]
        sc = lax.dot_general(q, k, (((1,), (1,)), ((), ())), preferred_element_type=F32) * intra
        o = (jnp.dot(sc.astype(BF16), v, preferred_element_type=F32)
             + jnp.dot(q, s.astype(BF16), preferred_element_type=F32) * cross)
        kd = (k.astype(F32) * kdec).astype(BF16)
        s_ref[...] = sdec * s + lax.dot_general(kd, v, (((0,), (0,)), ((), ())),
                                                preferred_element_type=F32)
        o_ref[rows, :] = _groupnorm_gate(o, g_ref[rows, :]).astype(o_ref.dtype)
        return carry

    lax.fori_loop(0, t // chunk, body, 0, unroll=4)
    s_out_ref[...] = s_ref[...]


def _ret_prompt(proj, consts, *, batch, seq, heads, dk, dv):
    intra, cross, kdec, sdec = consts
    chunk = intra.shape[1]
    qk = heads * dk
    k_off = qk // dk
    v_off = 2 * qk // dv
    g_off = v_off + heads
    head_const = lambda a: pl.BlockSpec((None,) + a.shape[1:], lambda b, h: (h, 0, 0))
    return pl.pallas_call(
        functools.partial(_ret_prompt_kernel, chunk=chunk),
        grid=(batch, heads),
        in_specs=[pl.BlockSpec((seq, dk), lambda b, h: (b, h)),
                  pl.BlockSpec((seq, dk), lambda b, h: (b, k_off + h)),
                  pl.BlockSpec((seq, dv), lambda b, h: (b, v_off + h)),
                  pl.BlockSpec((seq, dv), lambda b, h: (b, g_off + h)),
                  head_const(intra), head_const(cross), head_const(kdec), head_const(sdec)],
        out_specs=[pl.BlockSpec((seq, dv), lambda b, h: (b, h)),
                   pl.BlockSpec((None, None, dk, dv), lambda b, h: (b, h, 0, 0))],
        out_shape=[jax.ShapeDtypeStruct((batch * seq, heads * dv), BF16),
                   jax.ShapeDtypeStruct((batch, heads, dk, dv), F32)],
        scratch_shapes=[pltpu.VMEM((dk, dv), F32)],
        compiler_params=_params(2),
        name="retention_prompt",
    )(proj, proj, proj, proj, intra, cross, kdec, sdec)


def _ret_sample_kernel(p_ref, s_ref, intra_ref, cross_ref, kdec_ref, sdec_ref, o_ref, s_out_ref,
                       *, heads, dk, dv):
    qk = heads * dk
    for h in range(heads):
        q = p_ref[:, h * dk:(h + 1) * dk].astype(BF16)
        kf = p_ref[:, qk + h * dk:qk + (h + 1) * dk]
        v = p_ref[:, 2 * qk + h * dv:2 * qk + (h + 1) * dv].astype(BF16)
        gate = p_ref[:, 2 * qk + heads * dv + h * dv:2 * qk + heads * dv + (h + 1) * dv]
        s = s_ref[h]
        sc = lax.dot_general(q, kf.astype(BF16), (((1,), (1,)), ((), ())),
                             preferred_element_type=F32) * intra_ref[h]
        o = (jnp.dot(sc.astype(BF16), v, preferred_element_type=F32)
             + jnp.dot(q, s.astype(BF16), preferred_element_type=F32) * cross_ref[h])
        kd = (kf * kdec_ref[h]).astype(BF16)
        s_out_ref[h] = sdec_ref[h] * s + lax.dot_general(kd, v, (((0,), (0,)), ((), ())),
                                                         preferred_element_type=F32)
        o_ref[:, h * dv:(h + 1) * dv] = _groupnorm_gate(o, gate).astype(o_ref.dtype)


def _ret_sample(proj, state, consts, *, heads, dk, dv):
    intra, cross, kdec, sdec = consts
    b, tp, e = proj.shape
    const = lambda a: pl.BlockSpec(a.shape, lambda i: (0,) * a.ndim)
    return pl.pallas_call(
        functools.partial(_ret_sample_kernel, heads=heads, dk=dk, dv=dv),
        grid=(b,),
        in_specs=[pl.BlockSpec((None, tp, e), lambda i: (i, 0, 0)),
                  pl.BlockSpec((None, heads, dk, dv), lambda i: (i, 0, 0, 0)),
                  const(intra), const(cross), const(kdec), const(sdec)],
        out_specs=[pl.BlockSpec((None, tp, heads * dv), lambda i: (i, 0, 0)),
                   pl.BlockSpec((None, heads, dk, dv), lambda i: (i, 0, 0, 0))],
        out_shape=[jax.ShapeDtypeStruct((b, tp, heads * dv), F32),
                   jax.ShapeDtypeStruct((b, heads, dk, dv), F32)],
        compiler_params=_params(1),
        name="retention_sample",
    )(proj, state, intra, cross, kdec, sdec)


def _ret_out_kernel(x_ref, a_ref, w_ref, o_ref):
    o_ref[...] = x_ref[...] + jnp.dot(a_ref[...].astype(BF16), w_ref[...], preferred_element_type=F32)


def _ret_out(x, a, w_out):
    n, d = x.shape
    v = a.shape[1]
    tm = min(TOKEN_TILE, n)
    return pl.pallas_call(
        _ret_out_kernel,
        grid=(n // tm,),
        in_specs=[pl.BlockSpec((tm, d), lambda i: (i, 0)),
                  pl.BlockSpec((tm, v), lambda i: (i, 0)),
                  pl.BlockSpec((v, d), lambda i: (0, 0), pipeline_mode=pl.Buffered(1))],
        out_specs=pl.BlockSpec((tm, d), lambda i: (i, 0)),
        out_shape=jax.ShapeDtypeStruct((n, d), F32),
        compiler_params=_params(1),
        name="ret_out",
    )(x, a, w_out)


def _decay_consts(chunk, rows, heads, dk, dv):
    log_g = np.log1p(-(2.0 ** (-5.0 - np.arange(heads, dtype=np.float64))))
    n = np.arange(rows, dtype=np.float64)
    live = n < chunk
    diff = n[:, None] - n[None, :]
    intra = np.where((diff >= 0) & live[:, None] & live[None, :],
                     np.exp(log_g[:, None, None] * np.maximum(diff, 0.0)), 0.0)
    cross = np.where(live, np.exp(log_g[:, None] * (n + 1.0)), 0.0)
    kdec = np.where(live, np.exp(log_g[:, None] * (chunk - 1.0 - n)), 0.0)
    sdec = np.exp(log_g * chunk)
    f32 = lambda a: jnp.asarray(np.ascontiguousarray(a), dtype=F32)
    return (f32(intra),
            f32(np.broadcast_to(cross[:, :, None], (heads, rows, dv))),
            f32(np.broadcast_to(kdec[:, :, None], (heads, rows, dk))),
            f32(np.broadcast_to(sdec[:, None, None], (heads, 1, dv))))


def _rotary_tables(pos, dk):
    half = dk // 2
    inv = ROPE_BASE ** (-np.arange(half, dtype=np.float64) / half)
    ang = np.asarray(pos, dtype=np.float64)[:, None] * inv[None, :]
    cos, sin = np.cos(ang), np.sin(ang)
    scale = dk ** -0.5
    f32 = lambda a: jnp.asarray(a, dtype=F32)
    return (f32(np.concatenate([cos, cos * scale, np.ones_like(cos)])),
            f32(np.concatenate([sin, sin * scale, np.zeros_like(sin)])))


def kernel(x_prompt, x_sample, p_prompt, p_sample, state_pool, state_ret, state_conv, norm_mix, norm_ffn,
           norm_ple, norm_final, pool_w, pool_scale, ret_w_in, ret_w_out, ffn_w_up, ffn_conv_w,
           ffn_conv_b, ffn_w_down, ple_w_proj, ple_w_gate):
    b, t, d = x_prompt.shape
    bs, ts, _ = x_sample.shape
    depth = norm_mix.shape[0]
    heads = RET_HEADS
    dk = d // heads
    dv = 2 * dk
    qk = heads * dk
    f2 = ffn_w_up.shape[2]
    ffn = f2 // 2
    tp = SUBLANES
    assert ts <= tp and ts >= CONV_W - 1 and PAST_LEN >= POOL_BUF

    row2 = lambda a: a.reshape(1, -1)
    t_major = lambda a: jnp.swapaxes(a, 0, 1).reshape(-1, a.shape[-1])
    b_major = lambda a: jnp.swapaxes(a.reshape(-1, bs, a.shape[-1]), 0, 1)

    xp = x_prompt.reshape(b * t, d)
    xs = t_major(x_sample)

    tables_p = _rotary_tables(np.arange(t), dk)
    tables_s = _rotary_tables(PAST_LEN + np.repeat(np.arange(ts), bs), dk)
    chunk_p = RET_CHUNK if t % RET_CHUNK == 0 else t
    consts_p = _decay_consts(chunk_p, chunk_p, heads, dk, dv)
    consts_s = _decay_consts(ts, tp, heads, dk, dv)

    new_pool_p, new_pool_s, new_ret_p, new_ret_s, new_conv_p, new_conv_s = [], [], [], [], [], []
    for i in range(depth):
        jm = i // 2
        if i % 2 == 0:
            w = pool_w[jm]
            g = row2(norm_mix[i])
            sc = row2(pool_scale[jm])
            xp, hl = _pool_mixer(xp, None, g, w, sc, shift=1, pos0=0, rows_per_seq=t)
            new_pool_p.append(hl[:, -POOL_BUF:, :])
            halo_s = t_major(state_pool[jm])
            xs, h_s = _pool_mixer(xs, halo_s, g, w, sc, shift=bs, pos0=PAST_LEN, rows_per_seq=ts * bs)
            new_pool_s.append(jnp.concatenate([state_pool[jm], b_major(h_s)], axis=1)[:, -POOL_BUF:, :])
        else:
            w_in = ret_w_in[jm]
            w_out = ret_w_out[jm].astype(BF16)
            g = row2(norm_mix[i])
            proj_p = _ret_in(xp, g, w_in, tables_p, qk_width=qk, dk=dk, rows_per_seq=t, out_dtype=BF16)
            a_p, s_p = _ret_prompt(proj_p, consts_p, batch=b, seq=t, heads=heads, dk=dk, dv=dv)
            xp = _ret_out(xp, a_p, w_out)
            new_ret_p.append(s_p)
            proj_s = _ret_in(xs, g, w_in, tables_s, qk_width=qk, dk=dk, rows_per_seq=ts * bs, out_dtype=F32)
            proj_s = jnp.pad(b_major(proj_s), ((0, 0), (0, tp - ts), (0, 0)))
            a_s, s_s = _ret_sample(proj_s, state_ret[jm], consts_s, heads=heads, dk=dk, dv=dv)
            xs = _ret_out(xs, t_major(a_s[:, :ts, :]), w_out)
            new_ret_s.append(s_s)

        g = row2(norm_ffn[i])
        cw = ffn_conv_w[i]
        cb = row2(ffn_conv_b[i])
        xp, tg, tu = _conv_ffn(xp, g, None, ffn_w_up[i], cw, cb, ffn_w_down[i], shift=1, rows_per_seq=t)
        tiles_per_seq = tg.shape[0] // b
        last = jnp.concatenate([tg, tu], axis=-1)[tiles_per_seq - 1::tiles_per_seq]
        new_conv_p.append(last[:, -(CONV_W - 1):, :])
        st = t_major(state_conv[i])
        xs, tg, tu = _conv_ffn(xs, g, st, ffn_w_up[i], cw, cb, ffn_w_down[i], shift=bs, rows_per_seq=ts * bs)
        new_conv_s.append(b_major(jnp.concatenate([tg[0], tu[0]], axis=-1)))

        g = row2(norm_ple[i])
        g_final = row2(norm_final) if i == depth - 1 else None
        xp = _ple(xp, p_prompt[i].reshape(b * t, -1), g, ple_w_gate[i], ple_w_proj[i], g_final)
        xs = _ple(xs, t_major(p_sample[i]), g, ple_w_gate[i], ple_w_proj[i], g_final)

    return (xp.reshape(b, t, d), b_major(xs),
            jnp.stack(new_pool_p), jnp.stack(new_pool_s),
            jnp.stack(new_ret_p), jnp.stack(new_ret_s),
            jnp.stack(new_conv_p), jnp.stack(new_conv_s))
```

```python
import functools

import jax
import jax.numpy as jnp
import numpy as np
from jax import lax
from jax.experimental import pallas as pl
from jax.experimental.pallas import tpu as pltpu

F32 = jnp.float32
BF16 = jnp.bfloat16

EPS = 1e-6
POOL_WINDOWS = (2, 4, 8, 16)
POOL_BUF = max(POOL_WINDOWS) - 1
RET_HEADS = 8
RET_CHUNK = 128
ROPE_BASE = 10000.0
CONV_W = 3
PAST_LEN = 16384

TOKEN_TILE = 512
WIDE_TILE = 1024
FFN_CHUNK = 512
ROW_BLOCK = 512
PROJ_CHUNK = 1024
SUBLANES = 8
VMEM_LIMIT = 60 * 1024 * 1024


def _params(n_axes):
    return pltpu.CompilerParams(
        dimension_semantics=("arbitrary",) * n_axes, vmem_limit_bytes=VMEM_LIMIT)


def _rmsnorm(x, g):
    return x * lax.rsqrt(jnp.mean(x * x, axis=-1, keepdims=True) + EPS) * g


def _shift_rows(a, rows):
    return pltpu.roll(a, rows, axis=0)


def _pool_kernel(*refs, shift, pos0, tiles_per_seq, from_state):
    if from_state:
        x_ref, halo_ref, g_ref, w_ref, sc_ref, o_ref, h_ref, hs_ref, ext_ref = refs
    else:
        x_ref, g_ref, w_ref, sc_ref, o_ref, h_ref, hs_ref, ext_ref = refs
    tm, d = x_ref.shape
    gw = o_ref.shape[1]
    halo = (POOL_BUF + 1) * shift
    j = pl.program_id(0) % tiles_per_seq
    grp = pl.program_id(1)

    @pl.when(grp == 0)
    def _():
        h = _rmsnorm(x_ref[...], g_ref[...])
        hs_ref[...] = h
        h_ref[...] = h if from_state else h[tm - halo:, :]

    row = lax.broadcasted_iota(jnp.int32, (tm, 1), 0)
    if shift == 1:
        step = j * tm + row
    else:
        step = jnp.zeros_like(row)
        for k in range(1, tm // shift):
            step = step + (row >= k * shift).astype(jnp.int32)
    pos = pos0 + step

    for gi, win in enumerate(POOL_WINDOWS):
        @pl.when(grp == gi)
        def _(gi=gi, win=win):
            cols = slice(gi * gw, (gi + 1) * gw)
            ext = ext_ref.at[0 if from_state else gi]
            h = hs_ref[:, cols]
            if from_state:
                ext[0:shift, :] = jnp.zeros((shift, gw), F32)
                ext[shift:halo, :] = halo_ref[...]
            else:
                @pl.when(j == 0)
                def _():
                    ext[0:halo, :] = jnp.zeros((halo, gw), F32)

                @pl.when(j != 0)
                def _():
                    ext[0:halo, :] = ext[tm:tm + halo, :]
            ext[halo:halo + tm, :] = h
            a = ext[...]
            span = 1
            while span < win:
                a = a + _shift_rows(a, span * shift)
                span *= 2
            cnt = jnp.minimum(pos + 1, win).astype(F32)
            dlt = a[halo:halo + tm, :] / cnt - h
            y = jnp.dot(dlt.astype(BF16), w_ref[...], preferred_element_type=F32)
            o_ref[...] = x_ref[:, cols] + y * sc_ref[...]


def _pool_mixer(x, halo, g, w, scale, *, shift, pos0, rows_per_seq):
    n, d = x.shape
    ng = len(POOL_WINDOWS)
    gw = d // ng
    tm = min(TOKEN_TILE, n)
    from_state = halo is not None
    tiles_per_seq = max(rows_per_seq // tm, 1)
    halo_rows = (POOL_BUF + 1) * shift
    in_specs = [pl.BlockSpec((tm, d), lambda i, gi: (i, 0))]
    args = [x]
    if from_state:
        in_specs.append(pl.BlockSpec((halo.shape[0], gw), lambda i, gi: (0, gi)))
        args.append(halo)
    in_specs += [pl.BlockSpec((1, d), lambda i, gi: (0, 0)),
                 pl.BlockSpec((None, gw, gw), lambda i, gi: (gi, 0, 0)),
                 pl.BlockSpec((1, gw), lambda i, gi: (0, gi))]
    args += [g, w, scale]
    if from_state:
        h_shape = jax.ShapeDtypeStruct((n, d), F32)
        h_spec = pl.BlockSpec((tm, d), lambda i, gi: (i, 0))
    else:
        h_shape = jax.ShapeDtypeStruct((n // rows_per_seq, halo_rows, d), F32)
        h_spec = pl.BlockSpec((None, halo_rows, d), lambda i, gi: (i // tiles_per_seq, 0, 0))
    return pl.pallas_call(
        functools.partial(_pool_kernel, shift=shift, pos0=pos0, tiles_per_seq=tiles_per_seq,
                          from_state=from_state),
        grid=(n // tm, ng),
        in_specs=in_specs,
        out_specs=[pl.BlockSpec((tm, gw), lambda i, gi: (i, gi)), h_spec],
        out_shape=[jax.ShapeDtypeStruct((n, d), F32), h_shape],
        scratch_shapes=[pltpu.VMEM((tm, d), F32),
                        pltpu.VMEM((1 if from_state else ng, halo_rows + tm, gw), F32)],
        compiler_params=_params(2),
        name="pool_mixer_state" if from_state else "pool_mixer",
    )(*args)


def _ffn_kernel(*refs, shift, tiles_per_seq, from_state):
    if from_state:
        (x_ref, g_ref, sg_ref, su_ref, wg_ref, wu_ref, cwg_ref, cwu_ref, cbg_ref, cbu_ref, wd_ref,
         o_ref, tg_ref, tu_ref, hs_ref) = refs
        carry_g = carry_u = None
    else:
        (x_ref, g_ref, wg_ref, wu_ref, cwg_ref, cwu_ref, cbg_ref, cbu_ref, wd_ref,
         o_ref, tg_ref, tu_ref, hs_ref, carry_g, carry_u) = refs
        sg_ref = su_ref = None
    tm, d = x_ref.shape
    halo = max((CONV_W - 1) * shift, SUBLANES)
    i = pl.program_id(0)
    f = pl.program_id(1)
    first_of_seq = (i % tiles_per_seq) == 0

    @pl.when(f == 0)
    def _():
        x = x_ref[...]
        hs_ref[...] = _rmsnorm(x, g_ref[...]).astype(BF16)
        o_ref[...] = x

    rb = min(ROW_BLOCK, tm)
    cwg, cwu, cbg, cbu = cwg_ref[...], cwu_ref[...], cbg_ref[...], cbu_ref[...]
    if from_state:
        assert rb == tm
        prev_g, prev_u = sg_ref[...], su_ref[...]
    else:
        prev_g = jnp.where(first_of_seq, 0.0, carry_g[f])
        prev_u = jnp.where(first_of_seq, 0.0, carry_u[f])

    def up(r):
        hs = hs_ref[r:r + rb, :]
        return (jnp.dot(hs, wg_ref[...], preferred_element_type=F32),
                jnp.dot(hs, wu_ref[...], preferred_element_type=F32))

    def conv(prev, u, cw, cb):
        ext = jnp.concatenate([prev, u], axis=0)
        if shift % SUBLANES == 0:
            u1 = ext[halo - shift:halo - shift + rb, :]
            u2 = ext[halo - 2 * shift:halo - 2 * shift + rb, :]
        else:
            u1 = _shift_rows(ext, shift)[halo:, :]
            u2 = _shift_rows(ext, 2 * shift)[halo:, :]
        return cw[0:1, :] * u2 + cw[1:2, :] * u1 + cw[2:3, :] * u + cb

    nxt = up(0)
    for r in range(0, tm, rb):
        ug, uu = nxt
        if r + rb < tm:
            nxt = up(r + rb)
        cg = conv(prev_g, ug, cwg, cbg)
        cu = conv(prev_u, uu, cwu, cbu)
        prev_g, prev_u = ug[rb - halo:, :], uu[rb - halo:, :]
        act = ((cg / (1.0 + jnp.exp(-cg))) * cu).astype(BF16)
        o_ref[r:r + rb, :] += jnp.dot(act, wd_ref[...], preferred_element_type=F32)
    if not from_state:
        carry_g[f] = prev_g
        carry_u[f] = prev_u
    tg_ref[...] = prev_g
    tu_ref[...] = prev_u


def _conv_ffn(x, g, state, w_up, cw, cb, w_down, layer, *, shift, rows_per_seq):
    n, d = x.shape
    f2 = w_up.shape[2]
    ffn = f2 // 2
    tm = min(WIDE_TILE, n)
    fn = FFN_CHUNK
    nf = ffn // fn
    n_tiles = n // tm
    tiles_per_seq = max(rows_per_seq // tm, 1)
    halo = max((CONV_W - 1) * shift, SUBLANES)
    from_state = state is not None

    tile = pl.BlockSpec((tm, d), lambda i, f: (i, 0))
    tile_in = pl.BlockSpec((tm, d), lambda i, f: (i, 0), pipeline_mode=pl.Buffered(1))
    gate_cols = lambda rows: pl.BlockSpec((rows, fn), lambda i, f: (0, f))
    up_cols = lambda rows: pl.BlockSpec((rows, fn), lambda i, f: (0, f + nf))
    layer_gate = lambda rows: pl.BlockSpec((None, rows, fn), lambda i, f: (layer, 0, f))
    layer_up = lambda rows: pl.BlockSpec((None, rows, fn), lambda i, f: (layer, 0, f + nf))
    in_specs = [tile_in, pl.BlockSpec((1, d), lambda i, f: (0, 0))]
    args = [x, g]
    if from_state:
        in_specs += [gate_cols(halo), up_cols(halo)]
        args += [state, state]
    in_specs += [layer_gate(d), layer_up(d), layer_gate(CONV_W), layer_up(CONV_W), gate_cols(1), up_cols(1),
                 pl.BlockSpec((None, fn, d), lambda i, f: (layer, f, 0))]
    args += [w_up, w_up, cw, cw, cb, cb, w_down]
    tail_spec = pl.BlockSpec((None, halo, fn), lambda i, f: (i, 0, f))
    tail_shape = jax.ShapeDtypeStruct((n_tiles, halo, ffn), F32)
    scratch = [pltpu.VMEM((tm, d), BF16)]
    if not from_state:
        scratch += [pltpu.VMEM((nf, halo, fn), F32), pltpu.VMEM((nf, halo, fn), F32)]
    return pl.pallas_call(
        functools.partial(_ffn_kernel, shift=shift, tiles_per_seq=tiles_per_seq, from_state=from_state),
        grid=(n_tiles, nf),
        in_specs=in_specs,
        out_specs=[tile, tail_spec, tail_spec],
        out_shape=[jax.ShapeDtypeStruct((n, d), F32), tail_shape, tail_shape],
        scratch_shapes=scratch,
        compiler_params=_params(2),
        name="conv_ffn_state" if from_state else "conv_ffn",
    )(*args)


def _ple_kernel(*refs, final):
    if final:
        x_ref, p_ref, g_ref, wg_ref, wp_ref, gf_ref, o_ref = refs
    else:
        x_ref, p_ref, g_ref, wg_ref, wp_ref, o_ref = refs
    x = x_ref[...]
    hn = _rmsnorm(x, g_ref[...]).astype(BF16)
    gate = jax.nn.sigmoid(jnp.dot(hn, wg_ref[...], preferred_element_type=F32))
    emb = jnp.dot(p_ref[...].astype(BF16), wp_ref[...], preferred_element_type=F32)
    y = x + gate * emb
    if final:
        y = _rmsnorm(y, gf_ref[...])
    o_ref[...] = y


def _ple(x, p, g, w_gate, w_proj, layer, g_final):
    n, d = x.shape
    tm = min(TOKEN_TILE, n)
    final = g_final is not None
    const = lambda a: pl.BlockSpec(a.shape, lambda i: (0, 0))
    resident = lambda a: pl.BlockSpec((None,) + a.shape[1:], lambda i: (layer, 0, 0),
                                      pipeline_mode=pl.Buffered(1))
    in_specs = [pl.BlockSpec((tm, d), lambda i: (i, 0)),
                pl.BlockSpec((None, tm, p.shape[2]), lambda i: (layer, i, 0)),
                const(g), resident(w_gate), resident(w_proj)]
    args = [x, p, g, w_gate, w_proj]
    if final:
        in_specs.append(const(g_final))
        args.append(g_final)
    return pl.pallas_call(
        functools.partial(_ple_kernel, final=final),
        grid=(n // tm,),
        in_specs=in_specs,
        out_specs=pl.BlockSpec((tm, d), lambda i: (i, 0)),
        out_shape=jax.ShapeDtypeStruct((n, d), F32),
        compiler_params=_params(1),
        name="ple_final" if final else "ple",
    )(*args)


def _ret_in_kernel(x_ref, g_ref, w_ref, cos_ref, sin_ref, o_ref, hs_ref, *, dk):
    @pl.when(pl.program_id(1) == 0)
    def _():
        hs_ref[...] = _rmsnorm(x_ref[...], g_ref[...]).astype(BF16)

    tm = hs_ref.shape[0]
    rb = min(ROW_BLOCK, tm)
    half = dk // 2
    project = lambda r: jnp.dot(hs_ref[r:r + rb, :], w_ref[...], preferred_element_type=F32)
    nxt = project(0)
    for r in range(0, tm, rb):
        proj = nxt
        if r + rb < tm:
            nxt = project(r + rb)
        cos = cos_ref[r:r + rb, :]
        sin = sin_ref[r:r + rb, :]
        for hh in range(proj.shape[1] // dk):
            x1 = proj[:, hh * dk:hh * dk + half]
            x2 = proj[:, hh * dk + half:(hh + 1) * dk]
            o_ref[r:r + rb, hh * dk:hh * dk + half] = (x1 * cos - x2 * sin).astype(o_ref.dtype)
            o_ref[r:r + rb, hh * dk + half:(hh + 1) * dk] = (x1 * sin + x2 * cos).astype(o_ref.dtype)


def _ret_in(x, g, w_in, tables, *, qk_width, dk, rows_per_seq, out_dtype):
    n, d = x.shape
    e = w_in.shape[1]
    tm = min(WIDE_TILE, n)
    nc = PROJ_CHUNK
    qc = qk_width // nc
    tps = rows_per_seq // tm

    def table_block(i, c):
        return (jnp.where(c < qc, 0, jnp.where(c < 2 * qc, tps, 2 * tps)) + i % tps, 0)

    return pl.pallas_call(
        functools.partial(_ret_in_kernel, dk=dk),
        grid=(n // tm, e // nc),
        in_specs=[pl.BlockSpec((tm, d), lambda i, c: (i, 0)),
                  pl.BlockSpec((1, d), lambda i, c: (0, 0)),
                  pl.BlockSpec((d, nc), lambda i, c: (0, c)),
                  pl.BlockSpec((tm, dk // 2), table_block),
                  pl.BlockSpec((tm, dk // 2), table_block)],
        out_specs=pl.BlockSpec((tm, nc), lambda i, c: (i, c)),
        out_shape=jax.ShapeDtypeStruct((n, e), out_dtype),
        scratch_shapes=[pltpu.VMEM((tm, d), BF16)],
        compiler_params=_params(2),
        name="ret_in",
    )(x, g, w_in, *tables)


def _groupnorm_gate(o, gate):
    mu = jnp.mean(o, axis=-1, keepdims=True)
    var = jnp.mean(jnp.square(o - mu), axis=-1, keepdims=True)
    on = (o - mu) * lax.rsqrt(var + EPS)
    gf = gate.astype(F32)
    return (gf / (1.0 + jnp.exp(-gf))) * on


def _ret_prompt_kernel(q_ref, k_ref, v_ref, g_ref, intra_ref, cross_ref, kdec_ref, sdec_ref,
                       o_ref, s_out_ref, s_ref, *, chunk):
    t = q_ref.shape[0]
    s_ref[...] = jnp.zeros(s_ref.shape, F32)
    intra = intra_ref[...]
    cross = cross_ref[...]
    kdec = kdec_ref[...]
    sdec = sdec_ref[...]

    def body(c, carry):
        rows = pl.ds(pl.multiple_of(c * chunk, chunk), chunk)
        q = q_ref[rows, :]
        k = k_ref[rows, :]
        v = v_ref[rows, :]
        s = s_ref[...]
        sc = lax.dot_general(q, k, (((1,), (1,)), ((), ())), preferred_element_type=F32) * intra
        o = (jnp.dot(sc.astype(BF16), v, preferred_element_type=F32)
             + jnp.dot(q, s.astype(BF16), preferred_element_type=F32) * cross)
        kd = (k.astype(F32) * kdec).astype(BF16)
        s_ref[...] = sdec * s + lax.dot_general(kd, v, (((0,), (0,)), ((), ())),
                                                preferred_element_type=F32)
        o_ref[rows, :] = _groupnorm_gate(o, g_ref[rows, :]).astype(o_ref.dtype)
        return carry

    lax.fori_loop(0, t // chunk, body, 0, unroll=4)
    s_out_ref[...] = s_ref[...]


def _ret_prompt(proj, consts, *, batch, seq, heads, dk, dv):
    intra, cross, kdec, sdec = consts
    chunk = intra.shape[1]
    qk = heads * dk
    k_off = qk // dk
    v_off = 2 * qk // dv
    g_off = v_off + heads
    head_const = lambda a: pl.BlockSpec((None,) + a.shape[1:], lambda b, h: (h, 0, 0))
    return pl.pallas_call(
        functools.partial(_ret_prompt_kernel, chunk=chunk),
        grid=(batch, heads),
        in_specs=[pl.BlockSpec((seq, dk), lambda b, h: (b, h)),
                  pl.BlockSpec((seq, dk), lambda b, h: (b, k_off + h)),
                  pl.BlockSpec((seq, dv), lambda b, h: (b, v_off + h)),
                  pl.BlockSpec((seq, dv), lambda b, h: (b, g_off + h)),
                  head_const(intra), head_const(cross), head_const(kdec), head_const(sdec)],
        out_specs=[pl.BlockSpec((seq, dv), lambda b, h: (b, h)),
                   pl.BlockSpec((None, None, dk, dv), lambda b, h: (b, h, 0, 0))],
        out_shape=[jax.ShapeDtypeStruct((batch * seq, heads * dv), BF16),
                   jax.ShapeDtypeStruct((batch, heads, dk, dv), F32)],
        scratch_shapes=[pltpu.VMEM((dk, dv), F32)],
        compiler_params=_params(2),
        name="retention_prompt",
    )(proj, proj, proj, proj, intra, cross, kdec, sdec)


def _ret_sample_kernel(p_ref, s_ref, intra_ref, cross_ref, kdec_ref, sdec_ref, o_ref, s_out_ref,
                       *, heads, dk, dv):
    qk = heads * dk
    for h in range(heads):
        q = p_ref[:, h * dk:(h + 1) * dk].astype(BF16)
        kf = p_ref[:, qk + h * dk:qk + (h + 1) * dk]
        v = p_ref[:, 2 * qk + h * dv:2 * qk + (h + 1) * dv].astype(BF16)
        gate = p_ref[:, 2 * qk + heads * dv + h * dv:2 * qk + heads * dv + (h + 1) * dv]
        s = s_ref[h]
        sc = lax.dot_general(q, kf.astype(BF16), (((1,), (1,)), ((), ())),
                             preferred_element_type=F32) * intra_ref[h]
        o = (jnp.dot(sc.astype(BF16), v, preferred_element_type=F32)
             + jnp.dot(q, s.astype(BF16), preferred_element_type=F32) * cross_ref[h])
        kd = (kf * kdec_ref[h]).astype(BF16)
        s_out_ref[h] = sdec_ref[h] * s + lax.dot_general(kd, v, (((0,), (0,)), ((), ())),
                                                         preferred_element_type=F32)
        o_ref[:, h * dv:(h + 1) * dv] = _groupnorm_gate(o, gate).astype(o_ref.dtype)


def _ret_sample(proj, state, consts, *, heads, dk, dv):
    intra, cross, kdec, sdec = consts
    b, tp, e = proj.shape
    const = lambda a: pl.BlockSpec(a.shape, lambda i: (0,) * a.ndim)
    return pl.pallas_call(
        functools.partial(_ret_sample_kernel, heads=heads, dk=dk, dv=dv),
        grid=(b,),
        in_specs=[pl.BlockSpec((None, tp, e), lambda i: (i, 0, 0)),
                  pl.BlockSpec((None, heads, dk, dv), lambda i: (i, 0, 0, 0)),
                  const(intra), const(cross), const(kdec), const(sdec)],
        out_specs=[pl.BlockSpec((None, tp, heads * dv), lambda i: (i, 0, 0)),
                   pl.BlockSpec((None, heads, dk, dv), lambda i: (i, 0, 0, 0))],
        out_shape=[jax.ShapeDtypeStruct((b, tp, heads * dv), F32),
                   jax.ShapeDtypeStruct((b, heads, dk, dv), F32)],
        compiler_params=_params(1),
        name="retention_sample",
    )(proj, state, intra, cross, kdec, sdec)


def _ret_out_kernel(x_ref, a_ref, w_ref, o_ref):
    o_ref[...] = x_ref[...] + jnp.dot(a_ref[...].astype(BF16), w_ref[...], preferred_element_type=F32)


def _ret_out(x, a, w_out):
    n, d = x.shape
    v = a.shape[1]
    tm = min(TOKEN_TILE, n)
    return pl.pallas_call(
        _ret_out_kernel,
        grid=(n // tm,),
        in_specs=[pl.BlockSpec((tm, d), lambda i: (i, 0)),
                  pl.BlockSpec((tm, v), lambda i: (i, 0)),
                  pl.BlockSpec((v, d), lambda i: (0, 0), pipeline_mode=pl.Buffered(1))],
        out_specs=pl.BlockSpec((tm, d), lambda i: (i, 0)),
        out_shape=jax.ShapeDtypeStruct((n, d), F32),
        compiler_params=_params(1),
        name="ret_out",
    )(x, a, w_out)


def _decay_consts(chunk, rows, heads, dk, dv):
    log_g = np.log1p(-(2.0 ** (-5.0 - np.arange(heads, dtype=np.float64))))
    n = np.arange(rows, dtype=np.float64)
    live = n < chunk
    diff = n[:, None] - n[None, :]
    intra = np.where((diff >= 0) & live[:, None] & live[None, :],
                     np.exp(log_g[:, None, None] * np.maximum(diff, 0.0)), 0.0)
    cross = np.where(live, np.exp(log_g[:, None] * (n + 1.0)), 0.0)
    kdec = np.where(live, np.exp(log_g[:, None] * (chunk - 1.0 - n)), 0.0)
    sdec = np.exp(log_g * chunk)
    f32 = lambda a: jnp.asarray(np.ascontiguousarray(a), dtype=F32)
    return (f32(intra),
            f32(np.broadcast_to(cross[:, :, None], (heads, rows, dv))),
            f32(np.broadcast_to(kdec[:, :, None], (heads, rows, dk))),
            f32(np.broadcast_to(sdec[:, None, None], (heads, 1, dv))))


def _rotary_tables(pos, dk):
    half = dk // 2
    inv = ROPE_BASE ** (-np.arange(half, dtype=np.float64) / half)
    ang = np.asarray(pos, dtype=np.float64)[:, None] * inv[None, :]
    cos, sin = np.cos(ang), np.sin(ang)
    scale = dk ** -0.5
    f32 = lambda a: jnp.asarray(a, dtype=F32)
    return (f32(np.concatenate([cos, cos * scale, np.ones_like(cos)])),
            f32(np.concatenate([sin, sin * scale, np.zeros_like(sin)])))


def kernel(x_prompt, x_sample, p_prompt, p_sample, state_pool, state_ret, state_conv, norm_mix, norm_ffn,
           norm_ple, norm_final, pool_w, pool_scale, ret_w_in, ret_w_out, ffn_w_up, ffn_conv_w,
           ffn_conv_b, ffn_w_down, ple_w_proj, ple_w_gate):
    b, t, d = x_prompt.shape
    bs, ts, _ = x_sample.shape
    depth = norm_mix.shape[0]
    heads = RET_HEADS
    dk = d // heads
    dv = 2 * dk
    qk = heads * dk
    f2 = ffn_w_up.shape[2]
    ffn = f2 // 2
    tp = SUBLANES
    assert ts <= tp and ts >= CONV_W - 1 and PAST_LEN >= POOL_BUF

    row2 = lambda a: a.reshape(1, -1)
    t_major = lambda a: jnp.swapaxes(a, 0, 1).reshape(-1, a.shape[-1])
    b_major = lambda a: jnp.swapaxes(a.reshape(-1, bs, a.shape[-1]), 0, 1)

    xp = x_prompt.reshape(b * t, d)
    xs = t_major(x_sample)
    pp = p_prompt.reshape(depth, b * t, -1)
    ps = jnp.swapaxes(p_sample, 1, 2).reshape(depth, ts * bs, -1)

    tables_p = _rotary_tables(np.arange(t), dk)
    tables_s = _rotary_tables(PAST_LEN + np.repeat(np.arange(ts), bs), dk)
    chunk_p = RET_CHUNK if t % RET_CHUNK == 0 else t
    consts_p = _decay_consts(chunk_p, chunk_p, heads, dk, dv)
    consts_s = _decay_consts(ts, tp, heads, dk, dv)

    new_pool_p, new_pool_s, new_ret_p, new_ret_s, new_conv_p, new_conv_s = [], [], [], [], [], []
    for i in range(depth):
        jm = i // 2
        if i % 2 == 0:
            w = pool_w[jm]
            g = row2(norm_mix[i])
            sc = row2(pool_scale[jm])
            xp, hl = _pool_mixer(xp, None, g, w, sc, shift=1, pos0=0, rows_per_seq=t)
            new_pool_p.append(hl[:, -POOL_BUF:, :])
            halo_s = t_major(state_pool[jm])
            xs, h_s = _pool_mixer(xs, halo_s, g, w, sc, shift=bs, pos0=PAST_LEN, rows_per_seq=ts * bs)
            new_pool_s.append(jnp.concatenate([state_pool[jm], b_major(h_s)], axis=1)[:, -POOL_BUF:, :])
        else:
            w_in = ret_w_in[jm]
            w_out = ret_w_out[jm].astype(BF16)
            g = row2(norm_mix[i])
            proj_p = _ret_in(xp, g, w_in, tables_p, qk_width=qk, dk=dk, rows_per_seq=t, out_dtype=BF16)
            a_p, s_p = _ret_prompt(proj_p, consts_p, batch=b, seq=t, heads=heads, dk=dk, dv=dv)
            xp = _ret_out(xp, a_p, w_out)
            new_ret_p.append(s_p)
            proj_s = _ret_in(xs, g, w_in, tables_s, qk_width=qk, dk=dk, rows_per_seq=ts * bs, out_dtype=F32)
            proj_s = jnp.pad(b_major(proj_s), ((0, 0), (0, tp - ts), (0, 0)))
            a_s, s_s = _ret_sample(proj_s, state_ret[jm], consts_s, heads=heads, dk=dk, dv=dv)
            xs = _ret_out(xs, t_major(a_s[:, :ts, :]), w_out)
            new_ret_s.append(s_s)

        g = row2(norm_ffn[i])
        cb = row2(ffn_conv_b[i])
        xp, tg, tu = _conv_ffn(xp, g, None, ffn_w_up, ffn_conv_w, cb, ffn_w_down, i, shift=1, rows_per_seq=t)
        tiles_per_seq = tg.shape[0] // b
        last = jnp.concatenate([tg, tu], axis=-1)[tiles_per_seq - 1::tiles_per_seq]
        new_conv_p.append(last[:, -(CONV_W - 1):, :])
        st = t_major(state_conv[i])
        xs, tg, tu = _conv_ffn(xs, g, st, ffn_w_up, ffn_conv_w, cb, ffn_w_down, i, shift=bs,
                               rows_per_seq=ts * bs)
        new_conv_s.append(b_major(jnp.concatenate([tg[0], tu[0]], axis=-1)))

        g = row2(norm_ple[i])
        g_final = row2(norm_final) if i == depth - 1 else None
        xp = _ple(xp, pp, g, ple_w_gate, ple_w_proj, i, g_final)
        xs = _ple(xs, ps, g, ple_w_gate, ple_w_proj, i, g_final)

    return (xp.reshape(b, t, d), b_major(xs),
            jnp.stack(new_pool_p), jnp.stack(new_pool_s),
            jnp.stack(new_ret_p), jnp.stack(new_ret_s),
            jnp.stack(new_conv_p), jnp.stack(new_conv_s))
```

```python
import functools

import jax
import jax.numpy as jnp
import numpy as np
from jax import lax
from jax.experimental import pallas as pl
from jax.experimental.pallas import tpu as pltpu

F32 = jnp.float32
BF16 = jnp.bfloat16

EPS = 1e-6
POOL_WINDOWS = (2, 4, 8, 16)
POOL_BUF = max(POOL_WINDOWS) - 1
RET_HEADS = 8
RET_CHUNK = 128
ROPE_BASE = 10000.0
CONV_W = 3
PAST_LEN = 16384

TOKEN_TILE = 512
WIDE_TILE = 1024
FFN_CHUNK = 512
ROW_BLOCK = 512
PROJ_CHUNK = 1024
SAMPLE_SEQS = 2
SUBLANES = 8
VMEM_LIMIT = 60 * 1024 * 1024


def _params(n_axes):
    return pltpu.CompilerParams(
        dimension_semantics=("arbitrary",) * n_axes, vmem_limit_bytes=VMEM_LIMIT)


def _rmsnorm(x, g):
    return x * lax.rsqrt(jnp.mean(x * x, axis=-1, keepdims=True) + EPS) * g


def _shift_rows(a, rows):
    return pltpu.roll(a, rows, axis=0)


def _pool_group(h, prev, x, w, sc, pos, win, shift):
    halo = prev.shape[0]
    a = jnp.concatenate([prev, h], axis=0)
    span = 1
    while span < win:
        a = a + _shift_rows(a, span * shift)
        span *= 2
    cnt = jnp.minimum(pos + 1, win).astype(F32)
    dlt = a[halo:, :] / cnt - h
    return x + jnp.dot(dlt.astype(BF16), w, preferred_element_type=F32) * sc


def _pool_prompt_kernel(x_ref, g_ref, w_ref, sc_ref, o_ref, h_ref, carry_ref, *, tiles_per_seq):
    tm, d = x_ref.shape
    gw = d // len(POOL_WINDOWS)
    halo = carry_ref.shape[0]
    j = pl.program_id(0) % tiles_per_seq
    x = x_ref[...]
    h = _rmsnorm(x, g_ref[...])
    last = h[tm - halo:, :]
    h_ref[...] = last
    pos = j * tm + lax.broadcasted_iota(jnp.int32, (tm, 1), 0)
    for gi, win in enumerate(POOL_WINDOWS):
        cols = slice(gi * gw, (gi + 1) * gw)
        prev = jnp.where(j == 0, 0.0, carry_ref[:, cols])
        o_ref[:, cols] = _pool_group(h[:, cols], prev, x[:, cols], w_ref[gi], sc_ref[:, cols], pos, win, 1)
    carry_ref[...] = last


def _pool_state_kernel(x_ref, halo_ref, g_ref, w_ref, sc_ref, o_ref, h_ref, hs_ref, *, shift, pos0):
    tm, d = x_ref.shape
    gw = o_ref.shape[1]
    grp = pl.program_id(0)

    @pl.when(grp == 0)
    def _():
        h = _rmsnorm(x_ref[...], g_ref[...])
        hs_ref[...] = h
        h_ref[...] = h

    row = lax.broadcasted_iota(jnp.int32, (tm, 1), 0)
    step = jnp.zeros_like(row)
    for k in range(1, tm // shift):
        step = step + (row >= k * shift).astype(jnp.int32)
    pos = pos0 + step

    for gi, win in enumerate(POOL_WINDOWS):
        @pl.when(grp == gi)
        def _(gi=gi, win=win):
            cols = slice(gi * gw, (gi + 1) * gw)
            prev = jnp.concatenate([jnp.zeros((shift, gw), F32), halo_ref[...]], axis=0)
            o_ref[...] = _pool_group(hs_ref[:, cols], prev, x_ref[:, cols], w_ref[...], sc_ref[...],
                                     pos, win, shift)


def _pool_mixer(x, halo, g, w, scale, *, shift, pos0, rows_per_seq):
    n, d = x.shape
    ng = len(POOL_WINDOWS)
    gw = d // ng
    tm = min(TOKEN_TILE, n)
    if halo is None:
        assert shift == 1 and pos0 == 0
        tiles_per_seq = rows_per_seq // tm
        halo_rows = POOL_BUF + 1
        full = lambda a: pl.BlockSpec(a.shape, lambda i: (0,) * a.ndim)
        tile = pl.BlockSpec((tm, d), lambda i: (i, 0))
        return pl.pallas_call(
            functools.partial(_pool_prompt_kernel, tiles_per_seq=tiles_per_seq),
            grid=(n // tm,),
            in_specs=[tile, full(g), full(w), full(scale)],
            out_specs=[tile, pl.BlockSpec((None, halo_rows, d), lambda i: (i // tiles_per_seq, 0, 0))],
            out_shape=[jax.ShapeDtypeStruct((n, d), F32),
                       jax.ShapeDtypeStruct((n // rows_per_seq, halo_rows, d), F32)],
            scratch_shapes=[pltpu.VMEM((halo_rows, d), F32)],
            compiler_params=_params(1),
            name="pool_mixer",
        )(x, g, w, scale)
    assert n == tm
    return pl.pallas_call(
        functools.partial(_pool_state_kernel, shift=shift, pos0=pos0),
        grid=(ng,),
        in_specs=[pl.BlockSpec((tm, d), lambda gi: (0, 0)),
                  pl.BlockSpec((halo.shape[0], gw), lambda gi: (0, gi)),
                  pl.BlockSpec((1, d), lambda gi: (0, 0)),
                  pl.BlockSpec((None, gw, gw), lambda gi: (gi, 0, 0)),
                  pl.BlockSpec((1, gw), lambda gi: (0, gi))],
        out_specs=[pl.BlockSpec((tm, gw), lambda gi: (0, gi)), pl.BlockSpec((tm, d), lambda gi: (0, 0))],
        out_shape=[jax.ShapeDtypeStruct((n, d), F32), jax.ShapeDtypeStruct((n, d), F32)],
        scratch_shapes=[pltpu.VMEM((tm, d), F32)],
        compiler_params=_params(1),
        name="pool_mixer_state",
    )(x, halo, g, w, scale)


def _ffn_kernel(*refs, shift, tiles_per_seq, from_state):
    if from_state:
        (x_ref, g_ref, sg_ref, su_ref, wg_ref, wu_ref, cwg_ref, cwu_ref, cbg_ref, cbu_ref, wd_ref,
         o_ref, tg_ref, tu_ref, hs_ref) = refs
        carry_g = carry_u = None
    else:
        (x_ref, g_ref, wg_ref, wu_ref, cwg_ref, cwu_ref, cbg_ref, cbu_ref, wd_ref,
         o_ref, tg_ref, tu_ref, hs_ref, carry_g, carry_u) = refs
        sg_ref = su_ref = None
    tm, d = x_ref.shape
    halo = max((CONV_W - 1) * shift, SUBLANES)
    i = pl.program_id(0)
    f = pl.program_id(1)
    first_of_seq = (i % tiles_per_seq) == 0

    @pl.when(f == 0)
    def _():
        x = x_ref[...]
        hs_ref[...] = _rmsnorm(x, g_ref[...]).astype(BF16)
        o_ref[...] = x

    rb = min(ROW_BLOCK, tm)
    cwg, cwu, cbg, cbu = cwg_ref[...], cwu_ref[...], cbg_ref[...], cbu_ref[...]
    if from_state:
        assert rb == tm
        prev_g, prev_u = sg_ref[...], su_ref[...]
    else:
        prev_g = jnp.where(first_of_seq, 0.0, carry_g[f])
        prev_u = jnp.where(first_of_seq, 0.0, carry_u[f])

    def up(r):
        hs = hs_ref[r:r + rb, :]
        return (jnp.dot(hs, wg_ref[...], preferred_element_type=F32),
                jnp.dot(hs, wu_ref[...], preferred_element_type=F32))

    def conv(prev, u, cw, cb):
        ext = jnp.concatenate([prev, u], axis=0)
        if shift % SUBLANES == 0:
            u1 = ext[halo - shift:halo - shift + rb, :]
            u2 = ext[halo - 2 * shift:halo - 2 * shift + rb, :]
        else:
            u1 = _shift_rows(ext, shift)[halo:, :]
            u2 = _shift_rows(ext, 2 * shift)[halo:, :]
        return cw[0:1, :] * u2 + cw[1:2, :] * u1 + cw[2:3, :] * u + cb

    nxt = up(0)
    for r in range(0, tm, rb):
        ug, uu = nxt
        if r + rb < tm:
            nxt = up(r + rb)
        cg = conv(prev_g, ug, cwg, cbg)
        cu = conv(prev_u, uu, cwu, cbu)
        prev_g, prev_u = ug[rb - halo:, :], uu[rb - halo:, :]
        act = ((cg / (1.0 + jnp.exp(-cg))) * cu).astype(BF16)
        o_ref[r:r + rb, :] += jnp.dot(act, wd_ref[...], preferred_element_type=F32)
    if not from_state:
        carry_g[f] = prev_g
        carry_u[f] = prev_u
    tg_ref[...] = prev_g
    tu_ref[...] = prev_u


def _conv_ffn(x, g, state, w_up, cw, cb, w_down, layer, *, shift, rows_per_seq):
    n, d = x.shape
    f2 = w_up.shape[2]
    ffn = f2 // 2
    tm = min(WIDE_TILE, n)
    fn = FFN_CHUNK
    nf = ffn // fn
    n_tiles = n // tm
    tiles_per_seq = max(rows_per_seq // tm, 1)
    halo = max((CONV_W - 1) * shift, SUBLANES)
    from_state = state is not None

    tile = pl.BlockSpec((tm, d), lambda i, f: (i, 0))
    tile_in = pl.BlockSpec((tm, d), lambda i, f: (i, 0), pipeline_mode=pl.Buffered(1))
    gate_cols = lambda rows: pl.BlockSpec((rows, fn), lambda i, f: (0, f))
    up_cols = lambda rows: pl.BlockSpec((rows, fn), lambda i, f: (0, f + nf))
    layer_gate = lambda rows: pl.BlockSpec((None, rows, fn), lambda i, f: (layer, 0, f))
    layer_up = lambda rows: pl.BlockSpec((None, rows, fn), lambda i, f: (layer, 0, f + nf))
    in_specs = [tile_in, pl.BlockSpec((1, d), lambda i, f: (0, 0))]
    args = [x, g]
    if from_state:
        in_specs += [gate_cols(halo), up_cols(halo)]
        args += [state, state]
    in_specs += [layer_gate(d), layer_up(d), layer_gate(CONV_W), layer_up(CONV_W), gate_cols(1), up_cols(1),
                 pl.BlockSpec((None, fn, d), lambda i, f: (layer, f, 0))]
    args += [w_up, w_up, cw, cw, cb, cb, w_down]
    tail_spec = pl.BlockSpec((None, halo, fn), lambda i, f: (i, 0, f))
    tail_shape = jax.ShapeDtypeStruct((n_tiles, halo, ffn), F32)
    scratch = [pltpu.VMEM((tm, d), BF16)]
    if not from_state:
        scratch += [pltpu.VMEM((nf, halo, fn), F32), pltpu.VMEM((nf, halo, fn), F32)]
    return pl.pallas_call(
        functools.partial(_ffn_kernel, shift=shift, tiles_per_seq=tiles_per_seq, from_state=from_state),
        grid=(n_tiles, nf),
        in_specs=in_specs,
        out_specs=[tile, tail_spec, tail_spec],
        out_shape=[jax.ShapeDtypeStruct((n, d), F32), tail_shape, tail_shape],
        scratch_shapes=scratch,
        compiler_params=_params(2),
        name="conv_ffn_state" if from_state else "conv_ffn",
    )(*args)


def _ple_kernel(*refs, final):
    if final:
        x_ref, p_ref, g_ref, wg_ref, wp_ref, gf_ref, o_ref = refs
    else:
        x_ref, p_ref, g_ref, wg_ref, wp_ref, o_ref = refs
    x = x_ref[...]
    hn = _rmsnorm(x, g_ref[...]).astype(BF16)
    gate = jax.nn.sigmoid(jnp.dot(hn, wg_ref[...], preferred_element_type=F32))
    emb = jnp.dot(p_ref[...].astype(BF16), wp_ref[...], preferred_element_type=F32)
    y = x + gate * emb
    if final:
        y = _rmsnorm(y, gf_ref[...])
    o_ref[...] = y


def _ple(x, p, g, w_gate, w_proj, layer, g_final):
    n, d = x.shape
    tm = min(TOKEN_TILE, n)
    final = g_final is not None
    const = lambda a: pl.BlockSpec(a.shape, lambda i: (0, 0))
    resident = lambda a: pl.BlockSpec((None,) + a.shape[1:], lambda i: (layer, 0, 0),
                                      pipeline_mode=pl.Buffered(1))
    in_specs = [pl.BlockSpec((tm, d), lambda i: (i, 0)),
                pl.BlockSpec((None, tm, p.shape[2]), lambda i: (layer, i, 0)),
                const(g), resident(w_gate), resident(w_proj)]
    args = [x, p, g, w_gate, w_proj]
    if final:
        in_specs.append(const(g_final))
        args.append(g_final)
    return pl.pallas_call(
        functools.partial(_ple_kernel, final=final),
        grid=(n // tm,),
        in_specs=in_specs,
        out_specs=pl.BlockSpec((tm, d), lambda i: (i, 0)),
        out_shape=jax.ShapeDtypeStruct((n, d), F32),
        compiler_params=_params(1),
        name="ple_final" if final else "ple",
    )(*args)


def _ret_in_kernel(x_ref, g_ref, w_ref, cos_ref, sin_ref, o_ref, hs_ref, *, dk):
    @pl.when(pl.program_id(1) == 0)
    def _():
        hs_ref[...] = _rmsnorm(x_ref[...], g_ref[...]).astype(BF16)

    tm = hs_ref.shape[0]
    rb = min(ROW_BLOCK, tm)
    half = dk // 2
    project = lambda r: jnp.dot(hs_ref[r:r + rb, :], w_ref[...], preferred_element_type=F32)
    nxt = project(0)
    for r in range(0, tm, rb):
        proj = nxt
        if r + rb < tm:
            nxt = project(r + rb)
        cos = cos_ref[r:r + rb, :]
        sin = sin_ref[r:r + rb, :]
        for hh in range(proj.shape[1] // dk):
            x1 = proj[:, hh * dk:hh * dk + half]
            x2 = proj[:, hh * dk + half:(hh + 1) * dk]
            y1 = (x1 * cos - x2 * sin).astype(o_ref.dtype)
            y2 = (x1 * sin + x2 * cos).astype(o_ref.dtype)
            if len(o_ref.shape) == 3:
                o_ref[hh, r:r + rb, 0:half] = y1
                o_ref[hh, r:r + rb, half:dk] = y2
            else:
                o_ref[r:r + rb, hh * dk:hh * dk + half] = y1
                o_ref[r:r + rb, hh * dk + half:(hh + 1) * dk] = y2


def _ret_in(x, g, w_in, tables, *, qk_width, dk, rows_per_seq, out_dtype, head_major):
    n, d = x.shape
    e = w_in.shape[1]
    tm = min(WIDE_TILE, n)
    nc = PROJ_CHUNK
    qc = qk_width // nc
    tps = rows_per_seq // tm

    def table_block(i, c):
        return (jnp.where(c < qc, 0, jnp.where(c < 2 * qc, tps, 2 * tps)) + i % tps, 0)

    if head_major:
        out_spec = pl.BlockSpec((nc // dk, tm, dk), lambda i, c: (c, i, 0))
        out_shape = jax.ShapeDtypeStruct((e // dk, n, dk), out_dtype)
    else:
        out_spec = pl.BlockSpec((tm, nc), lambda i, c: (i, c))
        out_shape = jax.ShapeDtypeStruct((n, e), out_dtype)

    return pl.pallas_call(
        functools.partial(_ret_in_kernel, dk=dk),
        grid=(n // tm, e // nc),
        in_specs=[pl.BlockSpec((tm, d), lambda i, c: (i, 0)),
                  pl.BlockSpec((1, d), lambda i, c: (0, 0)),
                  pl.BlockSpec((d, nc), lambda i, c: (0, c)),
                  pl.BlockSpec((tm, dk // 2), table_block),
                  pl.BlockSpec((tm, dk // 2), table_block)],
        out_specs=out_spec,
        out_shape=out_shape,
        scratch_shapes=[pltpu.VMEM((tm, d), BF16)],
        compiler_params=_params(2),
        name="ret_in",
    )(x, g, w_in, *tables)


def _groupnorm_gate(o, gate):
    mu = jnp.mean(o, axis=-1, keepdims=True)
    var = jnp.mean(jnp.square(o - mu), axis=-1, keepdims=True)
    on = (o - mu) * lax.rsqrt(var + EPS)
    gf = gate.astype(F32)
    return (gf / (1.0 + jnp.exp(-gf))) * on


def _ret_prompt_kernel(q_ref, k_ref, v_ref, g_ref, intra_ref, cross_ref, kdec_ref, sdec_ref,
                       o_ref, s_out_ref, s_ref, *, chunk):
    t = q_ref.shape[0]
    s_ref[...] = jnp.zeros(s_ref.shape, F32)
    intra = intra_ref[...]
    cross = cross_ref[...]
    kdec = kdec_ref[...]
    sdec = sdec_ref[...]

    wide = lambda ref, rows: jnp.concatenate([ref[i, rows, :] for i in range(ref.shape[0])], axis=-1)

    def body(c, carry):
        rows = pl.ds(pl.multiple_of(c * chunk, chunk), chunk)
        q = q_ref[rows, :]
        k = k_ref[rows, :]
        v = wide(v_ref, rows)
        s = s_ref[...]
        sc = lax.dot_general(q, k, (((1,), (1,)), ((), ())), preferred_element_type=F32) * intra
        o = (jnp.dot(sc.astype(BF16), v, preferred_element_type=F32)
             + jnp.dot(q, s.astype(BF16), preferred_element_type=F32) * cross)
        kd = (k.astype(F32) * kdec).astype(BF16)
        s_ref[...] = sdec * s + lax.dot_general(kd, v, (((0,), (0,)), ((), ())),
                                                preferred_element_type=F32)
        o_ref[rows, :] = _groupnorm_gate(o, wide(g_ref, rows)).astype(o_ref.dtype)
        return carry

    lax.fori_loop(0, t // chunk, body, 0, unroll=4)
    s_out_ref[...] = s_ref[...]


def _ret_prompt(proj, consts, *, batch, seq, heads, dk, dv):
    intra, cross, kdec, sdec = consts
    chunk = intra.shape[1]
    per = dv // dk
    v_off = 2 * heads // per
    g_off = v_off + heads
    head_const = lambda a: pl.BlockSpec((None,) + a.shape[1:], lambda b, h: (h, 0, 0))
    return pl.pallas_call(
        functools.partial(_ret_prompt_kernel, chunk=chunk),
        grid=(batch, heads),
        in_specs=[pl.BlockSpec((None, seq, dk), lambda b, h: (h, b, 0)),
                  pl.BlockSpec((None, seq, dk), lambda b, h: (heads + h, b, 0)),
                  pl.BlockSpec((per, seq, dk), lambda b, h: (v_off + h, b, 0)),
                  pl.BlockSpec((per, seq, dk), lambda b, h: (g_off + h, b, 0)),
                  head_const(intra), head_const(cross), head_const(kdec), head_const(sdec)],
        out_specs=[pl.BlockSpec((None, seq, dv), lambda b, h: (h, b, 0)),
                   pl.BlockSpec((None, None, dk, dv), lambda b, h: (b, h, 0, 0))],
        out_shape=[jax.ShapeDtypeStruct((heads, batch * seq, dv), BF16),
                   jax.ShapeDtypeStruct((batch, heads, dk, dv), F32)],
        scratch_shapes=[pltpu.VMEM((dk, dv), F32)],
        compiler_params=_params(2),
        name="retention_prompt",
    )(proj, proj, proj, proj, intra, cross, kdec, sdec)


def _ret_sample_kernel(p_ref, s_ref, intra_ref, cross_ref, kdec_ref, sdec_ref, o_ref, s_out_ref,
                       *, heads, dk, dv):
    qk = heads * dk
    for n in range(p_ref.shape[0]):
        for h in range(heads):
            q = p_ref[n, :, h * dk:(h + 1) * dk].astype(BF16)
            kf = p_ref[n, :, qk + h * dk:qk + (h + 1) * dk]
            v = p_ref[n, :, 2 * qk + h * dv:2 * qk + (h + 1) * dv].astype(BF16)
            gate = p_ref[n, :, 2 * qk + heads * dv + h * dv:2 * qk + heads * dv + (h + 1) * dv]
            s = s_ref[n, h]
            sc = lax.dot_general(q, kf.astype(BF16), (((1,), (1,)), ((), ())),
                                 preferred_element_type=F32) * intra_ref[h]
            o = (jnp.dot(sc.astype(BF16), v, preferred_element_type=F32)
                 + jnp.dot(q, s.astype(BF16), preferred_element_type=F32) * cross_ref[h])
            kd = (kf * kdec_ref[h]).astype(BF16)
            s_out_ref[n, h] = sdec_ref[h] * s + lax.dot_general(kd, v, (((0,), (0,)), ((), ())),
                                                                preferred_element_type=F32)
            o_ref[n, :, h * dv:(h + 1) * dv] = _groupnorm_gate(o, gate).astype(o_ref.dtype)


def _ret_sample(proj, state, consts, *, heads, dk, dv):
    intra, cross, kdec, sdec = consts
    b, tp, e = proj.shape
    nb = SAMPLE_SEQS
    const = lambda a: pl.BlockSpec(a.shape, lambda i: (0,) * a.ndim)
    return pl.pallas_call(
        functools.partial(_ret_sample_kernel, heads=heads, dk=dk, dv=dv),
        grid=(b // nb,),
        in_specs=[pl.BlockSpec((nb, tp, e), lambda i: (i, 0, 0)),
                  pl.BlockSpec((nb, heads, dk, dv), lambda i: (i, 0, 0, 0)),
                  const(intra), const(cross), const(kdec), const(sdec)],
        out_specs=[pl.BlockSpec((nb, tp, heads * dv), lambda i: (i, 0, 0)),
                   pl.BlockSpec((nb, heads, dk, dv), lambda i: (i, 0, 0, 0))],
        out_shape=[jax.ShapeDtypeStruct((b, tp, heads * dv), F32),
                   jax.ShapeDtypeStruct((b, heads, dk, dv), F32)],
        compiler_params=_params(1),
        name="retention_sample",
    )(proj, state, intra, cross, kdec, sdec)


def _ret_out_kernel(x_ref, a_ref, w_ref, o_ref):
    if len(a_ref.shape) == 3:
        dv = a_ref.shape[2]
        y = x_ref[...]
        for h in range(a_ref.shape[0]):
            y = y + jnp.dot(a_ref[h], w_ref[h * dv:(h + 1) * dv, :], preferred_element_type=F32)
        o_ref[...] = y
    else:
        o_ref[...] = x_ref[...] + jnp.dot(a_ref[...].astype(BF16), w_ref[...], preferred_element_type=F32)


def _ret_out(x, a, w_out):
    n, d = x.shape
    tm = min(TOKEN_TILE, n)
    if a.ndim == 3:
        a_spec = pl.BlockSpec((a.shape[0], tm, a.shape[2]), lambda i: (0, i, 0))
    else:
        a_spec = pl.BlockSpec((tm, a.shape[1]), lambda i: (i, 0))
    v = w_out.shape[0]
    return pl.pallas_call(
        _ret_out_kernel,
        grid=(n // tm,),
        in_specs=[pl.BlockSpec((tm, d), lambda i: (i, 0)),
                  a_spec,
                  pl.BlockSpec((v, d), lambda i: (0, 0), pipeline_mode=pl.Buffered(1))],
        out_specs=pl.BlockSpec((tm, d), lambda i: (i, 0)),
        out_shape=jax.ShapeDtypeStruct((n, d), F32),
        compiler_params=_params(1),
        name="ret_out",
    )(x, a, w_out)


def _decay_consts(chunk, rows, heads, dk, dv):
    log_g = np.log1p(-(2.0 ** (-5.0 - np.arange(heads, dtype=np.float64))))
    n = np.arange(rows, dtype=np.float64)
    live = n < chunk
    diff = n[:, None] - n[None, :]
    intra = np.where((diff >= 0) & live[:, None] & live[None, :],
                     np.exp(log_g[:, None, None] * np.maximum(diff, 0.0)), 0.0)
    cross = np.where(live, np.exp(log_g[:, None] * (n + 1.0)), 0.0)
    kdec = np.where(live, np.exp(log_g[:, None] * (chunk - 1.0 - n)), 0.0)
    sdec = np.exp(log_g * chunk)
    f32 = lambda a: jnp.asarray(np.ascontiguousarray(a), dtype=F32)
    return (f32(intra),
            f32(np.broadcast_to(cross[:, :, None], (heads, rows, dv))),
            f32(np.broadcast_to(kdec[:, :, None], (heads, rows, dk))),
            f32(np.broadcast_to(sdec[:, None, None], (heads, 1, dv))))


def _rotary_tables(pos, dk):
    half = dk // 2
    inv = ROPE_BASE ** (-np.arange(half, dtype=np.float64) / half)
    ang = np.asarray(pos, dtype=np.float64)[:, None] * inv[None, :]
    cos, sin = np.cos(ang), np.sin(ang)
    scale = dk ** -0.5
    f32 = lambda a: jnp.asarray(a, dtype=F32)
    return (f32(np.concatenate([cos, cos * scale, np.ones_like(cos)])),
            f32(np.concatenate([sin, sin * scale, np.zeros_like(sin)])))


def kernel(x_prompt, x_sample, p_prompt, p_sample, state_pool, state_ret, state_conv, norm_mix, norm_ffn,
           norm_ple, norm_final, pool_w, pool_scale, ret_w_in, ret_w_out, ffn_w_up, ffn_conv_w,
           ffn_conv_b, ffn_w_down, ple_w_proj, ple_w_gate):
    b, t, d = x_prompt.shape
    bs, ts, _ = x_sample.shape
    depth = norm_mix.shape[0]
    heads = RET_HEADS
    dk = d // heads
    dv = 2 * dk
    qk = heads * dk
    f2 = ffn_w_up.shape[2]
    ffn = f2 // 2
    tp = SUBLANES
    assert ts <= tp and ts >= CONV_W - 1 and PAST_LEN >= POOL_BUF

    row2 = lambda a: a.reshape(1, -1)
    t_major = lambda a: jnp.swapaxes(a, 0, 1).reshape(-1, a.shape[-1])
    b_major = lambda a: jnp.swapaxes(a.reshape(-1, bs, a.shape[-1]), 0, 1)

    xp = x_prompt.reshape(b * t, d)
    xs = t_major(x_sample)
    pp = p_prompt.reshape(depth, b * t, -1)
    ps = jnp.swapaxes(p_sample, 1, 2).reshape(depth, ts * bs, -1)

    tables_p = _rotary_tables(np.arange(t), dk)
    tables_s = _rotary_tables(PAST_LEN + np.repeat(np.arange(ts), bs), dk)
    chunk_p = RET_CHUNK if t % RET_CHUNK == 0 else t
    consts_p = _decay_consts(chunk_p, chunk_p, heads, dk, dv)
    consts_s = _decay_consts(ts, tp, heads, dk, dv)

    new_pool_p, new_pool_s, new_ret_p, new_ret_s, new_conv_p, new_conv_s = [], [], [], [], [], []
    for i in range(depth):
        jm = i // 2
        if i % 2 == 0:
            w = pool_w[jm]
            g = row2(norm_mix[i])
            sc = row2(pool_scale[jm])
            xp, hl = _pool_mixer(xp, None, g, w, sc, shift=1, pos0=0, rows_per_seq=t)
            new_pool_p.append(hl[:, -POOL_BUF:, :])
            halo_s = t_major(state_pool[jm])
            xs, h_s = _pool_mixer(xs, halo_s, g, w, sc, shift=bs, pos0=PAST_LEN, rows_per_seq=ts * bs)
            new_pool_s.append(jnp.concatenate([state_pool[jm], b_major(h_s)], axis=1)[:, -POOL_BUF:, :])
        else:
            w_in = ret_w_in[jm]
            w_out = ret_w_out[jm].astype(BF16)
            g = row2(norm_mix[i])
            proj_p = _ret_in(xp, g, w_in, tables_p, qk_width=qk, dk=dk, rows_per_seq=t, out_dtype=BF16,
                             head_major=True)
            a_p, s_p = _ret_prompt(proj_p, consts_p, batch=b, seq=t, heads=heads, dk=dk, dv=dv)
            xp = _ret_out(xp, a_p, w_out)
            new_ret_p.append(s_p)
            proj_s = _ret_in(xs, g, w_in, tables_s, qk_width=qk, dk=dk, rows_per_seq=ts * bs, out_dtype=F32,
                             head_major=False)
            proj_s = jnp.pad(b_major(proj_s), ((0, 0), (0, tp - ts), (0, 0)))
            a_s, s_s = _ret_sample(proj_s, state_ret[jm], consts_s, heads=heads, dk=dk, dv=dv)
            xs = _ret_out(xs, t_major(a_s[:, :ts, :]), w_out)
            new_ret_s.append(s_s)

        g = row2(norm_ffn[i])
        cb = row2(ffn_conv_b[i])
        xp, tg, tu = _conv_ffn(xp, g, None, ffn_w_up, ffn_conv_w, cb, ffn_w_down, i, shift=1, rows_per_seq=t)
        tiles_per_seq = tg.shape[0] // b
        last = jnp.concatenate([tg, tu], axis=-1)[tiles_per_seq - 1::tiles_per_seq]
        new_conv_p.append(last[:, -(CONV_W - 1):, :])
        st = t_major(state_conv[i])
        xs, tg, tu = _conv_ffn(xs, g, st, ffn_w_up, ffn_conv_w, cb, ffn_w_down, i, shift=bs,
                               rows_per_seq=ts * bs)
        new_conv_s.append(b_major(jnp.concatenate([tg[0], tu[0]], axis=-1)))

        g = row2(norm_ple[i])
        g_final = row2(norm_final) if i == depth - 1 else None
        xp = _ple(xp, pp, g, ple_w_gate, ple_w_proj, i, g_final)
        xs = _ple(xs, ps, g, ple_w_gate, ple_w_proj, i, g_final)

    return (xp.reshape(b, t, d), b_major(xs),
            jnp.stack(new_pool_p), jnp.stack(new_pool_s),
            jnp.stack(new_ret_p), jnp.stack(new_ret_s),
            jnp.stack(new_conv_p), jnp.stack(new_conv_s))
```

```python
import functools

import jax
import jax.numpy as jnp
import numpy as np
from jax import lax
from jax.experimental import pallas as pl
from jax.experimental.pallas import tpu as pltpu

F32 = jnp.float32
BF16 = jnp.bfloat16

EPS = 1e-6
POOL_WINDOWS = (2, 4, 8, 16)
POOL_BUF = max(POOL_WINDOWS) - 1
RET_HEADS = 8
RET_CHUNK = 256
ROPE_BASE = 10000.0
CONV_W = 3
PAST_LEN = 16384

TOKEN_TILE = 512
WIDE_TILE = 1024
FFN_CHUNK = 512
ROW_BLOCK = 512
PROJ_CHUNK = 1024
SAMPLE_SEQS = 2
SUBLANES = 8
VMEM_LIMIT = 60 * 1024 * 1024


def _params(n_axes):
    return pltpu.CompilerParams(
        dimension_semantics=("arbitrary",) * n_axes, vmem_limit_bytes=VMEM_LIMIT)


def _rmsnorm(x, g):
    return x * lax.rsqrt(jnp.mean(x * x, axis=-1, keepdims=True) + EPS) * g


def _shift_rows(a, rows):
    return pltpu.roll(a, rows, axis=0)


def _pool_group(h, prev, x, w, sc, pos, win, shift):
    halo = prev.shape[0]
    a = jnp.concatenate([prev, h], axis=0)
    span = 1
    while span < win:
        a = a + _shift_rows(a, span * shift)
        span *= 2
    cnt = jnp.minimum(pos + 1, win).astype(F32)
    dlt = a[halo:, :] / cnt - h
    return x + jnp.dot(dlt.astype(BF16), w, preferred_element_type=F32) * sc


def _pool_prompt_kernel(x_ref, g_ref, w_ref, sc_ref, o_ref, h_ref, carry_ref, *, tiles_per_seq):
    tm, d = x_ref.shape
    gw = d // len(POOL_WINDOWS)
    halo = carry_ref.shape[0]
    j = pl.program_id(0) % tiles_per_seq
    x = x_ref[...]
    h = _rmsnorm(x, g_ref[...])
    last = h[tm - halo:, :]
    h_ref[...] = last
    pos = j * tm + lax.broadcasted_iota(jnp.int32, (tm, 1), 0)
    for gi, win in enumerate(POOL_WINDOWS):
        cols = slice(gi * gw, (gi + 1) * gw)
        prev = jnp.where(j == 0, 0.0, carry_ref[:, cols])
        o_ref[:, cols] = _pool_group(h[:, cols], prev, x[:, cols], w_ref[gi], sc_ref[:, cols], pos, win, 1)
    carry_ref[...] = last


def _pool_state_kernel(x_ref, halo_ref, g_ref, w_ref, sc_ref, o_ref, h_ref, hs_ref, *, shift, pos0):
    tm, d = x_ref.shape
    gw = o_ref.shape[1]
    grp = pl.program_id(0)

    @pl.when(grp == 0)
    def _():
        h = _rmsnorm(x_ref[...], g_ref[...])
        hs_ref[...] = h
        h_ref[...] = h

    row = lax.broadcasted_iota(jnp.int32, (tm, 1), 0)
    step = jnp.zeros_like(row)
    for k in range(1, tm // shift):
        step = step + (row >= k * shift).astype(jnp.int32)
    pos = pos0 + step

    for gi, win in enumerate(POOL_WINDOWS):
        @pl.when(grp == gi)
        def _(gi=gi, win=win):
            cols = slice(gi * gw, (gi + 1) * gw)
            prev = jnp.concatenate([jnp.zeros((shift, gw), F32), halo_ref[...]], axis=0)
            o_ref[...] = _pool_group(hs_ref[:, cols], prev, x_ref[:, cols], w_ref[...], sc_ref[...],
                                     pos, win, shift)


def _pool_mixer(x, halo, g, w, scale, *, shift, pos0, rows_per_seq):
    n, d = x.shape
    ng = len(POOL_WINDOWS)
    gw = d // ng
    tm = min(TOKEN_TILE, n)
    if halo is None:
        assert shift == 1 and pos0 == 0
        tiles_per_seq = rows_per_seq // tm
        halo_rows = POOL_BUF + 1
        full = lambda a: pl.BlockSpec(a.shape, lambda i: (0,) * a.ndim)
        tile = pl.BlockSpec((tm, d), lambda i: (i, 0))
        return pl.pallas_call(
            functools.partial(_pool_prompt_kernel, tiles_per_seq=tiles_per_seq),
            grid=(n // tm,),
            in_specs=[tile, full(g), full(w), full(scale)],
            out_specs=[tile, pl.BlockSpec((None, halo_rows, d), lambda i: (i // tiles_per_seq, 0, 0))],
            out_shape=[jax.ShapeDtypeStruct((n, d), F32),
                       jax.ShapeDtypeStruct((n // rows_per_seq, halo_rows, d), F32)],
            scratch_shapes=[pltpu.VMEM((halo_rows, d), F32)],
            compiler_params=_params(1),
            name="pool_mixer",
        )(x, g, w, scale)
    assert n == tm
    return pl.pallas_call(
        functools.partial(_pool_state_kernel, shift=shift, pos0=pos0),
        grid=(ng,),
        in_specs=[pl.BlockSpec((tm, d), lambda gi: (0, 0)),
                  pl.BlockSpec((halo.shape[0], gw), lambda gi: (0, gi)),
                  pl.BlockSpec((1, d), lambda gi: (0, 0)),
                  pl.BlockSpec((None, gw, gw), lambda gi: (gi, 0, 0)),
                  pl.BlockSpec((1, gw), lambda gi: (0, gi))],
        out_specs=[pl.BlockSpec((tm, gw), lambda gi: (0, gi)), pl.BlockSpec((tm, d), lambda gi: (0, 0))],
        out_shape=[jax.ShapeDtypeStruct((n, d), F32), jax.ShapeDtypeStruct((n, d), F32)],
        scratch_shapes=[pltpu.VMEM((tm, d), F32)],
        compiler_params=_params(1),
        name="pool_mixer_state",
    )(x, halo, g, w, scale)


def _ffn_kernel(*refs, shift, tiles_per_seq, from_state):
    if from_state:
        (x_ref, g_ref, sg_ref, su_ref, wg_ref, wu_ref, cwg_ref, cwu_ref, cbg_ref, cbu_ref, wd_ref,
         o_ref, tg_ref, tu_ref, hs_ref) = refs
        carry_g = carry_u = None
    else:
        (x_ref, g_ref, wg_ref, wu_ref, cwg_ref, cwu_ref, cbg_ref, cbu_ref, wd_ref,
         o_ref, tg_ref, tu_ref, hs_ref, carry_g, carry_u) = refs
        sg_ref = su_ref = None
    tm, d = x_ref.shape
    halo = max((CONV_W - 1) * shift, SUBLANES)
    i = pl.program_id(0)
    f = pl.program_id(1)
    first_of_seq = (i % tiles_per_seq) == 0

    @pl.when(f == 0)
    def _():
        x = x_ref[...]
        hs_ref[...] = _rmsnorm(x, g_ref[...]).astype(BF16)
        o_ref[...] = x

    rb = min(ROW_BLOCK, tm)
    cwg, cwu, cbg, cbu = cwg_ref[...], cwu_ref[...], cbg_ref[...], cbu_ref[...]
    if from_state:
        assert rb == tm
        prev_g, prev_u = sg_ref[...], su_ref[...]
    else:
        prev_g = jnp.where(first_of_seq, 0.0, carry_g[f])
        prev_u = jnp.where(first_of_seq, 0.0, carry_u[f])

    def up(r):
        hs = hs_ref[r:r + rb, :]
        return (jnp.dot(hs, wg_ref[...], preferred_element_type=F32),
                jnp.dot(hs, wu_ref[...], preferred_element_type=F32))

    def conv(prev, u, cw, cb):
        ext = jnp.concatenate([prev, u], axis=0)
        if shift % SUBLANES == 0:
            u1 = ext[halo - shift:halo - shift + rb, :]
            u2 = ext[halo - 2 * shift:halo - 2 * shift + rb, :]
        else:
            u1 = _shift_rows(ext, shift)[halo:, :]
            u2 = _shift_rows(ext, 2 * shift)[halo:, :]
        return cw[0:1, :] * u2 + cw[1:2, :] * u1 + cw[2:3, :] * u + cb

    nxt = up(0)
    for r in range(0, tm, rb):
        ug, uu = nxt
        if r + rb < tm:
            nxt = up(r + rb)
        cg = conv(prev_g, ug, cwg, cbg)
        cu = conv(prev_u, uu, cwu, cbu)
        prev_g, prev_u = ug[rb - halo:, :], uu[rb - halo:, :]
        act = ((cg / (1.0 + jnp.exp(-cg))) * cu).astype(BF16)
        o_ref[r:r + rb, :] += jnp.dot(act, wd_ref[...], preferred_element_type=F32)
    if not from_state:
        carry_g[f] = prev_g
        carry_u[f] = prev_u
    tg_ref[...] = prev_g
    tu_ref[...] = prev_u


def _conv_ffn(x, g, state, w_up, cw, cb, w_down, layer, *, shift, rows_per_seq):
    n, d = x.shape
    f2 = w_up.shape[2]
    ffn = f2 // 2
    tm = min(WIDE_TILE, n)
    fn = FFN_CHUNK
    nf = ffn // fn
    n_tiles = n // tm
    tiles_per_seq = max(rows_per_seq // tm, 1)
    halo = max((CONV_W - 1) * shift, SUBLANES)
    from_state = state is not None

    tile = pl.BlockSpec((tm, d), lambda i, f: (i, 0))
    tile_in = pl.BlockSpec((tm, d), lambda i, f: (i, 0), pipeline_mode=pl.Buffered(1))
    gate_cols = lambda rows: pl.BlockSpec((rows, fn), lambda i, f: (0, f))
    up_cols = lambda rows: pl.BlockSpec((rows, fn), lambda i, f: (0, f + nf))
    layer_gate = lambda rows: pl.BlockSpec((None, rows, fn), lambda i, f: (layer, 0, f))
    layer_up = lambda rows: pl.BlockSpec((None, rows, fn), lambda i, f: (layer, 0, f + nf))
    in_specs = [tile_in, pl.BlockSpec((1, d), lambda i, f: (0, 0))]
    args = [x, g]
    if from_state:
        in_specs += [gate_cols(halo), up_cols(halo)]
        args += [state, state]
    in_specs += [layer_gate(d), layer_up(d), layer_gate(CONV_W), layer_up(CONV_W), gate_cols(1), up_cols(1),
                 pl.BlockSpec((None, fn, d), lambda i, f: (layer, f, 0))]
    args += [w_up, w_up, cw, cw, cb, cb, w_down]
    tail_spec = pl.BlockSpec((None, halo, fn), lambda i, f: (i, 0, f))
    tail_shape = jax.ShapeDtypeStruct((n_tiles, halo, ffn), F32)
    scratch = [pltpu.VMEM((tm, d), BF16)]
    if not from_state:
        scratch += [pltpu.VMEM((nf, halo, fn), F32), pltpu.VMEM((nf, halo, fn), F32)]
    return pl.pallas_call(
        functools.partial(_ffn_kernel, shift=shift, tiles_per_seq=tiles_per_seq, from_state=from_state),
        grid=(n_tiles, nf),
        in_specs=in_specs,
        out_specs=[tile, tail_spec, tail_spec],
        out_shape=[jax.ShapeDtypeStruct((n, d), F32), tail_shape, tail_shape],
        scratch_shapes=scratch,
        compiler_params=_params(2),
        name="conv_ffn_state" if from_state else "conv_ffn",
    )(*args)


def _ple_kernel(*refs, final):
    if final:
        x_ref, p_ref, g_ref, wg_ref, wp_ref, gf_ref, o_ref = refs
    else:
        x_ref, p_ref, g_ref, wg_ref, wp_ref, o_ref = refs
    x = x_ref[...]
    hn = _rmsnorm(x, g_ref[...]).astype(BF16)
    gate = jax.nn.sigmoid(jnp.dot(hn, wg_ref[...], preferred_element_type=F32))
    emb = jnp.dot(p_ref[...].astype(BF16), wp_ref[...], preferred_element_type=F32)
    y = x + gate * emb
    if final:
        y = _rmsnorm(y, gf_ref[...])
    o_ref[...] = y


def _ple(x, p, g, w_gate, w_proj, layer, g_final):
    n, d = x.shape
    tm = min(TOKEN_TILE, n)
    final = g_final is not None
    const = lambda a: pl.BlockSpec(a.shape, lambda i: (0, 0))
    resident = lambda a: pl.BlockSpec((None,) + a.shape[1:], lambda i: (layer, 0, 0),
                                      pipeline_mode=pl.Buffered(1))
    in_specs = [pl.BlockSpec((tm, d), lambda i: (i, 0)),
                pl.BlockSpec((None, tm, p.shape[2]), lambda i: (layer, i, 0)),
                const(g), resident(w_gate), resident(w_proj)]
    args = [x, p, g, w_gate, w_proj]
    if final:
        in_specs.append(const(g_final))
        args.append(g_final)
    return pl.pallas_call(
        functools.partial(_ple_kernel, final=final),
        grid=(n // tm,),
        in_specs=in_specs,
        out_specs=pl.BlockSpec((tm, d), lambda i: (i, 0)),
        out_shape=jax.ShapeDtypeStruct((n, d), F32),
        compiler_params=_params(1),
        name="ple_final" if final else "ple",
    )(*args)


def _ret_in_kernel(x_ref, g_ref, w_ref, cos_ref, sin_ref, o_ref, hs_ref, *, dk):
    @pl.when(pl.program_id(1) == 0)
    def _():
        hs_ref[...] = _rmsnorm(x_ref[...], g_ref[...]).astype(BF16)

    tm = hs_ref.shape[0]
    rb = min(ROW_BLOCK, tm)
    half = dk // 2
    project = lambda r: jnp.dot(hs_ref[r:r + rb, :], w_ref[...], preferred_element_type=F32)
    nxt = project(0)
    for r in range(0, tm, rb):
        proj = nxt
        if r + rb < tm:
            nxt = project(r + rb)
        cos = cos_ref[r:r + rb, :]
        sin = sin_ref[r:r + rb, :]
        for hh in range(proj.shape[1] // dk):
            x1 = proj[:, hh * dk:hh * dk + half]
            x2 = proj[:, hh * dk + half:(hh + 1) * dk]
            y1 = (x1 * cos - x2 * sin).astype(o_ref.dtype)
            y2 = (x1 * sin + x2 * cos).astype(o_ref.dtype)
            if len(o_ref.shape) == 3:
                o_ref[hh, r:r + rb, 0:half] = y1
                o_ref[hh, r:r + rb, half:dk] = y2
            else:
                o_ref[r:r + rb, hh * dk:hh * dk + half] = y1
                o_ref[r:r + rb, hh * dk + half:(hh + 1) * dk] = y2


def _ret_in(x, g, w_in, tables, *, qk_width, dk, rows_per_seq, out_dtype, head_major):
    n, d = x.shape
    e = w_in.shape[1]
    tm = min(WIDE_TILE, n)
    nc = PROJ_CHUNK
    qc = qk_width // nc
    tps = rows_per_seq // tm

    def table_block(i, c):
        return (jnp.where(c < qc, 0, jnp.where(c < 2 * qc, tps, 2 * tps)) + i % tps, 0)

    if head_major:
        out_spec = pl.BlockSpec((nc // dk, tm, dk), lambda i, c: (c, i, 0))
        out_shape = jax.ShapeDtypeStruct((e // dk, n, dk), out_dtype)
    else:
        out_spec = pl.BlockSpec((tm, nc), lambda i, c: (i, c))
        out_shape = jax.ShapeDtypeStruct((n, e), out_dtype)

    return pl.pallas_call(
        functools.partial(_ret_in_kernel, dk=dk),
        grid=(n // tm, e // nc),
        in_specs=[pl.BlockSpec((tm, d), lambda i, c: (i, 0)),
                  pl.BlockSpec((1, d), lambda i, c: (0, 0)),
                  pl.BlockSpec((d, nc), lambda i, c: (0, c)),
                  pl.BlockSpec((tm, dk // 2), table_block),
                  pl.BlockSpec((tm, dk // 2), table_block)],
        out_specs=out_spec,
        out_shape=out_shape,
        scratch_shapes=[pltpu.VMEM((tm, d), BF16)],
        compiler_params=_params(2),
        name="ret_in",
    )(x, g, w_in, *tables)


def _groupnorm_gate(o, gate):
    mu = jnp.mean(o, axis=-1, keepdims=True)
    var = jnp.mean(jnp.square(o - mu), axis=-1, keepdims=True)
    on = (o - mu) * lax.rsqrt(var + EPS)
    gf = gate.astype(F32)
    return (gf / (1.0 + jnp.exp(-gf))) * on


def _ret_prompt_kernel(q_ref, k_ref, v_ref, g_ref, intra_ref, cross_ref, kdec_ref, sdec_ref,
                       o_ref, s_out_ref, s_ref, *, chunk):
    t = q_ref.shape[0]
    s_ref[...] = jnp.zeros(s_ref.shape, F32)
    intra = intra_ref[...]
    cross = cross_ref[...]
    kdec = kdec_ref[...]
    sdec = sdec_ref[...]

    wide = lambda ref, rows: jnp.concatenate([ref[i, rows, :] for i in range(ref.shape[0])], axis=-1)

    def body(c, carry):
        rows = pl.ds(pl.multiple_of(c * chunk, chunk), chunk)
        q = q_ref[rows, :]
        k = k_ref[rows, :]
        v = wide(v_ref, rows)
        s = s_ref[...]
        sc = lax.dot_general(q, k, (((1,), (1,)), ((), ())), preferred_element_type=F32) * intra
        o = (jnp.dot(sc.astype(BF16), v, preferred_element_type=F32)
             + jnp.dot(q, s.astype(BF16), preferred_element_type=F32) * cross)
        kd = (k.astype(F32) * kdec).astype(BF16)
        s_ref[...] = sdec * s + lax.dot_general(kd, v, (((0,), (0,)), ((), ())),
                                                preferred_element_type=F32)
        o_ref[rows, :] = _groupnorm_gate(o, wide(g_ref, rows)).astype(o_ref.dtype)
        return carry

    lax.fori_loop(0, t // chunk, body, 0, unroll=2)
    s_out_ref[...] = s_ref[...]


def _ret_prompt(proj, consts, *, batch, seq, heads, dk, dv):
    intra, cross, kdec, sdec = consts
    chunk = intra.shape[1]
    per = dv // dk
    v_off = 2 * heads // per
    g_off = v_off + heads
    head_const = lambda a: pl.BlockSpec((None,) + a.shape[1:], lambda b, h: (h, 0, 0))
    return pl.pallas_call(
        functools.partial(_ret_prompt_kernel, chunk=chunk),
        grid=(batch, heads),
        in_specs=[pl.BlockSpec((None, seq, dk), lambda b, h: (h, b, 0)),
                  pl.BlockSpec((None, seq, dk), lambda b, h: (heads + h, b, 0)),
                  pl.BlockSpec((per, seq, dk), lambda b, h: (v_off + h, b, 0)),
                  pl.BlockSpec((per, seq, dk), lambda b, h: (g_off + h, b, 0)),
                  head_const(intra), head_const(cross), head_const(kdec), head_const(sdec)],
        out_specs=[pl.BlockSpec((None, seq, dv), lambda b, h: (h, b, 0)),
                   pl.BlockSpec((None, None, dk, dv), lambda b, h: (b, h, 0, 0))],
        out_shape=[jax.ShapeDtypeStruct((heads, batch * seq, dv), BF16),
                   jax.ShapeDtypeStruct((batch, heads, dk, dv), F32)],
        scratch_shapes=[pltpu.VMEM((dk, dv), F32)],
        compiler_params=_params(2),
        name="retention_prompt",
    )(proj, proj, proj, proj, intra, cross, kdec, sdec)


def _ret_sample_kernel(p_ref, s_ref, intra_ref, cross_ref, kdec_ref, sdec_ref, o_ref, s_out_ref,
                       *, heads, dk, dv):
    qk = heads * dk
    for n in range(p_ref.shape[0]):
        for h in range(heads):
            q = p_ref[n, :, h * dk:(h + 1) * dk].astype(BF16)
            kf = p_ref[n, :, qk + h * dk:qk + (h + 1) * dk]
            v = p_ref[n, :, 2 * qk + h * dv:2 * qk + (h + 1) * dv].astype(BF16)
            gate = p_ref[n, :, 2 * qk + heads * dv + h * dv:2 * qk + heads * dv + (h + 1) * dv]
            s = s_ref[n, h]
            sc = lax.dot_general(q, kf.astype(BF16), (((1,), (1,)), ((), ())),
                                 preferred_element_type=F32) * intra_ref[h]
            o = (jnp.dot(sc.astype(BF16), v, preferred_element_type=F32)
                 + jnp.dot(q, s.astype(BF16), preferred_element_type=F32) * cross_ref[h])
            kd = (kf * kdec_ref[h]).astype(BF16)
            s_out_ref[n, h] = sdec_ref[h] * s + lax.dot_general(kd, v, (((0,), (0,)), ((), ())),
                                                                preferred_element_type=F32)
            o_ref[n, :, h * dv:(h + 1) * dv] = _groupnorm_gate(o, gate).astype(o_ref.dtype)


def _ret_sample(proj, state, consts, *, heads, dk, dv):
    intra, cross, kdec, sdec = consts
    b, tp, e = proj.shape
    nb = SAMPLE_SEQS
    const = lambda a: pl.BlockSpec(a.shape, lambda i: (0,) * a.ndim)
    return pl.pallas_call(
        functools.partial(_ret_sample_kernel, heads=heads, dk=dk, dv=dv),
        grid=(b // nb,),
        in_specs=[pl.BlockSpec((nb, tp, e), lambda i: (i, 0, 0)),
                  pl.BlockSpec((nb, heads, dk, dv), lambda i: (i, 0, 0, 0)),
                  const(intra), const(cross), const(kdec), const(sdec)],
        out_specs=[pl.BlockSpec((nb, tp, heads * dv), lambda i: (i, 0, 0)),
                   pl.BlockSpec((nb, heads, dk, dv), lambda i: (i, 0, 0, 0))],
        out_shape=[jax.ShapeDtypeStruct((b, tp, heads * dv), F32),
                   jax.ShapeDtypeStruct((b, heads, dk, dv), F32)],
        compiler_params=_params(1),
        name="retention_sample",
    )(proj, state, intra, cross, kdec, sdec)


def _ret_out_kernel(x_ref, a_ref, w_ref, o_ref):
    if len(a_ref.shape) == 3:
        dv = a_ref.shape[2]
        y = x_ref[...]
        for h in range(a_ref.shape[0]):
            y = y + jnp.dot(a_ref[h], w_ref[h * dv:(h + 1) * dv, :], preferred_element_type=F32)
        o_ref[...] = y
    else:
        o_ref[...] = x_ref[...] + jnp.dot(a_ref[...].astype(BF16), w_ref[...], preferred_element_type=F32)


def _ret_out(x, a, w_out):
    n, d = x.shape
    tm = min(TOKEN_TILE, n)
    if a.ndim == 3:
        a_spec = pl.BlockSpec((a.shape[0], tm, a.shape[2]), lambda i: (0, i, 0))
    else:
        a_spec = pl.BlockSpec((tm, a.shape[1]), lambda i: (i, 0))
    v = w_out.shape[0]
    return pl.pallas_call(
        _ret_out_kernel,
        grid=(n // tm,),
        in_specs=[pl.BlockSpec((tm, d), lambda i: (i, 0)),
                  a_spec,
                  pl.BlockSpec((v, d), lambda i: (0, 0), pipeline_mode=pl.Buffered(1))],
        out_specs=pl.BlockSpec((tm, d), lambda i: (i, 0)),
        out_shape=jax.ShapeDtypeStruct((n, d), F32),
        compiler_params=_params(1),
        name="ret_out",
    )(x, a, w_out)


def _decay_consts(chunk, heads, dk, dv):
    log_g = np.log1p(-(2.0 ** (-5.0 - np.arange(heads, dtype=np.float64))))
    n = np.arange(chunk, dtype=np.float64)
    diff = n[:, None] - n[None, :]
    intra = np.where(diff >= 0, np.exp(log_g[:, None, None] * np.maximum(diff, 0.0)), 0.0)
    cross = np.exp(log_g[:, None] * (n + 1.0))
    kdec = np.exp(log_g[:, None] * (chunk - 1.0 - n))
    sdec = np.exp(log_g * chunk)
    f32 = lambda a: jnp.asarray(np.ascontiguousarray(a), dtype=F32)
    return (f32(intra),
            f32(np.broadcast_to(cross[:, :, None], (heads, chunk, dv))),
            f32(np.broadcast_to(kdec[:, :, None], (heads, chunk, dk))),
            f32(np.broadcast_to(sdec[:, None, None], (heads, 1, dv))))


def _rotary_tables(pos, dk):
    half = dk // 2
    inv = ROPE_BASE ** (-np.arange(half, dtype=np.float64) / half)
    ang = np.asarray(pos, dtype=np.float64)[:, None] * inv[None, :]
    cos, sin = np.cos(ang), np.sin(ang)
    scale = dk ** -0.5
    f32 = lambda a: jnp.asarray(a, dtype=F32)
    return (f32(np.concatenate([cos, cos * scale, np.ones_like(cos)])),
            f32(np.concatenate([sin, sin * scale, np.zeros_like(sin)])))


def kernel(x_prompt, x_sample, p_prompt, p_sample, state_pool, state_ret, state_conv, norm_mix, norm_ffn,
           norm_ple, norm_final, pool_w, pool_scale, ret_w_in, ret_w_out, ffn_w_up, ffn_conv_w,
           ffn_conv_b, ffn_w_down, ple_w_proj, ple_w_gate):
    b, t, d = x_prompt.shape
    bs, ts, _ = x_sample.shape
    depth = norm_mix.shape[0]
    heads = RET_HEADS
    dk = d // heads
    dv = 2 * dk
    qk = heads * dk
    f2 = ffn_w_up.shape[2]
    ffn = f2 // 2
    assert ts >= CONV_W - 1 and PAST_LEN >= POOL_BUF

    row2 = lambda a: a.reshape(1, -1)
    t_major = lambda a: jnp.swapaxes(a, 0, 1).reshape(-1, a.shape[-1])
    b_major = lambda a: jnp.swapaxes(a.reshape(-1, bs, a.shape[-1]), 0, 1)

    xp = x_prompt.reshape(b * t, d)
    xs = t_major(x_sample)
    pp = p_prompt.reshape(depth, b * t, -1)
    ps = jnp.swapaxes(p_sample, 1, 2).reshape(depth, ts * bs, -1)

    tables_p = _rotary_tables(np.arange(t), dk)
    tables_s = _rotary_tables(PAST_LEN + np.repeat(np.arange(ts), bs), dk)
    chunk_p = RET_CHUNK if t % RET_CHUNK == 0 else t
    consts_p = _decay_consts(chunk_p, heads, dk, dv)
    consts_s = _decay_consts(ts, heads, dk, dv)

    new_pool_p, new_pool_s, new_ret_p, new_ret_s, new_conv_p, new_conv_s = [], [], [], [], [], []
    for i in range(depth):
        jm = i // 2
        if i % 2 == 0:
            w = pool_w[jm]
            g = row2(norm_mix[i])
            sc = row2(pool_scale[jm])
            xp, hl = _pool_mixer(xp, None, g, w, sc, shift=1, pos0=0, rows_per_seq=t)
            new_pool_p.append(hl[:, -POOL_BUF:, :])
            halo_s = t_major(state_pool[jm])
            xs, h_s = _pool_mixer(xs, halo_s, g, w, sc, shift=bs, pos0=PAST_LEN, rows_per_seq=ts * bs)
            new_pool_s.append(jnp.concatenate([state_pool[jm], b_major(h_s)], axis=1)[:, -POOL_BUF:, :])
        else:
            w_in = ret_w_in[jm]
            w_out = ret_w_out[jm].astype(BF16)
            g = row2(norm_mix[i])
            proj_p = _ret_in(xp, g, w_in, tables_p, qk_width=qk, dk=dk, rows_per_seq=t, out_dtype=BF16,
                             head_major=True)
            a_p, s_p = _ret_prompt(proj_p, consts_p, batch=b, seq=t, heads=heads, dk=dk, dv=dv)
            xp = _ret_out(xp, a_p, w_out)
            new_ret_p.append(s_p)
            proj_s = _ret_in(xs, g, w_in, tables_s, qk_width=qk, dk=dk, rows_per_seq=ts * bs, out_dtype=F32,
                             head_major=False)
            a_s, s_s = _ret_sample(b_major(proj_s), state_ret[jm], consts_s, heads=heads, dk=dk, dv=dv)
            xs = _ret_out(xs, t_major(a_s), w_out)
            new_ret_s.append(s_s)

        g = row2(norm_ffn[i])
        cb = row2(ffn_conv_b[i])
        xp, tg, tu = _conv_ffn(xp, g, None, ffn_w_up, ffn_conv_w, cb, ffn_w_down, i, shift=1, rows_per_seq=t)
        tiles_per_seq = tg.shape[0] // b
        last = jnp.concatenate([tg, tu], axis=-1)[tiles_per_seq - 1::tiles_per_seq]
        new_conv_p.append(last[:, -(CONV_W - 1):, :])
        st = t_major(state_conv[i])
        xs, tg, tu = _conv_ffn(xs, g, st, ffn_w_up, ffn_conv_w, cb, ffn_w_down, i, shift=bs,
                               rows_per_seq=ts * bs)
        new_conv_s.append(b_major(jnp.concatenate([tg[0], tu[0]], axis=-1)))

        g = row2(norm_ple[i])
        g_final = row2(norm_final) if i == depth - 1 else None
        xp = _ple(xp, pp, g, ple_w_gate, ple_w_proj, i, g_final)
        xs = _ple(xs, ps, g, ple_w_gate, ple_w_proj, i, g_final)

    return (xp.reshape(b, t, d), b_major(xs),
            jnp.stack(new_pool_p), jnp.stack(new_pool_s),
            jnp.stack(new_ret_p), jnp.stack(new_ret_s),
            jnp.stack(new_conv_p), jnp.stack(new_conv_s))
```

```python
import functools

import jax
import jax.numpy as jnp
import numpy as np
from jax import lax
from jax.experimental import pallas as pl
from jax.experimental.pallas import tpu as pltpu

F32 = jnp.float32
BF16 = jnp.bfloat16

EPS = 1e-6
POOL_WINDOWS = (2, 4, 8, 16)
POOL_BUF = max(POOL_WINDOWS) - 1
RET_HEADS = 8
RET_CHUNK = 256
ROPE_BASE = 10000.0
CONV_W = 3
PAST_LEN = 16384

TOKEN_TILE = 512
WIDE_TILE = 1024
FFN_CHUNK = 512
ROW_BLOCK = 512
PROJ_CHUNK = 1024
SAMPLE_SEQS = 2
SUBLANES = 8
VMEM_LIMIT = 60 * 1024 * 1024


def _params(n_axes):
    return pltpu.CompilerParams(
        dimension_semantics=("arbitrary",) * n_axes, vmem_limit_bytes=VMEM_LIMIT)


def _rmsnorm(x, g):
    return x * lax.rsqrt(jnp.mean(x * x, axis=-1, keepdims=True) + EPS) * g


def _shift_rows(a, rows):
    return pltpu.roll(a, rows, axis=0)


def _time_major(ref, cols=slice(None)):
    return jnp.concatenate([ref[:, t, cols] for t in range(ref.shape[1])], axis=0)


def _store_batch_major(ref, val, cols=slice(None)):
    nb = ref.shape[0]
    for t in range(ref.shape[1]):
        ref[:, t, cols] = val[t * nb:(t + 1) * nb, :]


def _pool_group(h, prev, x, w, sc, pos, win, shift):
    halo = prev.shape[0]
    a = jnp.concatenate([prev, h], axis=0)
    span = 1
    while span < win:
        a = a + _shift_rows(a, span * shift)
        span *= 2
    cnt = jnp.minimum(pos + 1, win).astype(F32)
    dlt = a[halo:, :] / cnt - h
    return x + jnp.dot(dlt.astype(BF16), w, preferred_element_type=F32) * sc


def _pool_prompt_kernel(x_ref, g_ref, w_ref, sc_ref, o_ref, h_ref, carry_ref, *, tiles_per_seq):
    tm, d = x_ref.shape
    gw = d // len(POOL_WINDOWS)
    halo = carry_ref.shape[0]
    j = pl.program_id(0) % tiles_per_seq
    x = x_ref[...]
    h = _rmsnorm(x, g_ref[...])
    last = h[tm - halo:, :]
    h_ref[...] = last
    pos = j * tm + lax.broadcasted_iota(jnp.int32, (tm, 1), 0)
    for gi, win in enumerate(POOL_WINDOWS):
        cols = slice(gi * gw, (gi + 1) * gw)
        prev = jnp.where(j == 0, 0.0, carry_ref[:, cols])
        o_ref[:, cols] = _pool_group(h[:, cols], prev, x[:, cols], w_ref[gi], sc_ref[:, cols], pos, win, 1)
    carry_ref[...] = last


def _pool_state_kernel(x_ref, state_ref, g_ref, w_ref, sc_ref, o_ref, ns_ref, hs_ref, *, pos0):
    shift, steps, d = x_ref.shape
    tm = shift * steps
    gw = o_ref.shape[1]
    kept = state_ref.shape[1]
    grp = pl.program_id(0)

    @pl.when(grp == 0)
    def _():
        hs_ref[...] = _rmsnorm(_time_major(x_ref), g_ref[...])

    row = lax.broadcasted_iota(jnp.int32, (tm, 1), 0)
    step = jnp.zeros_like(row)
    for k in range(1, steps):
        step = step + (row >= k * shift).astype(jnp.int32)
    pos = pos0 + step

    for gi, win in enumerate(POOL_WINDOWS):
        @pl.when(grp == gi)
        def _(gi=gi, win=win):
            cols = slice(gi * gw, (gi + 1) * gw)
            h = hs_ref[:, cols]
            prev = jnp.concatenate([jnp.zeros((shift, gw), F32), _time_major(state_ref)], axis=0)
            o_ref[...] = _pool_group(h, prev, _time_major(x_ref, cols), w_ref[...], sc_ref[...],
                                     pos, win, shift)
            ext = jnp.concatenate([prev, h], axis=0)
            _store_batch_major(ns_ref, ext[ext.shape[0] - kept * shift:, :])


def _pool_mixer(x, halo, g, w, scale, *, shift, pos0, rows_per_seq):
    d = x.shape[-1]
    ng = len(POOL_WINDOWS)
    gw = d // ng
    if halo is None:
        n = x.shape[0]
        tm = min(TOKEN_TILE, n)
        assert shift == 1 and pos0 == 0
        tiles_per_seq = rows_per_seq // tm
        halo_rows = POOL_BUF + 1
        full = lambda a: pl.BlockSpec(a.shape, lambda i: (0,) * a.ndim)
        tile = pl.BlockSpec((tm, d), lambda i: (i, 0))
        return pl.pallas_call(
            functools.partial(_pool_prompt_kernel, tiles_per_seq=tiles_per_seq),
            grid=(n // tm,),
            in_specs=[tile, full(g), full(w), full(scale)],
            out_specs=[tile, pl.BlockSpec((None, halo_rows, d), lambda i: (i // tiles_per_seq, 0, 0))],
            out_shape=[jax.ShapeDtypeStruct((n, d), F32),
                       jax.ShapeDtypeStruct((n // rows_per_seq, halo_rows, d), F32)],
            scratch_shapes=[pltpu.VMEM((halo_rows, d), F32)],
            compiler_params=_params(1),
            name="pool_mixer",
        )(x, g, w, scale)
    nb, steps, _ = x.shape
    kept = halo.shape[1]
    assert shift == nb and kept == POOL_BUF
    return pl.pallas_call(
        functools.partial(_pool_state_kernel, pos0=pos0),
        grid=(ng,),
        in_specs=[pl.BlockSpec((nb, steps, d), lambda gi: (0, 0, 0)),
                  pl.BlockSpec((nb, kept, gw), lambda gi: (0, 0, gi)),
                  pl.BlockSpec((1, d), lambda gi: (0, 0)),
                  pl.BlockSpec((None, gw, gw), lambda gi: (gi, 0, 0)),
                  pl.BlockSpec((1, gw), lambda gi: (0, gi))],
        out_specs=[pl.BlockSpec((nb * steps, gw), lambda gi: (0, gi)),
                   pl.BlockSpec((nb, kept, gw), lambda gi: (0, 0, gi))],
        out_shape=[jax.ShapeDtypeStruct((nb * steps, d), F32), jax.ShapeDtypeStruct((nb, kept, d), F32)],
        scratch_shapes=[pltpu.VMEM((nb * steps, d), F32)],
        compiler_params=_params(1),
        name="pool_mixer_state",
    )(x, halo, g, w, scale)


def _ffn_kernel(*refs, shift, tiles_per_seq, from_state):
    if from_state:
        (x_ref, g_ref, sg_ref, su_ref, wg_ref, wu_ref, cwg_ref, cwu_ref, cbg_ref, cbu_ref, wd_ref,
         o_ref, tail_ref, hs_ref) = refs
        carry_g = carry_u = tg_ref = tu_ref = None
    else:
        (x_ref, g_ref, wg_ref, wu_ref, cwg_ref, cwu_ref, cbg_ref, cbu_ref, wd_ref,
         o_ref, tg_ref, tu_ref, hs_ref, carry_g, carry_u) = refs
        sg_ref = su_ref = None
    tm, d = x_ref.shape
    halo = max((CONV_W - 1) * shift, SUBLANES)
    i = pl.program_id(0)
    f = pl.program_id(1)
    first_of_seq = (i % tiles_per_seq) == 0

    @pl.when(f == 0)
    def _():
        x = x_ref[...]
        hs_ref[...] = _rmsnorm(x, g_ref[...]).astype(BF16)
        o_ref[...] = x

    rb = min(ROW_BLOCK, tm)
    cwg, cwu, cbg, cbu = cwg_ref[...], cwu_ref[...], cbg_ref[...], cbu_ref[...]
    if from_state:
        assert rb == tm
        prev_g, prev_u = _time_major(sg_ref), _time_major(su_ref)
    else:
        prev_g = jnp.where(first_of_seq, 0.0, carry_g[f])
        prev_u = jnp.where(first_of_seq, 0.0, carry_u[f])

    def up(r):
        hs = hs_ref[r:r + rb, :]
        return (jnp.dot(hs, wg_ref[...], preferred_element_type=F32),
                jnp.dot(hs, wu_ref[...], preferred_element_type=F32))

    def conv(prev, u, cw, cb):
        ext = jnp.concatenate([prev, u], axis=0)
        if shift % SUBLANES == 0:
            u1 = ext[halo - shift:halo - shift + rb, :]
            u2 = ext[halo - 2 * shift:halo - 2 * shift + rb, :]
        else:
            u1 = _shift_rows(ext, shift)[halo:, :]
            u2 = _shift_rows(ext, 2 * shift)[halo:, :]
        return cw[0:1, :] * u2 + cw[1:2, :] * u1 + cw[2:3, :] * u + cb

    nxt = up(0)
    for r in range(0, tm, rb):
        ug, uu = nxt
        if r + rb < tm:
            nxt = up(r + rb)
        cg = conv(prev_g, ug, cwg, cbg)
        cu = conv(prev_u, uu, cwu, cbu)
        prev_g, prev_u = ug[rb - halo:, :], uu[rb - halo:, :]
        act = ((cg / (1.0 + jnp.exp(-cg))) * cu).astype(BF16)
        o_ref[r:r + rb, :] += jnp.dot(act, wd_ref[...], preferred_element_type=F32)
    if from_state:
        nb = tail_ref.shape[0]
        for t in range(tail_ref.shape[1]):
            tail_ref[:, t, 0, :] = prev_g[t * nb:(t + 1) * nb, :]
            tail_ref[:, t, 1, :] = prev_u[t * nb:(t + 1) * nb, :]
    else:
        carry_g[f] = prev_g
        carry_u[f] = prev_u
        tg_ref[...] = prev_g
        tu_ref[...] = prev_u


def _conv_ffn(x, g, state, w_up, cw, cb, w_down, layer, *, shift, rows_per_seq):
    n, d = x.shape
    f2 = w_up.shape[2]
    ffn = f2 // 2
    tm = min(WIDE_TILE, n)
    fn = FFN_CHUNK
    nf = ffn // fn
    n_tiles = n // tm
    tiles_per_seq = max(rows_per_seq // tm, 1)
    halo = max((CONV_W - 1) * shift, SUBLANES)
    from_state = state is not None

    tile = pl.BlockSpec((tm, d), lambda i, f: (i, 0))
    tile_in = pl.BlockSpec((tm, d), lambda i, f: (i, 0), pipeline_mode=pl.Buffered(1))
    gate_cols = lambda rows: pl.BlockSpec((rows, fn), lambda i, f: (0, f))
    up_cols = lambda rows: pl.BlockSpec((rows, fn), lambda i, f: (0, f + nf))
    layer_gate = lambda rows: pl.BlockSpec((None, rows, fn), lambda i, f: (layer, 0, f))
    layer_up = lambda rows: pl.BlockSpec((None, rows, fn), lambda i, f: (layer, 0, f + nf))
    in_specs = [tile_in, pl.BlockSpec((1, d), lambda i, f: (0, 0))]
    args = [x, g]
    scratch = [pltpu.VMEM((tm, d), BF16)]
    if from_state:
        nb, steps = state.shape[1], state.shape[2]
        assert n == tm and nb == shift and halo == steps * shift
        in_specs += [pl.BlockSpec((None, nb, steps, fn), lambda i, f: (layer, 0, 0, f)),
                     pl.BlockSpec((None, nb, steps, fn), lambda i, f: (layer, 0, 0, f + nf))]
        args += [state, state]
        tail_specs = [pl.BlockSpec((nb, steps, 2, fn), lambda i, f: (0, 0, 0, f))]
        tail_shapes = [jax.ShapeDtypeStruct((nb, steps, 2, ffn), F32)]
    else:
        tail_specs = [pl.BlockSpec((None, halo, fn), lambda i, f: (i, 0, f))] * 2
        tail_shapes = [jax.ShapeDtypeStruct((n_tiles, halo, ffn), F32)] * 2
        scratch += [pltpu.VMEM((nf, halo, fn), F32), pltpu.VMEM((nf, halo, fn), F32)]
    in_specs += [layer_gate(d), layer_up(d), layer_gate(CONV_W), layer_up(CONV_W), gate_cols(1), up_cols(1),
                 pl.BlockSpec((None, fn, d), lambda i, f: (layer, f, 0))]
    args += [w_up, w_up, cw, cw, cb, cb, w_down]
    return pl.pallas_call(
        functools.partial(_ffn_kernel, shift=shift, tiles_per_seq=tiles_per_seq, from_state=from_state),
        grid=(n_tiles, nf),
        in_specs=in_specs,
        out_specs=[tile] + tail_specs,
        out_shape=[jax.ShapeDtypeStruct((n, d), F32)] + tail_shapes,
        scratch_shapes=scratch,
        compiler_params=_params(2),
        name="conv_ffn_state" if from_state else "conv_ffn",
    )(*args)


def _ple_kernel(*refs, final):
    if final:
        x_ref, p_ref, g_ref, wg_ref, wp_ref, gf_ref, o_ref = refs
    else:
        x_ref, p_ref, g_ref, wg_ref, wp_ref, o_ref = refs
    x = x_ref[...]
    hn = _rmsnorm(x, g_ref[...]).astype(BF16)
    gate = jax.nn.sigmoid(jnp.dot(hn, wg_ref[...], preferred_element_type=F32))
    emb = jnp.dot(p_ref[...].astype(BF16), wp_ref[...], preferred_element_type=F32)
    y = x + gate * emb
    if final:
        y = _rmsnorm(y, gf_ref[...])
    if len(o_ref.shape) == 3:
        _store_batch_major(o_ref, y)
    else:
        o_ref[...] = y


def _ple(x, p, g, w_gate, w_proj, layer, g_final, steps=None):
    n, d = x.shape
    tm = min(TOKEN_TILE, n)
    final = g_final is not None
    if steps is None:
        out_spec = pl.BlockSpec((tm, d), lambda i: (i, 0))
        out_shape = jax.ShapeDtypeStruct((n, d), F32)
    else:
        assert n == tm
        out_spec = pl.BlockSpec((n // steps, steps, d), lambda i: (0, 0, 0))
        out_shape = jax.ShapeDtypeStruct((n // steps, steps, d), F32)
    const = lambda a: pl.BlockSpec(a.shape, lambda i: (0, 0))
    resident = lambda a: pl.BlockSpec((None,) + a.shape[1:], lambda i: (layer, 0, 0),
                                      pipeline_mode=pl.Buffered(1))
    in_specs = [pl.BlockSpec((tm, d), lambda i: (i, 0)),
                pl.BlockSpec((None, tm, p.shape[2]), lambda i: (layer, i, 0)),
                const(g), resident(w_gate), resident(w_proj)]
    args = [x, p, g, w_gate, w_proj]
    if final:
        in_specs.append(const(g_final))
        args.append(g_final)
    return pl.pallas_call(
        functools.partial(_ple_kernel, final=final),
        grid=(n // tm,),
        in_specs=in_specs,
        out_specs=out_spec,
        out_shape=out_shape,
        compiler_params=_params(1),
        name="ple_final" if final else "ple",
    )(*args)


def _ret_in_kernel(x_ref, g_ref, w_ref, cos_ref, sin_ref, o_ref, hs_ref, *, dk, layout):
    @pl.when(pl.program_id(1) == 0)
    def _():
        hs_ref[...] = _rmsnorm(x_ref[...], g_ref[...]).astype(BF16)

    tm = hs_ref.shape[0]
    rb = min(ROW_BLOCK, tm)
    half = dk // 2
    project = lambda r: jnp.dot(hs_ref[r:r + rb, :], w_ref[...], preferred_element_type=F32)
    nxt = project(0)
    for r in range(0, tm, rb):
        proj = nxt
        if r + rb < tm:
            nxt = project(r + rb)
        cos = cos_ref[r:r + rb, :]
        sin = sin_ref[r:r + rb, :]
        for hh in range(proj.shape[1] // dk):
            x1 = proj[:, hh * dk:hh * dk + half]
            x2 = proj[:, hh * dk + half:(hh + 1) * dk]
            y1 = (x1 * cos - x2 * sin).astype(o_ref.dtype)
            y2 = (x1 * sin + x2 * cos).astype(o_ref.dtype)
            if layout == "heads":
                o_ref[hh, r:r + rb, 0:half] = y1
                o_ref[hh, r:r + rb, half:dk] = y2
            else:
                assert rb == tm
                _store_batch_major(o_ref, y1, slice(hh * dk, hh * dk + half))
                _store_batch_major(o_ref, y2, slice(hh * dk + half, (hh + 1) * dk))


def _ret_in(x, g, w_in, tables, *, qk_width, dk, rows_per_seq, out_dtype, steps=None):
    n, d = x.shape
    e = w_in.shape[1]
    tm = min(WIDE_TILE, n)
    nc = PROJ_CHUNK
    qc = qk_width // nc
    tps = rows_per_seq // tm

    def table_block(i, c):
        return (jnp.where(c < qc, 0, jnp.where(c < 2 * qc, tps, 2 * tps)) + i % tps, 0)

    if steps is None:
        out_spec = pl.BlockSpec((nc // dk, tm, dk), lambda i, c: (c, i, 0))
        out_shape = jax.ShapeDtypeStruct((e // dk, n, dk), out_dtype)
    else:
        assert n == tm
        out_spec = pl.BlockSpec((n // steps, steps, nc), lambda i, c: (0, 0, c))
        out_shape = jax.ShapeDtypeStruct((n // steps, steps, e), out_dtype)

    return pl.pallas_call(
        functools.partial(_ret_in_kernel, dk=dk, layout="heads" if steps is None else "batch"),
        grid=(n // tm, e // nc),
        in_specs=[pl.BlockSpec((tm, d), lambda i, c: (i, 0)),
                  pl.BlockSpec((1, d), lambda i, c: (0, 0)),
                  pl.BlockSpec((d, nc), lambda i, c: (0, c)),
                  pl.BlockSpec((tm, dk // 2), table_block),
                  pl.BlockSpec((tm, dk // 2), table_block)],
        out_specs=out_spec,
        out_shape=out_shape,
        scratch_shapes=[pltpu.VMEM((tm, d), BF16)],
        compiler_params=_params(2),
        name="ret_in",
    )(x, g, w_in, *tables)


def _groupnorm_gate(o, gate):
    mu = jnp.mean(o, axis=-1, keepdims=True)
    var = jnp.mean(jnp.square(o - mu), axis=-1, keepdims=True)
    on = (o - mu) * lax.rsqrt(var + EPS)
    gf = gate.astype(F32)
    return (gf / (1.0 + jnp.exp(-gf))) * on


def _ret_prompt_kernel(q_ref, k_ref, v_ref, g_ref, intra_ref, cross_ref, kdec_ref, sdec_ref,
                       o_ref, s_out_ref, s_ref, *, chunk):
    t = q_ref.shape[0]
    s_ref[...] = jnp.zeros(s_ref.shape, F32)
    intra = intra_ref[...]
    cross = cross_ref[...]
    kdec = kdec_ref[...]
    sdec = sdec_ref[...]

    wide = lambda ref, rows: jnp.concatenate([ref[i, rows, :] for i in range(ref.shape[0])], axis=-1)

    def body(c, carry):
        rows = pl.ds(pl.multiple_of(c * chunk, chunk), chunk)
        q = q_ref[rows, :]
        k = k_ref[rows, :]
        v = wide(v_ref, rows)
        s = s_ref[...]
        sc = lax.dot_general(q, k, (((1,), (1,)), ((), ())), preferred_element_type=F32) * intra
        o = (jnp.dot(sc.astype(BF16), v, preferred_element_type=F32)
             + jnp.dot(q, s.astype(BF16), preferred_element_type=F32) * cross)
        kd = (k.astype(F32) * kdec).astype(BF16)
        s_ref[...] = sdec * s + lax.dot_general(kd, v, (((0,), (0,)), ((), ())),
                                                preferred_element_type=F32)
        o_ref[rows, :] = _groupnorm_gate(o, wide(g_ref, rows)).astype(o_ref.dtype)
        return carry

    lax.fori_loop(0, t // chunk, body, 0, unroll=2)
    s_out_ref[...] = s_ref[...]


def _ret_prompt(proj, consts, *, batch, seq, heads, dk, dv):
    intra, cross, kdec, sdec = consts
    chunk = intra.shape[1]
    per = dv // dk
    v_off = 2 * heads // per
    g_off = v_off + heads
    head_const = lambda a: pl.BlockSpec((None,) + a.shape[1:], lambda b, h: (h, 0, 0))
    return pl.pallas_call(
        functools.partial(_ret_prompt_kernel, chunk=chunk),
        grid=(batch, heads),
        in_specs=[pl.BlockSpec((None, seq, dk), lambda b, h: (h, b, 0)),
                  pl.BlockSpec((None, seq, dk), lambda b, h: (heads + h, b, 0)),
                  pl.BlockSpec((per, seq, dk), lambda b, h: (v_off + h, b, 0)),
                  pl.BlockSpec((per, seq, dk), lambda b, h: (g_off + h, b, 0)),
                  head_const(intra), head_const(cross), head_const(kdec), head_const(sdec)],
        out_specs=[pl.BlockSpec((None, seq, dv), lambda b, h: (h, b, 0)),
                   pl.BlockSpec((None, None, dk, dv), lambda b, h: (b, h, 0, 0))],
        out_shape=[jax.ShapeDtypeStruct((heads, batch * seq, dv), BF16),
                   jax.ShapeDtypeStruct((batch, heads, dk, dv), F32)],
        scratch_shapes=[pltpu.VMEM((dk, dv), F32)],
        compiler_params=_params(2),
        name="retention_prompt",
    )(proj, proj, proj, proj, intra, cross, kdec, sdec)


def _ret_sample_kernel(p_ref, s_ref, intra_ref, cross_ref, kdec_ref, sdec_ref, o_ref, s_out_ref,
                       *, heads, dk, dv):
    qk = heads * dk
    for n in range(p_ref.shape[0]):
        for h in range(heads):
            q = p_ref[n, :, h * dk:(h + 1) * dk].astype(BF16)
            kf = p_ref[n, :, qk + h * dk:qk + (h + 1) * dk]
            v = p_ref[n, :, 2 * qk + h * dv:2 * qk + (h + 1) * dv].astype(BF16)
            gate = p_ref[n, :, 2 * qk + heads * dv + h * dv:2 * qk + heads * dv + (h + 1) * dv]
            s = s_ref[n, h]
            sc = lax.dot_general(q, kf.astype(BF16), (((1,), (1,)), ((), ())),
                                 preferred_element_type=F32) * intra_ref[h]
            o = (jnp.dot(sc.astype(BF16), v, preferred_element_type=F32)
                 + jnp.dot(q, s.astype(BF16), preferred_element_type=F32) * cross_ref[h])
            kd = (kf * kdec_ref[h]).astype(BF16)
            s_out_ref[n, h] = sdec_ref[h] * s + lax.dot_general(kd, v, (((0,), (0,)), ((), ())),
                                                                preferred_element_type=F32)
            o_ref[n, :, h * dv:(h + 1) * dv] = _groupnorm_gate(o, gate).astype(o_ref.dtype)


def _ret_sample(proj, state, consts, *, heads, dk, dv):
    intra, cross, kdec, sdec = consts
    b, tp, e = proj.shape
    nb = SAMPLE_SEQS
    const = lambda a: pl.BlockSpec(a.shape, lambda i: (0,) * a.ndim)
    return pl.pallas_call(
        functools.partial(_ret_sample_kernel, heads=heads, dk=dk, dv=dv),
        grid=(b // nb,),
        in_specs=[pl.BlockSpec((nb, tp, e), lambda i: (i, 0, 0)),
                  pl.BlockSpec((nb, heads, dk, dv), lambda i: (i, 0, 0, 0)),
                  const(intra), const(cross), const(kdec), const(sdec)],
        out_specs=[pl.BlockSpec((nb, tp, heads * dv), lambda i: (i, 0, 0)),
                   pl.BlockSpec((nb, heads, dk, dv), lambda i: (i, 0, 0, 0))],
        out_shape=[jax.ShapeDtypeStruct((b, tp, heads * dv), F32),
                   jax.ShapeDtypeStruct((b, heads, dk, dv), F32)],
        compiler_params=_params(1),
        name="retention_sample",
    )(proj, state, intra, cross, kdec, sdec)


def _ret_out_kernel(x_ref, a_ref, w_ref, o_ref, *, head_major):
    if head_major:
        dv = a_ref.shape[2]
        y = x_ref[...]
        for h in range(a_ref.shape[0]):
            y = y + jnp.dot(a_ref[h], w_ref[h * dv:(h + 1) * dv, :], preferred_element_type=F32)
        o_ref[...] = y
    else:
        a = _time_major(a_ref).astype(BF16)
        o_ref[...] = x_ref[...] + jnp.dot(a, w_ref[...], preferred_element_type=F32)


def _ret_out(x, a, w_out, *, head_major):
    n, d = x.shape
    tm = min(TOKEN_TILE, n)
    if head_major:
        a_spec = pl.BlockSpec((a.shape[0], tm, a.shape[2]), lambda i: (0, i, 0))
    else:
        assert n == tm
        a_spec = pl.BlockSpec(a.shape, lambda i: (0, 0, 0))
    v = w_out.shape[0]
    return pl.pallas_call(
        functools.partial(_ret_out_kernel, head_major=head_major),
        grid=(n // tm,),
        in_specs=[pl.BlockSpec((tm, d), lambda i: (i, 0)),
                  a_spec,
                  pl.BlockSpec((v, d), lambda i: (0, 0), pipeline_mode=pl.Buffered(1))],
        out_specs=pl.BlockSpec((tm, d), lambda i: (i, 0)),
        out_shape=jax.ShapeDtypeStruct((n, d), F32),
        compiler_params=_params(1),
        name="ret_out",
    )(x, a, w_out)


def _decay_consts(chunk, heads, dk, dv):
    log_g = np.log1p(-(2.0 ** (-5.0 - np.arange(heads, dtype=np.float64))))
    n = np.arange(chunk, dtype=np.float64)
    diff = n[:, None] - n[None, :]
    intra = np.where(diff >= 0, np.exp(log_g[:, None, None] * np.maximum(diff, 0.0)), 0.0)
    cross = np.exp(log_g[:, None] * (n + 1.0))
    kdec = np.exp(log_g[:, None] * (chunk - 1.0 - n))
    sdec = np.exp(log_g * chunk)
    f32 = lambda a: jnp.asarray(np.ascontiguousarray(a), dtype=F32)
    return (f32(intra),
            f32(np.broadcast_to(cross[:, :, None], (heads, chunk, dv))),
            f32(np.broadcast_to(kdec[:, :, None], (heads, chunk, dk))),
            f32(np.broadcast_to(sdec[:, None, None], (heads, 1, dv))))


def _rotary_tables(pos, dk):
    half = dk // 2
    inv = ROPE_BASE ** (-np.arange(half, dtype=np.float64) / half)
    ang = np.asarray(pos, dtype=np.float64)[:, None] * inv[None, :]
    cos, sin = np.cos(ang), np.sin(ang)
    scale = dk ** -0.5
    f32 = lambda a: jnp.asarray(a, dtype=F32)
    return (f32(np.concatenate([cos, cos * scale, np.ones_like(cos)])),
            f32(np.concatenate([sin, sin * scale, np.zeros_like(sin)])))


def kernel(x_prompt, x_sample, p_prompt, p_sample, state_pool, state_ret, state_conv, norm_mix, norm_ffn,
           norm_ple, norm_final, pool_w, pool_scale, ret_w_in, ret_w_out, ffn_w_up, ffn_conv_w,
           ffn_conv_b, ffn_w_down, ple_w_proj, ple_w_gate):
    b, t, d = x_prompt.shape
    bs, ts, _ = x_sample.shape
    depth = norm_mix.shape[0]
    heads = RET_HEADS
    dk = d // heads
    dv = 2 * dk
    qk = heads * dk
    f2 = ffn_w_up.shape[2]
    ffn = f2 // 2
    assert ts >= CONV_W - 1 and PAST_LEN >= POOL_BUF

    row2 = lambda a: a.reshape(1, -1)
    t_major = lambda a: jnp.swapaxes(a, 0, 1).reshape(-1, a.shape[-1])
    b_major = lambda a: jnp.swapaxes(a.reshape(-1, bs, a.shape[-1]), 0, 1)

    xp = x_prompt.reshape(b * t, d)
    xs = t_major(x_sample)
    pp = p_prompt.reshape(depth, b * t, -1)
    ps = jnp.swapaxes(p_sample, 1, 2).reshape(depth, ts * bs, -1)

    tables_p = _rotary_tables(np.arange(t), dk)
    tables_s = _rotary_tables(PAST_LEN + np.repeat(np.arange(ts), bs), dk)
    chunk_p = RET_CHUNK if t % RET_CHUNK == 0 else t
    consts_p = _decay_consts(chunk_p, heads, dk, dv)
    consts_s = _decay_consts(ts, heads, dk, dv)

    new_pool_p, new_pool_s, new_ret_p, new_ret_s, new_conv_p, new_conv_s = [], [], [], [], [], []
    for i in range(depth):
        jm = i // 2
        if i % 2 == 0:
            w = pool_w[jm]
            g = row2(norm_mix[i])
            sc = row2(pool_scale[jm])
            xp, hl = _pool_mixer(xp, None, g, w, sc, shift=1, pos0=0, rows_per_seq=t)
            new_pool_p.append(hl[:, -POOL_BUF:, :])
            xs3 = x_sample if i == 0 else b_major(xs)
            xs, ns = _pool_mixer(xs3, state_pool[jm], g, w, sc, shift=bs, pos0=PAST_LEN, rows_per_seq=ts * bs)
            new_pool_s.append(ns)
        else:
            w_in = ret_w_in[jm]
            w_out = ret_w_out[jm].astype(BF16)
            g = row2(norm_mix[i])
            proj_p = _ret_in(xp, g, w_in, tables_p, qk_width=qk, dk=dk, rows_per_seq=t, out_dtype=BF16)
            a_p, s_p = _ret_prompt(proj_p, consts_p, batch=b, seq=t, heads=heads, dk=dk, dv=dv)
            xp = _ret_out(xp, a_p, w_out, head_major=True)
            new_ret_p.append(s_p)
            proj_s = _ret_in(xs, g, w_in, tables_s, qk_width=qk, dk=dk, rows_per_seq=ts * bs, out_dtype=F32,
                             steps=ts)
            a_s, s_s = _ret_sample(proj_s, state_ret[jm], consts_s, heads=heads, dk=dk, dv=dv)
            xs = _ret_out(xs, a_s, w_out, head_major=False)
            new_ret_s.append(s_s)

        g = row2(norm_ffn[i])
        cb = row2(ffn_conv_b[i])
        xp, tg, tu = _conv_ffn(xp, g, None, ffn_w_up, ffn_conv_w, cb, ffn_w_down, i, shift=1, rows_per_seq=t)
        tiles_per_seq = tg.shape[0] // b
        last = jnp.concatenate([tg, tu], axis=-1)[tiles_per_seq - 1::tiles_per_seq]
        new_conv_p.append(last[:, -(CONV_W - 1):, :])
        xs, tail = _conv_ffn(xs, g, state_conv, ffn_w_up, ffn_conv_w, cb, ffn_w_down, i, shift=bs,
                             rows_per_seq=ts * bs)
        new_conv_s.append(tail.reshape(bs, CONV_W - 1, f2))

        g = row2(norm_ple[i])
        last_layer = i == depth - 1
        g_final = row2(norm_final) if last_layer else None
        xp = _ple(xp, pp, g, ple_w_gate, ple_w_proj, i, g_final)
        xs = _ple(xs, ps, g, ple_w_gate, ple_w_proj, i, g_final, steps=ts if last_layer else None)

    return (xp.reshape(b, t, d), xs,
            jnp.stack(new_pool_p), jnp.stack(new_pool_s),
            jnp.stack(new_ret_p), jnp.stack(new_ret_s),
            jnp.stack(new_conv_p), jnp.stack(new_conv_s))
```

```python
import functools

import jax
import jax.numpy as jnp
import numpy as np
from jax import lax
from jax.experimental import pallas as pl
from jax.experimental.pallas import tpu as pltpu

F32 = jnp.float32
BF16 = jnp.bfloat16

EPS = 1e-6
POOL_WINDOWS = (2, 4, 8, 16)
POOL_BUF = max(POOL_WINDOWS) - 1
RET_HEADS = 8
RET_CHUNK = 256
ROPE_BASE = 10000.0
CONV_W = 3
PAST_LEN = 16384

TOKEN_TILE = 512
WIDE_TILE = 1024
FFN_CHUNK = 512
ROW_BLOCK = 512
PROJ_CHUNK = 1024
SAMPLE_SEQS = 2
SUBLANES = 8
VMEM_LIMIT = 60 * 1024 * 1024


def _params(n_axes):
    return pltpu.CompilerParams(
        dimension_semantics=("arbitrary",) * n_axes, vmem_limit_bytes=VMEM_LIMIT)


def _rmsnorm(x, g):
    return x * lax.rsqrt(jnp.mean(x * x, axis=-1, keepdims=True) + EPS) * g


def _shift_rows(a, rows):
    return pltpu.roll(a, rows, axis=0)


def _time_major(ref, cols=slice(None)):
    return jnp.concatenate([ref[:, t, cols] for t in range(ref.shape[1])], axis=0)


def _store_batch_major(ref, val, cols=slice(None)):
    nb = ref.shape[0]
    for t in range(ref.shape[1]):
        ref[:, t, cols] = val[t * nb:(t + 1) * nb, :]


def _pool_group(h, prev, x, w, sc, pos, win, shift):
    halo = prev.shape[0]
    a = jnp.concatenate([prev, h], axis=0)
    span = 1
    while span < win:
        a = a + _shift_rows(a, span * shift)
        span *= 2
    cnt = jnp.minimum(pos + 1, win).astype(F32)
    dlt = a[halo:, :] / cnt - h
    return x + jnp.dot(dlt.astype(BF16), w, preferred_element_type=F32) * sc


def _pool_prompt_kernel(x_ref, g_ref, w_ref, sc_ref, o_ref, h_ref, carry_ref, *, tiles_per_seq):
    tm, d = x_ref.shape
    gw = d // len(POOL_WINDOWS)
    halo = carry_ref.shape[0]
    j = pl.program_id(0) % tiles_per_seq
    x = x_ref[...]
    h = _rmsnorm(x, g_ref[...])
    last = h[tm - halo:, :]
    h_ref[...] = last
    pos = j * tm + lax.broadcasted_iota(jnp.int32, (tm, 1), 0)
    for gi, win in enumerate(POOL_WINDOWS):
        cols = slice(gi * gw, (gi + 1) * gw)
        prev = jnp.where(j == 0, 0.0, carry_ref[:, cols])
        o_ref[:, cols] = _pool_group(h[:, cols], prev, x[:, cols], w_ref[gi], sc_ref[:, cols], pos, win, 1)
    carry_ref[...] = last


def _pool_state_kernel(x_ref, state_ref, g_ref, w_ref, sc_ref, o_ref, ns_ref, hs_ref, *, pos0):
    shift, steps, d = x_ref.shape
    tm = shift * steps
    gw = o_ref.shape[1]
    kept = ns_ref.shape[1]
    grp = pl.program_id(0)

    @pl.when(grp == 0)
    def _():
        hs_ref[...] = _rmsnorm(_time_major(x_ref), g_ref[...])

    row = lax.broadcasted_iota(jnp.int32, (tm, 1), 0)
    step = jnp.zeros_like(row)
    for k in range(1, steps):
        step = step + (row >= k * shift).astype(jnp.int32)
    pos = pos0 + step

    for gi, win in enumerate(POOL_WINDOWS):
        @pl.when(grp == gi)
        def _(gi=gi, win=win):
            cols = slice(gi * gw, (gi + 1) * gw)
            h = hs_ref[:, cols]
            prev = jnp.concatenate([jnp.zeros((shift, gw), F32), state_ref[...]], axis=0)
            o_ref[...] = _pool_group(h, prev, _time_major(x_ref, cols), w_ref[...], sc_ref[...],
                                     pos, win, shift)
            ext = jnp.concatenate([prev, h], axis=0)
            _store_batch_major(ns_ref, ext[ext.shape[0] - kept * shift:, :])


def _pool_mixer(x, halo, g, w, scale, *, shift, pos0, rows_per_seq):
    d = x.shape[-1]
    ng = len(POOL_WINDOWS)
    gw = d // ng
    if halo is None:
        n = x.shape[0]
        tm = min(TOKEN_TILE, n)
        assert shift == 1 and pos0 == 0
        tiles_per_seq = rows_per_seq // tm
        halo_rows = POOL_BUF + 1
        full = lambda a: pl.BlockSpec(a.shape, lambda i: (0,) * a.ndim)
        tile = pl.BlockSpec((tm, d), lambda i: (i, 0))
        return pl.pallas_call(
            functools.partial(_pool_prompt_kernel, tiles_per_seq=tiles_per_seq),
            grid=(n // tm,),
            in_specs=[tile, full(g), full(w), full(scale)],
            out_specs=[tile, pl.BlockSpec((None, halo_rows, d), lambda i: (i // tiles_per_seq, 0, 0))],
            out_shape=[jax.ShapeDtypeStruct((n, d), F32),
                       jax.ShapeDtypeStruct((n // rows_per_seq, halo_rows, d), F32)],
            scratch_shapes=[pltpu.VMEM((halo_rows, d), F32)],
            compiler_params=_params(1),
            name="pool_mixer",
        )(x, g, w, scale)
    nb, steps, _ = x.shape
    kept = halo.shape[0] // nb
    assert shift == nb and kept == POOL_BUF
    return pl.pallas_call(
        functools.partial(_pool_state_kernel, pos0=pos0),
        grid=(ng,),
        in_specs=[pl.BlockSpec((nb, steps, d), lambda gi: (0, 0, 0)),
                  pl.BlockSpec((kept * nb, gw), lambda gi: (0, gi)),
                  pl.BlockSpec((1, d), lambda gi: (0, 0)),
                  pl.BlockSpec((None, gw, gw), lambda gi: (gi, 0, 0)),
                  pl.BlockSpec((1, gw), lambda gi: (0, gi))],
        out_specs=[pl.BlockSpec((nb * steps, gw), lambda gi: (0, gi)),
                   pl.BlockSpec((nb, kept, gw), lambda gi: (0, 0, gi))],
        out_shape=[jax.ShapeDtypeStruct((nb * steps, d), F32), jax.ShapeDtypeStruct((nb, kept, d), F32)],
        scratch_shapes=[pltpu.VMEM((nb * steps, d), F32)],
        compiler_params=_params(1),
        name="pool_mixer_state",
    )(x, halo, g, w, scale)


def _ffn_kernel(*refs, shift, tiles_per_seq, from_state):
    if from_state:
        (x_ref, g_ref, sg_ref, su_ref, wg_ref, wu_ref, cwg_ref, cwu_ref, cbg_ref, cbu_ref, wd_ref,
         o_ref, tg_ref, tu_ref, hs_ref) = refs
        carry_g = carry_u = None
    else:
        (x_ref, g_ref, wg_ref, wu_ref, cwg_ref, cwu_ref, cbg_ref, cbu_ref, wd_ref,
         o_ref, tg_ref, tu_ref, hs_ref, carry_g, carry_u) = refs
        sg_ref = su_ref = None
    tm, d = x_ref.shape
    halo = max((CONV_W - 1) * shift, SUBLANES)
    i = pl.program_id(0)
    f = pl.program_id(1)
    first_of_seq = (i % tiles_per_seq) == 0

    @pl.when(f == 0)
    def _():
        x = x_ref[...]
        hs_ref[...] = _rmsnorm(x, g_ref[...]).astype(BF16)
        o_ref[...] = x

    rb = min(ROW_BLOCK, tm)
    cwg, cwu, cbg, cbu = cwg_ref[...], cwu_ref[...], cbg_ref[...], cbu_ref[...]
    if from_state:
        assert rb == tm
        prev_g, prev_u = _time_major(sg_ref), _time_major(su_ref)
    else:
        prev_g = jnp.where(first_of_seq, 0.0, carry_g[f])
        prev_u = jnp.where(first_of_seq, 0.0, carry_u[f])

    def up(r):
        hs = hs_ref[r:r + rb, :]
        return (jnp.dot(hs, wg_ref[...], preferred_element_type=F32),
                jnp.dot(hs, wu_ref[...], preferred_element_type=F32))

    def conv(prev, u, cw, cb):
        ext = jnp.concatenate([prev, u], axis=0)
        if shift % SUBLANES == 0:
            u1 = ext[halo - shift:halo - shift + rb, :]
            u2 = ext[halo - 2 * shift:halo - 2 * shift + rb, :]
        else:
            u1 = _shift_rows(ext, shift)[halo:, :]
            u2 = _shift_rows(ext, 2 * shift)[halo:, :]
        return cw[0:1, :] * u2 + cw[1:2, :] * u1 + cw[2:3, :] * u + cb

    nxt = up(0)
    for r in range(0, tm, rb):
        ug, uu = nxt
        if r + rb < tm:
            nxt = up(r + rb)
        cg = conv(prev_g, ug, cwg, cbg)
        cu = conv(prev_u, uu, cwu, cbu)
        prev_g, prev_u = ug[rb - halo:, :], uu[rb - halo:, :]
        act = ((cg / (1.0 + jnp.exp(-cg))) * cu).astype(BF16)
        o_ref[r:r + rb, :] += jnp.dot(act, wd_ref[...], preferred_element_type=F32)
    if from_state:
        _store_batch_major(tg_ref, prev_g)
        _store_batch_major(tu_ref, prev_u)
    else:
        carry_g[f] = prev_g
        carry_u[f] = prev_u
        tg_ref[...] = prev_g
        tu_ref[...] = prev_u


def _conv_ffn(x, g, state, w_up, cw, cb, w_down, layer, *, shift, rows_per_seq):
    n, d = x.shape
    f2 = w_up.shape[2]
    ffn = f2 // 2
    tm = min(WIDE_TILE, n)
    fn = FFN_CHUNK
    nf = ffn // fn
    n_tiles = n // tm
    tiles_per_seq = max(rows_per_seq // tm, 1)
    halo = max((CONV_W - 1) * shift, SUBLANES)
    from_state = state is not None

    tile = pl.BlockSpec((tm, d), lambda i, f: (i, 0))
    tile_in = pl.BlockSpec((tm, d), lambda i, f: (i, 0), pipeline_mode=pl.Buffered(1))
    gate_cols = lambda rows: pl.BlockSpec((rows, fn), lambda i, f: (0, f))
    up_cols = lambda rows: pl.BlockSpec((rows, fn), lambda i, f: (0, f + nf))
    layer_gate = lambda rows: pl.BlockSpec((None, rows, fn), lambda i, f: (layer, 0, f))
    layer_up = lambda rows: pl.BlockSpec((None, rows, fn), lambda i, f: (layer, 0, f + nf))
    in_specs = [tile_in, pl.BlockSpec((1, d), lambda i, f: (0, 0))]
    args = [x, g]
    scratch = [pltpu.VMEM((tm, d), BF16)]
    if from_state:
        nb, steps = state.shape[1], state.shape[2]
        assert n == tm and nb == shift and halo == steps * shift
        in_specs += [pl.BlockSpec((None, nb, steps, fn), lambda i, f: (layer, 0, 0, f)),
                     pl.BlockSpec((None, nb, steps, fn), lambda i, f: (layer, 0, 0, f + nf))]
        args += [state, state]
        tail_specs = [pl.BlockSpec((nb, steps, fn), lambda i, f: (0, 0, f))] * 2
        tail_shapes = [jax.ShapeDtypeStruct((nb, steps, ffn), F32)] * 2
    else:
        tail_specs = [pl.BlockSpec((None, halo, fn), lambda i, f: (i, 0, f))] * 2
        tail_shapes = [jax.ShapeDtypeStruct((n_tiles, halo, ffn), F32)] * 2
        scratch += [pltpu.VMEM((nf, halo, fn), F32), pltpu.VMEM((nf, halo, fn), F32)]
    in_specs += [layer_gate(d), layer_up(d), layer_gate(CONV_W), layer_up(CONV_W), gate_cols(1), up_cols(1),
                 pl.BlockSpec((None, fn, d), lambda i, f: (layer, f, 0))]
    args += [w_up, w_up, cw, cw, cb, cb, w_down]
    return pl.pallas_call(
        functools.partial(_ffn_kernel, shift=shift, tiles_per_seq=tiles_per_seq, from_state=from_state),
        grid=(n_tiles, nf),
        in_specs=in_specs,
        out_specs=[tile] + tail_specs,
        out_shape=[jax.ShapeDtypeStruct((n, d), F32)] + tail_shapes,
        scratch_shapes=scratch,
        compiler_params=_params(2),
        name="conv_ffn_state" if from_state else "conv_ffn",
    )(*args)


def _ple_kernel(*refs, final):
    if final:
        x_ref, p_ref, g_ref, wg_ref, wp_ref, gf_ref, o_ref = refs
    else:
        x_ref, p_ref, g_ref, wg_ref, wp_ref, o_ref = refs
    x = x_ref[...]
    hn = _rmsnorm(x, g_ref[...]).astype(BF16)
    gate = jax.nn.sigmoid(jnp.dot(hn, wg_ref[...], preferred_element_type=F32))
    emb = jnp.dot(p_ref[...].astype(BF16), wp_ref[...], preferred_element_type=F32)
    y = x + gate * emb
    if final:
        y = _rmsnorm(y, gf_ref[...])
    if len(o_ref.shape) == 3:
        _store_batch_major(o_ref, y)
    else:
        o_ref[...] = y


def _ple(x, p, g, w_gate, w_proj, layer, g_final, steps=None):
    n, d = x.shape
    tm = min(TOKEN_TILE, n)
    final = g_final is not None
    if steps is None:
        out_spec = pl.BlockSpec((tm, d), lambda i: (i, 0))
        out_shape = jax.ShapeDtypeStruct((n, d), F32)
    else:
        assert n == tm
        out_spec = pl.BlockSpec((n // steps, steps, d), lambda i: (0, 0, 0))
        out_shape = jax.ShapeDtypeStruct((n // steps, steps, d), F32)
    const = lambda a: pl.BlockSpec(a.shape, lambda i: (0, 0))
    resident = lambda a: pl.BlockSpec((None,) + a.shape[1:], lambda i: (layer, 0, 0),
                                      pipeline_mode=pl.Buffered(1))
    in_specs = [pl.BlockSpec((tm, d), lambda i: (i, 0)),
                pl.BlockSpec((None, tm, p.shape[2]), lambda i: (layer, i, 0)),
                const(g), resident(w_gate), resident(w_proj)]
    args = [x, p, g, w_gate, w_proj]
    if final:
        in_specs.append(const(g_final))
        args.append(g_final)
    return pl.pallas_call(
        functools.partial(_ple_kernel, final=final),
        grid=(n // tm,),
        in_specs=in_specs,
        out_specs=out_spec,
        out_shape=out_shape,
        compiler_params=_params(1),
        name="ple_final" if final else "ple",
    )(*args)


def _ret_in_kernel(x_ref, g_ref, w_ref, cos_ref, sin_ref, o_ref, hs_ref, *, dk, layout):
    @pl.when(pl.program_id(1) == 0)
    def _():
        hs_ref[...] = _rmsnorm(x_ref[...], g_ref[...]).astype(BF16)

    tm = hs_ref.shape[0]
    rb = min(ROW_BLOCK, tm)
    half = dk // 2
    project = lambda r: jnp.dot(hs_ref[r:r + rb, :], w_ref[...], preferred_element_type=F32)
    nxt = project(0)
    for r in range(0, tm, rb):
        proj = nxt
        if r + rb < tm:
            nxt = project(r + rb)
        cos = cos_ref[r:r + rb, :]
        sin = sin_ref[r:r + rb, :]
        for hh in range(proj.shape[1] // dk):
            x1 = proj[:, hh * dk:hh * dk + half]
            x2 = proj[:, hh * dk + half:(hh + 1) * dk]
            y1 = (x1 * cos - x2 * sin).astype(o_ref.dtype)
            y2 = (x1 * sin + x2 * cos).astype(o_ref.dtype)
            if layout == "heads":
                o_ref[hh, r:r + rb, 0:half] = y1
                o_ref[hh, r:r + rb, half:dk] = y2
            else:
                assert rb == tm
                _store_batch_major(o_ref, y1, slice(hh * dk, hh * dk + half))
                _store_batch_major(o_ref, y2, slice(hh * dk + half, (hh + 1) * dk))


def _ret_in(x, g, w_in, tables, *, qk_width, dk, rows_per_seq, out_dtype, steps=None):
    n, d = x.shape
    e = w_in.shape[1]
    tm = min(WIDE_TILE, n)
    nc = PROJ_CHUNK
    qc = qk_width // nc
    tps = rows_per_seq // tm

    def table_block(i, c):
        return (jnp.where(c < qc, 0, jnp.where(c < 2 * qc, tps, 2 * tps)) + i % tps, 0)

    if steps is None:
        out_spec = pl.BlockSpec((nc // dk, tm, dk), lambda i, c: (c, i, 0))
        out_shape = jax.ShapeDtypeStruct((e // dk, n, dk), out_dtype)
    else:
        assert n == tm
        out_spec = pl.BlockSpec((n // steps, steps, nc), lambda i, c: (0, 0, c))
        out_shape = jax.ShapeDtypeStruct((n // steps, steps, e), out_dtype)

    return pl.pallas_call(
        functools.partial(_ret_in_kernel, dk=dk, layout="heads" if steps is None else "batch"),
        grid=(n // tm, e // nc),
        in_specs=[pl.BlockSpec((tm, d), lambda i, c: (i, 0)),
                  pl.BlockSpec((1, d), lambda i, c: (0, 0)),
                  pl.BlockSpec((d, nc), lambda i, c: (0, c)),
                  pl.BlockSpec((tm, dk // 2), table_block),
                  pl.BlockSpec((tm, dk // 2), table_block)],
        out_specs=out_spec,
        out_shape=out_shape,
        scratch_shapes=[pltpu.VMEM((tm, d), BF16)],
        compiler_params=_params(2),
        name="ret_in",
    )(x, g, w_in, *tables)


def _groupnorm_gate(o, gate):
    mu = jnp.mean(o, axis=-1, keepdims=True)
    var = jnp.mean(jnp.square(o - mu), axis=-1, keepdims=True)
    on = (o - mu) * lax.rsqrt(var + EPS)
    gf = gate.astype(F32)
    return (gf / (1.0 + jnp.exp(-gf))) * on


def _ret_prompt_kernel(q_ref, k_ref, v_ref, g_ref, intra_ref, cross_ref, kdec_ref, sdec_ref,
                       o_ref, s_out_ref, s_ref, *, chunk):
    t = q_ref.shape[0]
    s_ref[...] = jnp.zeros(s_ref.shape, F32)
    intra = intra_ref[...]
    cross = cross_ref[...]
    kdec = kdec_ref[...]
    sdec = sdec_ref[...]

    wide = lambda ref, rows: jnp.concatenate([ref[i, rows, :] for i in range(ref.shape[0])], axis=-1)

    def body(c, carry):
        rows = pl.ds(pl.multiple_of(c * chunk, chunk), chunk)
        q = q_ref[rows, :]
        k = k_ref[rows, :]
        v = wide(v_ref, rows)
        s = s_ref[...]
        sc = lax.dot_general(q, k, (((1,), (1,)), ((), ())), preferred_element_type=F32) * intra
        o = (jnp.dot(sc.astype(BF16), v, preferred_element_type=F32)
             + jnp.dot(q, s.astype(BF16), preferred_element_type=F32) * cross)
        kd = (k.astype(F32) * kdec).astype(BF16)
        s_ref[...] = sdec * s + lax.dot_general(kd, v, (((0,), (0,)), ((), ())),
                                                preferred_element_type=F32)
        o_ref[rows, :] = _groupnorm_gate(o, wide(g_ref, rows)).astype(o_ref.dtype)
        return carry

    lax.fori_loop(0, t // chunk, body, 0, unroll=2)
    s_out_ref[...] = s_ref[...]


def _ret_prompt(proj, consts, *, batch, seq, heads, dk, dv):
    intra, cross, kdec, sdec = consts
    chunk = intra.shape[1]
    per = dv // dk
    v_off = 2 * heads // per
    g_off = v_off + heads
    head_const = lambda a: pl.BlockSpec((None,) + a.shape[1:], lambda b, h: (h, 0, 0))
    return pl.pallas_call(
        functools.partial(_ret_prompt_kernel, chunk=chunk),
        grid=(batch, heads),
        in_specs=[pl.BlockSpec((None, seq, dk), lambda b, h: (h, b, 0)),
                  pl.BlockSpec((None, seq, dk), lambda b, h: (heads + h, b, 0)),
                  pl.BlockSpec((per, seq, dk), lambda b, h: (v_off + h, b, 0)),
                  pl.BlockSpec((per, seq, dk), lambda b, h: (g_off + h, b, 0)),
                  head_const(intra), head_const(cross), head_const(kdec), head_const(sdec)],
        out_specs=[pl.BlockSpec((None, seq, dv), lambda b, h: (h, b, 0)),
                   pl.BlockSpec((None, None, dk, dv), lambda b, h: (b, h, 0, 0))],
        out_shape=[jax.ShapeDtypeStruct((heads, batch * seq, dv), BF16),
                   jax.ShapeDtypeStruct((batch, heads, dk, dv), F32)],
        scratch_shapes=[pltpu.VMEM((dk, dv), F32)],
        compiler_params=_params(2),
        name="retention_prompt",
    )(proj, proj, proj, proj, intra, cross, kdec, sdec)


def _ret_sample_kernel(p_ref, s_ref, intra_ref, cross_ref, kdec_ref, sdec_ref, o_ref, s_out_ref,
                       *, heads, dk, dv):
    qk = heads * dk
    for n in range(p_ref.shape[0]):
        for h in range(heads):
            q = p_ref[n, :, h * dk:(h + 1) * dk].astype(BF16)
            kf = p_ref[n, :, qk + h * dk:qk + (h + 1) * dk]
            v = p_ref[n, :, 2 * qk + h * dv:2 * qk + (h + 1) * dv].astype(BF16)
            gate = p_ref[n, :, 2 * qk + heads * dv + h * dv:2 * qk + heads * dv + (h + 1) * dv]
            s = s_ref[n, h]
            sc = lax.dot_general(q, kf.astype(BF16), (((1,), (1,)), ((), ())),
                                 preferred_element_type=F32) * intra_ref[h]
            o = (jnp.dot(sc.astype(BF16), v, preferred_element_type=F32)
                 + jnp.dot(q, s.astype(BF16), preferred_element_type=F32) * cross_ref[h])
            kd = (kf * kdec_ref[h]).astype(BF16)
            s_out_ref[n, h] = sdec_ref[h] * s + lax.dot_general(kd, v, (((0,), (0,)), ((), ())),
                                                                preferred_element_type=F32)
            o_ref[n, :, h * dv:(h + 1) * dv] = _groupnorm_gate(o, gate).astype(o_ref.dtype)


def _ret_sample(proj, state, consts, *, heads, dk, dv):
    intra, cross, kdec, sdec = consts
    b, tp, e = proj.shape
    nb = SAMPLE_SEQS
    const = lambda a: pl.BlockSpec(a.shape, lambda i: (0,) * a.ndim)
    return pl.pallas_call(
        functools.partial(_ret_sample_kernel, heads=heads, dk=dk, dv=dv),
        grid=(b // nb,),
        in_specs=[pl.BlockSpec((nb, tp, e), lambda i: (i, 0, 0)),
                  pl.BlockSpec((nb, heads, dk, dv), lambda i: (i, 0, 0, 0)),
                  const(intra), const(cross), const(kdec), const(sdec)],
        out_specs=[pl.BlockSpec((nb, tp, heads * dv), lambda i: (i, 0, 0)),
                   pl.BlockSpec((nb, heads, dk, dv), lambda i: (i, 0, 0, 0))],
        out_shape=[jax.ShapeDtypeStruct((b, tp, heads * dv), F32),
                   jax.ShapeDtypeStruct((b, heads, dk, dv), F32)],
        compiler_params=_params(1),
        name="retention_sample",
    )(proj, state, intra, cross, kdec, sdec)


def _ret_out_kernel(x_ref, a_ref, w_ref, o_ref, *, head_major):
    if head_major:
        dv = a_ref.shape[2]
        y = x_ref[...]
        for h in range(a_ref.shape[0]):
            y = y + jnp.dot(a_ref[h], w_ref[h * dv:(h + 1) * dv, :], preferred_element_type=F32)
        o_ref[...] = y
    else:
        a = _time_major(a_ref).astype(BF16)
        o_ref[...] = x_ref[...] + jnp.dot(a, w_ref[...], preferred_element_type=F32)


def _ret_out(x, a, w_out, *, head_major):
    n, d = x.shape
    tm = min(TOKEN_TILE, n)
    if head_major:
        a_spec = pl.BlockSpec((a.shape[0], tm, a.shape[2]), lambda i: (0, i, 0))
    else:
        assert n == tm
        a_spec = pl.BlockSpec(a.shape, lambda i: (0, 0, 0))
    v = w_out.shape[0]
    return pl.pallas_call(
        functools.partial(_ret_out_kernel, head_major=head_major),
        grid=(n // tm,),
        in_specs=[pl.BlockSpec((tm, d), lambda i: (i, 0)),
                  a_spec,
                  pl.BlockSpec((v, d), lambda i: (0, 0), pipeline_mode=pl.Buffered(1))],
        out_specs=pl.BlockSpec((tm, d), lambda i: (i, 0)),
        out_shape=jax.ShapeDtypeStruct((n, d), F32),
        compiler_params=_params(1),
        name="ret_out",
    )(x, a, w_out)


def _decay_consts(chunk, heads, dk, dv):
    log_g = np.log1p(-(2.0 ** (-5.0 - np.arange(heads, dtype=np.float64))))
    n = np.arange(chunk, dtype=np.float64)
    diff = n[:, None] - n[None, :]
    intra = np.where(diff >= 0, np.exp(log_g[:, None, None] * np.maximum(diff, 0.0)), 0.0)
    cross = np.exp(log_g[:, None] * (n + 1.0))
    kdec = np.exp(log_g[:, None] * (chunk - 1.0 - n))
    sdec = np.exp(log_g * chunk)
    f32 = lambda a: jnp.asarray(np.ascontiguousarray(a), dtype=F32)
    return (f32(intra),
            f32(np.broadcast_to(cross[:, :, None], (heads, chunk, dv))),
            f32(np.broadcast_to(kdec[:, :, None], (heads, chunk, dk))),
            f32(np.broadcast_to(sdec[:, None, None], (heads, 1, dv))))


def _rotary_tables(pos, dk):
    half = dk // 2
    inv = ROPE_BASE ** (-np.arange(half, dtype=np.float64) / half)
    ang = np.asarray(pos, dtype=np.float64)[:, None] * inv[None, :]
    cos, sin = np.cos(ang), np.sin(ang)
    scale = dk ** -0.5
    f32 = lambda a: jnp.asarray(a, dtype=F32)
    return (f32(np.concatenate([cos, cos * scale, np.ones_like(cos)])),
            f32(np.concatenate([sin, sin * scale, np.zeros_like(sin)])))


def kernel(x_prompt, x_sample, p_prompt, p_sample, state_pool, state_ret, state_conv, norm_mix, norm_ffn,
           norm_ple, norm_final, pool_w, pool_scale, ret_w_in, ret_w_out, ffn_w_up, ffn_conv_w,
           ffn_conv_b, ffn_w_down, ple_w_proj, ple_w_gate):
    b, t, d = x_prompt.shape
    bs, ts, _ = x_sample.shape
    depth = norm_mix.shape[0]
    heads = RET_HEADS
    dk = d // heads
    dv = 2 * dk
    qk = heads * dk
    f2 = ffn_w_up.shape[2]
    ffn = f2 // 2
    assert ts >= CONV_W - 1 and PAST_LEN >= POOL_BUF

    row2 = lambda a: a.reshape(1, -1)
    t_major = lambda a: jnp.swapaxes(a, 0, 1).reshape(-1, a.shape[-1])
    b_major = lambda a: jnp.swapaxes(a.reshape(-1, bs, a.shape[-1]), 0, 1)

    xp = x_prompt.reshape(b * t, d)
    xs = t_major(x_sample)
    pp = p_prompt.reshape(depth, b * t, -1)
    ps = jnp.swapaxes(p_sample, 1, 2).reshape(depth, ts * bs, -1)

    tables_p = _rotary_tables(np.arange(t), dk)
    tables_s = _rotary_tables(PAST_LEN + np.repeat(np.arange(ts), bs), dk)
    chunk_p = RET_CHUNK if t % RET_CHUNK == 0 else t
    consts_p = _decay_consts(chunk_p, heads, dk, dv)
    consts_s = _decay_consts(ts, heads, dk, dv)

    new_pool_p, new_pool_s, new_ret_p, new_ret_s, new_conv_p, new_conv_s = [], [], [], [], [], []
    for i in range(depth):
        jm = i // 2
        if i % 2 == 0:
            w = pool_w[jm]
            g = row2(norm_mix[i])
            sc = row2(pool_scale[jm])
            xp, hl = _pool_mixer(xp, None, g, w, sc, shift=1, pos0=0, rows_per_seq=t)
            new_pool_p.append(hl[:, -POOL_BUF:, :])
            xs3 = x_sample if i == 0 else b_major(xs)
            xs, ns = _pool_mixer(xs3, t_major(state_pool[jm]), g, w, sc, shift=bs, pos0=PAST_LEN,
                                 rows_per_seq=ts * bs)
            new_pool_s.append(ns)
        else:
            w_in = ret_w_in[jm]
            w_out = ret_w_out[jm].astype(BF16)
            g = row2(norm_mix[i])
            proj_p = _ret_in(xp, g, w_in, tables_p, qk_width=qk, dk=dk, rows_per_seq=t, out_dtype=BF16)
            a_p, s_p = _ret_prompt(proj_p, consts_p, batch=b, seq=t, heads=heads, dk=dk, dv=dv)
            xp = _ret_out(xp, a_p, w_out, head_major=True)
            new_ret_p.append(s_p)
            proj_s = _ret_in(xs, g, w_in, tables_s, qk_width=qk, dk=dk, rows_per_seq=ts * bs, out_dtype=F32,
                             steps=ts)
            a_s, s_s = _ret_sample(proj_s, state_ret[jm], consts_s, heads=heads, dk=dk, dv=dv)
            xs = _ret_out(xs, a_s, w_out, head_major=False)
            new_ret_s.append(s_s)

        g = row2(norm_ffn[i])
        cb = row2(ffn_conv_b[i])
        xp, tg, tu = _conv_ffn(xp, g, None, ffn_w_up, ffn_conv_w, cb, ffn_w_down, i, shift=1, rows_per_seq=t)
        tiles_per_seq = tg.shape[0] // b
        last = jnp.concatenate([tg, tu], axis=-1)[tiles_per_seq - 1::tiles_per_seq]
        new_conv_p.append(last[:, -(CONV_W - 1):, :])
        xs, tg, tu = _conv_ffn(xs, g, state_conv, ffn_w_up, ffn_conv_w, cb, ffn_w_down, i, shift=bs,
                               rows_per_seq=ts * bs)
        new_conv_s.append(jnp.concatenate([tg, tu], axis=-1))

        g = row2(norm_ple[i])
        last_layer = i == depth - 1
        g_final = row2(norm_final) if last_layer else None
        xp = _ple(xp, pp, g, ple_w_gate, ple_w_proj, i, g_final)
        xs = _ple(xs, ps, g, ple_w_gate, ple_w_proj, i, g_final, steps=ts if last_layer else None)

    return (xp.reshape(b, t, d), xs,
            jnp.stack(new_pool_p), jnp.stack(new_pool_s),
            jnp.stack(new_ret_p), jnp.stack(new_ret_s),
            jnp.stack(new_conv_p), jnp.stack(new_conv_s))
```

```python
import functools

import jax
import jax.numpy as jnp
import numpy as np
from jax import lax
from jax.experimental import pallas as pl
from jax.experimental.pallas import tpu as pltpu

F32 = jnp.float32
BF16 = jnp.bfloat16

EPS = 1e-6
POOL_WINDOWS = (2, 4, 8, 16)
POOL_BUF = max(POOL_WINDOWS) - 1
RET_HEADS = 8
RET_CHUNK = 256
ROPE_BASE = 10000.0
CONV_W = 3
PAST_LEN = 16384

TOKEN_TILE = 512
WIDE_TILE = 1024
FFN_CHUNK = 512
ROW_BLOCK = 512
PROJ_CHUNK = 1024
SAMPLE_SEQS = 2
SUBLANES = 8
VMEM_LIMIT = 60 * 1024 * 1024


def _params(n_axes):
    return pltpu.CompilerParams(
        dimension_semantics=("arbitrary",) * n_axes, vmem_limit_bytes=VMEM_LIMIT)


def _rmsnorm(x, g):
    return x * lax.rsqrt(jnp.mean(x * x, axis=-1, keepdims=True) + EPS) * g


def _shift_rows(a, rows):
    return pltpu.roll(a, rows, axis=0)


def _time_major(ref, cols=slice(None)):
    return jnp.concatenate([ref[:, t, cols] for t in range(ref.shape[1])], axis=0)


def _store_batch_major(ref, val, cols=slice(None)):
    nb = ref.shape[0]
    for t in range(ref.shape[1]):
        ref[:, t, cols] = val[t * nb:(t + 1) * nb, :]


def _pool_group(h, prev, x, w, sc, pos, win, shift):
    halo = prev.shape[0]
    a = jnp.concatenate([prev, h], axis=0)
    span = 1
    while span < win:
        a = a + _shift_rows(a, span * shift)
        span *= 2
    cnt = jnp.minimum(pos + 1, win).astype(F32)
    dlt = a[halo:, :] / cnt - h
    return x + jnp.dot(dlt.astype(BF16), w, preferred_element_type=F32) * sc


def _pool_prompt_kernel(x_ref, g_ref, w_ref, sc_ref, o_ref, h_ref, carry_ref, *, tiles_per_seq):
    tm, d = x_ref.shape
    gw = d // len(POOL_WINDOWS)
    halo = carry_ref.shape[0]
    j = pl.program_id(0) % tiles_per_seq
    x = x_ref[...]
    h = _rmsnorm(x, g_ref[...])
    last = h[tm - halo:, :]
    h_ref[...] = last
    pos = j * tm + lax.broadcasted_iota(jnp.int32, (tm, 1), 0)
    for gi, win in enumerate(POOL_WINDOWS):
        cols = slice(gi * gw, (gi + 1) * gw)
        prev = jnp.where(j == 0, 0.0, carry_ref[:, cols])
        o_ref[:, cols] = _pool_group(h[:, cols], prev, x[:, cols], w_ref[gi], sc_ref[:, cols], pos, win, 1)
    carry_ref[...] = last


def _pool_state_kernel(x_ref, state_ref, g_ref, w_ref, sc_ref, o_ref, ns_ref, hs_ref, *, pos0):
    shift, steps, d = x_ref.shape
    tm = shift * steps
    gw = o_ref.shape[1]
    kept = ns_ref.shape[1]
    grp = pl.program_id(0)

    @pl.when(grp == 0)
    def _():
        hs_ref[...] = _rmsnorm(_time_major(x_ref), g_ref[...])

    row = lax.broadcasted_iota(jnp.int32, (tm, 1), 0)
    step = jnp.zeros_like(row)
    for k in range(1, steps):
        step = step + (row >= k * shift).astype(jnp.int32)
    pos = pos0 + step

    for gi, win in enumerate(POOL_WINDOWS):
        @pl.when(grp == gi)
        def _(gi=gi, win=win):
            cols = slice(gi * gw, (gi + 1) * gw)
            h = hs_ref[:, cols]
            prev = jnp.concatenate([jnp.zeros((shift, gw), F32), state_ref[...]], axis=0)
            o_ref[...] = _pool_group(h, prev, _time_major(x_ref, cols), w_ref[...], sc_ref[...],
                                     pos, win, shift)
            ext = jnp.concatenate([prev, h], axis=0)
            _store_batch_major(ns_ref, ext[ext.shape[0] - kept * shift:, :])


def _pool_mixer(x, halo, g, w, scale, *, shift, pos0, rows_per_seq):
    d = x.shape[-1]
    ng = len(POOL_WINDOWS)
    gw = d // ng
    if halo is None:
        n = x.shape[0]
        tm = min(TOKEN_TILE, n)
        assert shift == 1 and pos0 == 0
        tiles_per_seq = rows_per_seq // tm
        halo_rows = POOL_BUF + 1
        full = lambda a: pl.BlockSpec(a.shape, lambda i: (0,) * a.ndim)
        tile = pl.BlockSpec((tm, d), lambda i: (i, 0))
        return pl.pallas_call(
            functools.partial(_pool_prompt_kernel, tiles_per_seq=tiles_per_seq),
            grid=(n // tm,),
            in_specs=[tile, full(g), full(w), full(scale)],
            out_specs=[tile, pl.BlockSpec((None, halo_rows, d), lambda i: (i // tiles_per_seq, 0, 0))],
            out_shape=[jax.ShapeDtypeStruct((n, d), F32),
                       jax.ShapeDtypeStruct((n // rows_per_seq, halo_rows, d), F32)],
            scratch_shapes=[pltpu.VMEM((halo_rows, d), F32)],
            compiler_params=_params(1),
            name="pool_mixer",
        )(x, g, w, scale)
    nb, steps, _ = x.shape
    kept = halo.shape[0] // nb
    assert shift == nb and kept == POOL_BUF
    return pl.pallas_call(
        functools.partial(_pool_state_kernel, pos0=pos0),
        grid=(ng,),
        in_specs=[pl.BlockSpec((nb, steps, d), lambda gi: (0, 0, 0)),
                  pl.BlockSpec((kept * nb, gw), lambda gi: (0, gi)),
                  pl.BlockSpec((1, d), lambda gi: (0, 0)),
                  pl.BlockSpec((None, gw, gw), lambda gi: (gi, 0, 0)),
                  pl.BlockSpec((1, gw), lambda gi: (0, gi))],
        out_specs=[pl.BlockSpec((nb * steps, gw), lambda gi: (0, gi)),
                   pl.BlockSpec((nb, kept, gw), lambda gi: (0, 0, gi))],
        out_shape=[jax.ShapeDtypeStruct((nb * steps, d), F32), jax.ShapeDtypeStruct((nb, kept, d), F32)],
        scratch_shapes=[pltpu.VMEM((nb * steps, d), F32)],
        compiler_params=_params(1),
        name="pool_mixer_state",
    )(x, halo, g, w, scale)


def _ffn_kernel(*refs, shift, tiles_per_seq, from_state):
    if from_state:
        (x_ref, g_ref, sg_ref, su_ref, wg_ref, wu_ref, cwg_ref, cwu_ref, cbg_ref, cbu_ref, wd_ref,
         o_ref, tg_ref, tu_ref, hs_ref) = refs
        carry_g = carry_u = None
    else:
        (x_hbm, g_ref, wg_ref, wu_ref, cwg_ref, cwu_ref, cbg_ref, cbu_ref, wd_ref,
         o_ref, tg_ref, tu_ref, hs_ref, carry_g, carry_u, x_ref, x_sem) = refs
        sg_ref = su_ref = None
    tm, d = x_ref.shape
    halo = max((CONV_W - 1) * shift, SUBLANES)
    i = pl.program_id(0)
    f = pl.program_id(1)
    first_of_seq = (i % tiles_per_seq) == 0

    if not from_state:
        def x_copy(tile):
            return pltpu.make_async_copy(x_hbm.at[pl.ds(tile * tm, tm), :], x_ref, x_sem)

        @pl.when(jnp.logical_and(f == 0, i == 0))
        def _():
            x_copy(0).start()

        @pl.when(jnp.logical_and(f == 1, i + 1 < pl.num_programs(0)))
        def _():
            x_copy(i + 1).start()

    @pl.when(f == 0)
    def _():
        if not from_state:
            x_copy(i).wait()
        x = x_ref[...]
        hs_ref[...] = _rmsnorm(x, g_ref[...]).astype(BF16)
        o_ref[...] = x

    rb = min(ROW_BLOCK, tm)
    cwg, cwu, cbg, cbu = cwg_ref[...], cwu_ref[...], cbg_ref[...], cbu_ref[...]
    if from_state:
        assert rb == tm
        prev_g, prev_u = _time_major(sg_ref), _time_major(su_ref)
    else:
        prev_g = jnp.where(first_of_seq, 0.0, carry_g[f])
        prev_u = jnp.where(first_of_seq, 0.0, carry_u[f])

    def up(r):
        hs = hs_ref[r:r + rb, :]
        return (jnp.dot(hs, wg_ref[...], preferred_element_type=F32),
                jnp.dot(hs, wu_ref[...], preferred_element_type=F32))

    def conv(prev, u, cw, cb):
        ext = jnp.concatenate([prev, u], axis=0)
        if shift % SUBLANES == 0:
            u1 = ext[halo - shift:halo - shift + rb, :]
            u2 = ext[halo - 2 * shift:halo - 2 * shift + rb, :]
        else:
            u1 = _shift_rows(ext, shift)[halo:, :]
            u2 = _shift_rows(ext, 2 * shift)[halo:, :]
        return cw[0:1, :] * u2 + cw[1:2, :] * u1 + cw[2:3, :] * u + cb

    nxt = up(0)
    for r in range(0, tm, rb):
        ug, uu = nxt
        if r + rb < tm:
            nxt = up(r + rb)
        cg = conv(prev_g, ug, cwg, cbg)
        cu = conv(prev_u, uu, cwu, cbu)
        prev_g, prev_u = ug[rb - halo:, :], uu[rb - halo:, :]
        act = ((cg / (1.0 + jnp.exp(-cg))) * cu).astype(BF16)
        o_ref[r:r + rb, :] += jnp.dot(act, wd_ref[...], preferred_element_type=F32)
    if from_state:
        _store_batch_major(tg_ref, prev_g)
        _store_batch_major(tu_ref, prev_u)
    else:
        carry_g[f] = prev_g
        carry_u[f] = prev_u
        tg_ref[...] = prev_g
        tu_ref[...] = prev_u


def _conv_ffn(x, g, state, w_up, cw, cb, w_down, layer, *, shift, rows_per_seq):
    n, d = x.shape
    f2 = w_up.shape[2]
    ffn = f2 // 2
    tm = min(WIDE_TILE, n)
    fn = FFN_CHUNK
    nf = ffn // fn
    n_tiles = n // tm
    tiles_per_seq = max(rows_per_seq // tm, 1)
    halo = max((CONV_W - 1) * shift, SUBLANES)
    from_state = state is not None

    tile = pl.BlockSpec((tm, d), lambda i, f: (i, 0))
    tile_in = tile if from_state else pl.BlockSpec(memory_space=pl.ANY)
    gate_cols = lambda rows: pl.BlockSpec((rows, fn), lambda i, f: (0, f))
    up_cols = lambda rows: pl.BlockSpec((rows, fn), lambda i, f: (0, f + nf))
    layer_gate = lambda rows: pl.BlockSpec((None, rows, fn), lambda i, f: (layer, 0, f))
    layer_up = lambda rows: pl.BlockSpec((None, rows, fn), lambda i, f: (layer, 0, f + nf))
    in_specs = [tile_in, pl.BlockSpec((1, d), lambda i, f: (0, 0))]
    args = [x, g]
    scratch = [pltpu.VMEM((tm, d), BF16)]
    if from_state:
        nb, steps = state.shape[1], state.shape[2]
        assert n == tm and nb == shift and halo == steps * shift
        in_specs += [pl.BlockSpec((None, nb, steps, fn), lambda i, f: (layer, 0, 0, f)),
                     pl.BlockSpec((None, nb, steps, fn), lambda i, f: (layer, 0, 0, f + nf))]
        args += [state, state]
        tail_specs = [pl.BlockSpec((nb, steps, fn), lambda i, f: (0, 0, f))] * 2
        tail_shapes = [jax.ShapeDtypeStruct((nb, steps, ffn), F32)] * 2
    else:
        tail_specs = [pl.BlockSpec((None, halo, fn), lambda i, f: (i, 0, f))] * 2
        tail_shapes = [jax.ShapeDtypeStruct((n_tiles, halo, ffn), F32)] * 2
        scratch += [pltpu.VMEM((nf, halo, fn), F32), pltpu.VMEM((nf, halo, fn), F32),
                    pltpu.VMEM((tm, d), F32), pltpu.SemaphoreType.DMA(())]
    in_specs += [layer_gate(d), layer_up(d), layer_gate(CONV_W), layer_up(CONV_W), gate_cols(1), up_cols(1),
                 pl.BlockSpec((None, fn, d), lambda i, f: (layer, f, 0))]
    args += [w_up, w_up, cw, cw, cb, cb, w_down]
    return pl.pallas_call(
        functools.partial(_ffn_kernel, shift=shift, tiles_per_seq=tiles_per_seq, from_state=from_state),
        grid=(n_tiles, nf),
        in_specs=in_specs,
        out_specs=[tile] + tail_specs,
        out_shape=[jax.ShapeDtypeStruct((n, d), F32)] + tail_shapes,
        scratch_shapes=scratch,
        compiler_params=_params(2),
        name="conv_ffn_state" if from_state else "conv_ffn",
    )(*args)


def _ple_kernel(*refs, final):
    if final:
        x_ref, p_ref, g_ref, wg_ref, wp_ref, gf_ref, o_ref = refs
    else:
        x_ref, p_ref, g_ref, wg_ref, wp_ref, o_ref = refs
    x = x_ref[...]
    hn = _rmsnorm(x, g_ref[...]).astype(BF16)
    gate = jax.nn.sigmoid(jnp.dot(hn, wg_ref[...], preferred_element_type=F32))
    emb = jnp.dot(p_ref[...].astype(BF16), wp_ref[...], preferred_element_type=F32)
    y = x + gate * emb
    if final:
        y = _rmsnorm(y, gf_ref[...])
    if len(o_ref.shape) == 3:
        _store_batch_major(o_ref, y)
    else:
        o_ref[...] = y


def _ple(x, p, g, w_gate, w_proj, layer, g_final, steps=None):
    n, d = x.shape
    tm = min(TOKEN_TILE, n)
    final = g_final is not None
    if steps is None:
        out_spec = pl.BlockSpec((tm, d), lambda i: (i, 0))
        out_shape = jax.ShapeDtypeStruct((n, d), F32)
    else:
        assert n == tm
        out_spec = pl.BlockSpec((n // steps, steps, d), lambda i: (0, 0, 0))
        out_shape = jax.ShapeDtypeStruct((n // steps, steps, d), F32)
    const = lambda a: pl.BlockSpec(a.shape, lambda i: (0, 0))
    resident = lambda a: pl.BlockSpec((None,) + a.shape[1:], lambda i: (layer, 0, 0),
                                      pipeline_mode=pl.Buffered(1))
    in_specs = [pl.BlockSpec((tm, d), lambda i: (i, 0)),
                pl.BlockSpec((None, tm, p.shape[2]), lambda i: (layer, i, 0)),
                const(g), resident(w_gate), resident(w_proj)]
    args = [x, p, g, w_gate, w_proj]
    if final:
        in_specs.append(const(g_final))
        args.append(g_final)
    return pl.pallas_call(
        functools.partial(_ple_kernel, final=final),
        grid=(n // tm,),
        in_specs=in_specs,
        out_specs=out_spec,
        out_shape=out_shape,
        compiler_params=_params(1),
        name="ple_final" if final else "ple",
    )(*args)


def _ret_in_kernel(x_ref, g_ref, w_ref, cos_ref, sin_ref, o_ref, hs_ref, *, dk, layout):
    @pl.when(pl.program_id(1) == 0)
    def _():
        hs_ref[...] = _rmsnorm(x_ref[...], g_ref[...]).astype(BF16)

    tm = hs_ref.shape[0]
    rb = min(ROW_BLOCK, tm)
    half = dk // 2
    project = lambda r: jnp.dot(hs_ref[r:r + rb, :], w_ref[...], preferred_element_type=F32)
    nxt = project(0)
    for r in range(0, tm, rb):
        proj = nxt
        if r + rb < tm:
            nxt = project(r + rb)
        cos = cos_ref[r:r + rb, :]
        sin = sin_ref[r:r + rb, :]
        for hh in range(proj.shape[1] // dk):
            x1 = proj[:, hh * dk:hh * dk + half]
            x2 = proj[:, hh * dk + half:(hh + 1) * dk]
            y1 = (x1 * cos - x2 * sin).astype(o_ref.dtype)
            y2 = (x1 * sin + x2 * cos).astype(o_ref.dtype)
            if layout == "heads":
                o_ref[hh, r:r + rb, 0:half] = y1
                o_ref[hh, r:r + rb, half:dk] = y2
            else:
                assert rb == tm
                _store_batch_major(o_ref, y1, slice(hh * dk, hh * dk + half))
                _store_batch_major(o_ref, y2, slice(hh * dk + half, (hh + 1) * dk))


def _ret_in(x, g, w_in, tables, *, qk_width, dk, rows_per_seq, out_dtype, steps=None):
    n, d = x.shape
    e = w_in.shape[1]
    tm = min(WIDE_TILE, n)
    nc = PROJ_CHUNK
    qc = qk_width // nc
    tps = rows_per_seq // tm

    def table_block(i, c):
        return (jnp.where(c < qc, 0, jnp.where(c < 2 * qc, tps, 2 * tps)) + i % tps, 0)

    if steps is None:
        out_spec = pl.BlockSpec((nc // dk, tm, dk), lambda i, c: (c, i, 0))
        out_shape = jax.ShapeDtypeStruct((e // dk, n, dk), out_dtype)
    else:
        assert n == tm
        out_spec = pl.BlockSpec((n // steps, steps, nc), lambda i, c: (0, 0, c))
        out_shape = jax.ShapeDtypeStruct((n // steps, steps, e), out_dtype)

    return pl.pallas_call(
        functools.partial(_ret_in_kernel, dk=dk, layout="heads" if steps is None else "batch"),
        grid=(n // tm, e // nc),
        in_specs=[pl.BlockSpec((tm, d), lambda i, c: (i, 0)),
                  pl.BlockSpec((1, d), lambda i, c: (0, 0)),
                  pl.BlockSpec((d, nc), lambda i, c: (0, c)),
                  pl.BlockSpec((tm, dk // 2), table_block),
                  pl.BlockSpec((tm, dk // 2), table_block)],
        out_specs=out_spec,
        out_shape=out_shape,
        scratch_shapes=[pltpu.VMEM((tm, d), BF16)],
        compiler_params=_params(2),
        name="ret_in",
    )(x, g, w_in, *tables)


def _groupnorm_gate(o, gate):
    mu = jnp.mean(o, axis=-1, keepdims=True)
    var = jnp.mean(jnp.square(o - mu), axis=-1, keepdims=True)
    on = (o - mu) * lax.rsqrt(var + EPS)
    gf = gate.astype(F32)
    return (gf / (1.0 + jnp.exp(-gf))) * on


def _ret_prompt_kernel(q_ref, k_ref, v_ref, g_ref, intra_ref, cross_ref, kdec_ref, sdec_ref,
                       o_ref, s_out_ref, s_ref, *, chunk):
    t = q_ref.shape[0]
    s_ref[...] = jnp.zeros(s_ref.shape, F32)
    intra = intra_ref[...]
    cross = cross_ref[...]
    kdec = kdec_ref[...]
    sdec = sdec_ref[...]

    wide = lambda ref, rows: jnp.concatenate([ref[i, rows, :] for i in range(ref.shape[0])], axis=-1)

    def body(c, carry):
        rows = pl.ds(pl.multiple_of(c * chunk, chunk), chunk)
        q = q_ref[rows, :]
        k = k_ref[rows, :]
        v = wide(v_ref, rows)
        s = s_ref[...]
        sc = lax.dot_general(q, k, (((1,), (1,)), ((), ())), preferred_element_type=F32) * intra
        o = (jnp.dot(sc.astype(BF16), v, preferred_element_type=F32)
             + jnp.dot(q, s.astype(BF16), preferred_element_type=F32) * cross)
        kd = (k.astype(F32) * kdec).astype(BF16)
        s_ref[...] = sdec * s + lax.dot_general(kd, v, (((0,), (0,)), ((), ())),
                                                preferred_element_type=F32)
        o_ref[rows, :] = _groupnorm_gate(o, wide(g_ref, rows)).astype(o_ref.dtype)
        return carry

    lax.fori_loop(0, t // chunk, body, 0, unroll=2)
    s_out_ref[...] = s_ref[...]


def _ret_prompt(proj, consts, *, batch, seq, heads, dk, dv):
    intra, cross, kdec, sdec = consts
    chunk = intra.shape[1]
    per = dv // dk
    v_off = 2 * heads // per
    g_off = v_off + heads
    head_const = lambda a: pl.BlockSpec((None,) + a.shape[1:], lambda b, h: (h, 0, 0))
    return pl.pallas_call(
        functools.partial(_ret_prompt_kernel, chunk=chunk),
        grid=(batch, heads),
        in_specs=[pl.BlockSpec((None, seq, dk), lambda b, h: (h, b, 0)),
                  pl.BlockSpec((None, seq, dk), lambda b, h: (heads + h, b, 0)),
                  pl.BlockSpec((per, seq, dk), lambda b, h: (v_off + h, b, 0)),
                  pl.BlockSpec((per, seq, dk), lambda b, h: (g_off + h, b, 0)),
                  head_const(intra), head_const(cross), head_const(kdec), head_const(sdec)],
        out_specs=[pl.BlockSpec((None, seq, dv), lambda b, h: (h, b, 0)),
                   pl.BlockSpec((None, None, dk, dv), lambda b, h: (b, h, 0, 0))],
        out_shape=[jax.ShapeDtypeStruct((heads, batch * seq, dv), BF16),
                   jax.ShapeDtypeStruct((batch, heads, dk, dv), F32)],
        scratch_shapes=[pltpu.VMEM((dk, dv), F32)],
        compiler_params=_params(2),
        name="retention_prompt",
    )(proj, proj, proj, proj, intra, cross, kdec, sdec)


def _ret_sample_kernel(p_ref, s_ref, intra_ref, cross_ref, kdec_ref, sdec_ref, o_ref, s_out_ref,
                       *, heads, dk, dv):
    qk = heads * dk
    for n in range(p_ref.shape[0]):
        for h in range(heads):
            q = p_ref[n, :, h * dk:(h + 1) * dk].astype(BF16)
            kf = p_ref[n, :, qk + h * dk:qk + (h + 1) * dk]
            v = p_ref[n, :, 2 * qk + h * dv:2 * qk + (h + 1) * dv].astype(BF16)
            gate = p_ref[n, :, 2 * qk + heads * dv + h * dv:2 * qk + heads * dv + (h + 1) * dv]
            s = s_ref[n, h]
            sc = lax.dot_general(q, kf.astype(BF16), (((1,), (1,)), ((), ())),
                                 preferred_element_type=F32) * intra_ref[h]
            o = (jnp.dot(sc.astype(BF16), v, preferred_element_type=F32)
                 + jnp.dot(q, s.astype(BF16), preferred_element_type=F32) * cross_ref[h])
            kd = (kf * kdec_ref[h]).astype(BF16)
            s_out_ref[n, h] = sdec_ref[h] * s + lax.dot_general(kd, v, (((0,), (0,)), ((), ())),
                                                                preferred_element_type=F32)
            o_ref[n, :, h * dv:(h + 1) * dv] = _groupnorm_gate(o, gate).astype(o_ref.dtype)


def _ret_sample(proj, state, consts, *, heads, dk, dv):
    intra, cross, kdec, sdec = consts
    b, tp, e = proj.shape
    nb = SAMPLE_SEQS
    const = lambda a: pl.BlockSpec(a.shape, lambda i: (0,) * a.ndim)
    return pl.pallas_call(
        functools.partial(_ret_sample_kernel, heads=heads, dk=dk, dv=dv),
        grid=(b // nb,),
        in_specs=[pl.BlockSpec((nb, tp, e), lambda i: (i, 0, 0)),
                  pl.BlockSpec((nb, heads, dk, dv), lambda i: (i, 0, 0, 0)),
                  const(intra), const(cross), const(kdec), const(sdec)],
        out_specs=[pl.BlockSpec((nb, tp, heads * dv), lambda i: (i, 0, 0)),
                   pl.BlockSpec((nb, heads, dk, dv), lambda i: (i, 0, 0, 0))],
        out_shape=[jax.ShapeDtypeStruct((b, tp, heads * dv), F32),
                   jax.ShapeDtypeStruct((b, heads, dk, dv), F32)],
        compiler_params=_params(1),
        name="retention_sample",
    )(proj, state, intra, cross, kdec, sdec)


def _ret_out_kernel(x_ref, a_ref, w_ref, o_ref, *, head_major):
    if head_major:
        dv = a_ref.shape[2]
        y = x_ref[...]
        for h in range(a_ref.shape[0]):
            y = y + jnp.dot(a_ref[h], w_ref[h * dv:(h + 1) * dv, :], preferred_element_type=F32)
        o_ref[...] = y
    else:
        a = _time_major(a_ref).astype(BF16)
        o_ref[...] = x_ref[...] + jnp.dot(a, w_ref[...], preferred_element_type=F32)


def _ret_out(x, a, w_out, *, head_major):
    n, d = x.shape
    tm = min(TOKEN_TILE, n)
    if head_major:
        a_spec = pl.BlockSpec((a.shape[0], tm, a.shape[2]), lambda i: (0, i, 0))
    else:
        assert n == tm
        a_spec = pl.BlockSpec(a.shape, lambda i: (0, 0, 0))
    v = w_out.shape[0]
    return pl.pallas_call(
        functools.partial(_ret_out_kernel, head_major=head_major),
        grid=(n // tm,),
        in_specs=[pl.BlockSpec((tm, d), lambda i: (i, 0)),
                  a_spec,
                  pl.BlockSpec((v, d), lambda i: (0, 0), pipeline_mode=pl.Buffered(1))],
        out_specs=pl.BlockSpec((tm, d), lambda i: (i, 0)),
        out_shape=jax.ShapeDtypeStruct((n, d), F32),
        compiler_params=_params(1),
        name="ret_out",
    )(x, a, w_out)


def _decay_consts(chunk, heads, dk, dv):
    log_g = np.log1p(-(2.0 ** (-5.0 - np.arange(heads, dtype=np.float64))))
    n = np.arange(chunk, dtype=np.float64)
    diff = n[:, None] - n[None, :]
    intra = np.where(diff >= 0, np.exp(log_g[:, None, None] * np.maximum(diff, 0.0)), 0.0)
    cross = np.exp(log_g[:, None] * (n + 1.0))
    kdec = np.exp(log_g[:, None] * (chunk - 1.0 - n))
    sdec = np.exp(log_g * chunk)
    f32 = lambda a: jnp.asarray(np.ascontiguousarray(a), dtype=F32)
    return (f32(intra),
            f32(np.broadcast_to(cross[:, :, None], (heads, chunk, dv))),
            f32(np.broadcast_to(kdec[:, :, None], (heads, chunk, dk))),
            f32(np.broadcast_to(sdec[:, None, None], (heads, 1, dv))))


def _rotary_tables(pos, dk):
    half = dk // 2
    inv = ROPE_BASE ** (-np.arange(half, dtype=np.float64) / half)
    ang = np.asarray(pos, dtype=np.float64)[:, None] * inv[None, :]
    cos, sin = np.cos(ang), np.sin(ang)
    scale = dk ** -0.5
    f32 = lambda a: jnp.asarray(a, dtype=F32)
    return (f32(np.concatenate([cos, cos * scale, np.ones_like(cos)])),
            f32(np.concatenate([sin, sin * scale, np.zeros_like(sin)])))


def kernel(x_prompt, x_sample, p_prompt, p_sample, state_pool, state_ret, state_conv, norm_mix, norm_ffn,
           norm_ple, norm_final, pool_w, pool_scale, ret_w_in, ret_w_out, ffn_w_up, ffn_conv_w,
           ffn_conv_b, ffn_w_down, ple_w_proj, ple_w_gate):
    b, t, d = x_prompt.shape
    bs, ts, _ = x_sample.shape
    depth = norm_mix.shape[0]
    heads = RET_HEADS
    dk = d // heads
    dv = 2 * dk
    qk = heads * dk
    f2 = ffn_w_up.shape[2]
    ffn = f2 // 2
    assert ts >= CONV_W - 1 and PAST_LEN >= POOL_BUF

    row2 = lambda a: a.reshape(1, -1)
    t_major = lambda a: jnp.swapaxes(a, 0, 1).reshape(-1, a.shape[-1])
    b_major = lambda a: jnp.swapaxes(a.reshape(-1, bs, a.shape[-1]), 0, 1)

    xp = x_prompt.reshape(b * t, d)
    xs = t_major(x_sample)
    pp = p_prompt.reshape(depth, b * t, -1)
    ps = jnp.swapaxes(p_sample, 1, 2).reshape(depth, ts * bs, -1)

    tables_p = _rotary_tables(np.arange(t), dk)
    tables_s = _rotary_tables(PAST_LEN + np.repeat(np.arange(ts), bs), dk)
    chunk_p = RET_CHUNK if t % RET_CHUNK == 0 else t
    consts_p = _decay_consts(chunk_p, heads, dk, dv)
    consts_s = _decay_consts(ts, heads, dk, dv)

    new_pool_p, new_pool_s, new_ret_p, new_ret_s, new_conv_p, new_conv_s = [], [], [], [], [], []
    for i in range(depth):
        jm = i // 2
        if i % 2 == 0:
            w = pool_w[jm]
            g = row2(norm_mix[i])
            sc = row2(pool_scale[jm])
            xp, hl = _pool_mixer(xp, None, g, w, sc, shift=1, pos0=0, rows_per_seq=t)
            new_pool_p.append(hl[:, -POOL_BUF:, :])
            xs3 = x_sample if i == 0 else b_major(xs)
            xs, ns = _pool_mixer(xs3, t_major(state_pool[jm]), g, w, sc, shift=bs, pos0=PAST_LEN,
                                 rows_per_seq=ts * bs)
            new_pool_s.append(ns)
        else:
            w_in = ret_w_in[jm]
            w_out = ret_w_out[jm].astype(BF16)
            g = row2(norm_mix[i])
            proj_p = _ret_in(xp, g, w_in, tables_p, qk_width=qk, dk=dk, rows_per_seq=t, out_dtype=BF16)
            a_p, s_p = _ret_prompt(proj_p, consts_p, batch=b, seq=t, heads=heads, dk=dk, dv=dv)
            xp = _ret_out(xp, a_p, w_out, head_major=True)
            new_ret_p.append(s_p)
            proj_s = _ret_in(xs, g, w_in, tables_s, qk_width=qk, dk=dk, rows_per_seq=ts * bs, out_dtype=F32,
                             steps=ts)
            a_s, s_s = _ret_sample(proj_s, state_ret[jm], consts_s, heads=heads, dk=dk, dv=dv)
            xs = _ret_out(xs, a_s, w_out, head_major=False)
            new_ret_s.append(s_s)

        g = row2(norm_ffn[i])
        cb = row2(ffn_conv_b[i])
        xp, tg, tu = _conv_ffn(xp, g, None, ffn_w_up, ffn_conv_w, cb, ffn_w_down, i, shift=1, rows_per_seq=t)
        tiles_per_seq = tg.shape[0] // b
        last = jnp.concatenate([tg, tu], axis=-1)[tiles_per_seq - 1::tiles_per_seq]
        new_conv_p.append(last[:, -(CONV_W - 1):, :])
        xs, tg, tu = _conv_ffn(xs, g, state_conv, ffn_w_up, ffn_conv_w, cb, ffn_w_down, i, shift=bs,
                               rows_per_seq=ts * bs)
        new_conv_s.append(jnp.concatenate([tg, tu], axis=-1))

        g = row2(norm_ple[i])
        last_layer = i == depth - 1
        g_final = row2(norm_final) if last_layer else None
        xp = _ple(xp, pp, g, ple_w_gate, ple_w_proj, i, g_final)
        xs = _ple(xs, ps, g, ple_w_gate, ple_w_proj, i, g_final, steps=ts if last_layer else None)

    return (xp.reshape(b, t, d), xs,
            jnp.stack(new_pool_p), jnp.stack(new_pool_s),
            jnp.stack(new_ret_p), jnp.stack(new_ret_s),
            jnp.stack(new_conv_p), jnp.stack(new_conv_s))
```

```python
import functools

import jax
import jax.numpy as jnp
import numpy as np
from jax import lax
from jax.experimental import pallas as pl
from jax.experimental.pallas import tpu as pltpu

F32 = jnp.float32
BF16 = jnp.bfloat16

EPS = 1e-6
POOL_WINDOWS = (2, 4, 8, 16)
POOL_BUF = max(POOL_WINDOWS) - 1
RET_HEADS = 8
RET_CHUNK = 256
ROPE_BASE = 10000.0
CONV_W = 3
PAST_LEN = 16384

TOKEN_TILE = 512
WIDE_TILE = 1024
PROJ_TILE = 2048
FFN_CHUNK = 512
ROW_BLOCK = 512
PROJ_CHUNK = 1024
SAMPLE_SEQS = 2
SUBLANES = 8
VMEM_LIMIT = 60 * 1024 * 1024


def _params(n_axes):
    return pltpu.CompilerParams(
        dimension_semantics=("arbitrary",) * n_axes, vmem_limit_bytes=VMEM_LIMIT)


def _rmsnorm(x, g):
    return x * lax.rsqrt(jnp.mean(x * x, axis=-1, keepdims=True) + EPS) * g


def _shift_rows(a, rows):
    return pltpu.roll(a, rows, axis=0)


def _tile_fetch(x_hbm, x_ref, sem):
    tm = x_ref.shape[0]
    i = pl.program_id(0)
    f = pl.program_id(1)

    def copy(tile):
        return pltpu.make_async_copy(x_hbm.at[pl.ds(tile * tm, tm), :], x_ref, sem)

    @pl.when(jnp.logical_and(f == 0, i == 0))
    def _():
        copy(0).start()

    @pl.when(jnp.logical_and(f == 1, i + 1 < pl.num_programs(0)))
    def _():
        copy(i + 1).start()

    return lambda: copy(i).wait()


def _time_major(ref, cols=slice(None)):
    return jnp.concatenate([ref[:, t, cols] for t in range(ref.shape[1])], axis=0)


def _store_batch_major(ref, val, cols=slice(None)):
    nb = ref.shape[0]
    for t in range(ref.shape[1]):
        ref[:, t, cols] = val[t * nb:(t + 1) * nb, :]


def _pool_group(h, prev, x, w, sc, pos, win, shift):
    halo = prev.shape[0]
    a = jnp.concatenate([prev, h], axis=0)
    span = 1
    while span < win:
        a = a + _shift_rows(a, span * shift)
        span *= 2
    cnt = jnp.minimum(pos + 1, win).astype(F32)
    dlt = a[halo:, :] / cnt - h
    return x + jnp.dot(dlt.astype(BF16), w, preferred_element_type=F32) * sc


def _pool_prompt_kernel(x_ref, g_ref, w_ref, sc_ref, o_ref, h_ref, carry_ref, *, tiles_per_seq):
    tm, d = x_ref.shape
    gw = d // len(POOL_WINDOWS)
    halo = carry_ref.shape[0]
    j = pl.program_id(0) % tiles_per_seq
    x = x_ref[...]
    h = _rmsnorm(x, g_ref[...])
    last = h[tm - halo:, :]
    h_ref[...] = last
    pos = j * tm + lax.broadcasted_iota(jnp.int32, (tm, 1), 0)
    for gi, win in enumerate(POOL_WINDOWS):
        cols = slice(gi * gw, (gi + 1) * gw)
        prev = jnp.where(j == 0, 0.0, carry_ref[:, cols])
        o_ref[:, cols] = _pool_group(h[:, cols], prev, x[:, cols], w_ref[gi], sc_ref[:, cols], pos, win, 1)
    carry_ref[...] = last


def _pool_state_kernel(x_ref, state_ref, g_ref, w_ref, sc_ref, o_ref, ns_ref, hs_ref, *, pos0):
    shift, steps, d = x_ref.shape
    tm = shift * steps
    gw = o_ref.shape[1]
    kept = ns_ref.shape[1]
    grp = pl.program_id(0)

    @pl.when(grp == 0)
    def _():
        hs_ref[...] = _rmsnorm(_time_major(x_ref), g_ref[...])

    row = lax.broadcasted_iota(jnp.int32, (tm, 1), 0)
    step = jnp.zeros_like(row)
    for k in range(1, steps):
        step = step + (row >= k * shift).astype(jnp.int32)
    pos = pos0 + step

    for gi, win in enumerate(POOL_WINDOWS):
        @pl.when(grp == gi)
        def _(gi=gi, win=win):
            cols = slice(gi * gw, (gi + 1) * gw)
            h = hs_ref[:, cols]
            prev = jnp.concatenate([jnp.zeros((shift, gw), F32), state_ref[...]], axis=0)
            o_ref[...] = _pool_group(h, prev, _time_major(x_ref, cols), w_ref[...], sc_ref[...],
                                     pos, win, shift)
            ext = jnp.concatenate([prev, h], axis=0)
            _store_batch_major(ns_ref, ext[ext.shape[0] - kept * shift:, :])


def _pool_mixer(x, halo, g, w, scale, *, shift, pos0, rows_per_seq):
    d = x.shape[-1]
    ng = len(POOL_WINDOWS)
    gw = d // ng
    if halo is None:
        n = x.shape[0]
        tm = min(TOKEN_TILE, n)
        assert shift == 1 and pos0 == 0
        tiles_per_seq = rows_per_seq // tm
        halo_rows = POOL_BUF + 1
        full = lambda a: pl.BlockSpec(a.shape, lambda i: (0,) * a.ndim)
        tile = pl.BlockSpec((tm, d), lambda i: (i, 0))
        return pl.pallas_call(
            functools.partial(_pool_prompt_kernel, tiles_per_seq=tiles_per_seq),
            grid=(n // tm,),
            in_specs=[tile, full(g), full(w), full(scale)],
            out_specs=[tile, pl.BlockSpec((None, halo_rows, d), lambda i: (i // tiles_per_seq, 0, 0))],
            out_shape=[jax.ShapeDtypeStruct((n, d), F32),
                       jax.ShapeDtypeStruct((n // rows_per_seq, halo_rows, d), F32)],
            scratch_shapes=[pltpu.VMEM((halo_rows, d), F32)],
            compiler_params=_params(1),
            name="pool_mixer",
        )(x, g, w, scale)
    nb, steps, _ = x.shape
    kept = halo.shape[0] // nb
    assert shift == nb and kept == POOL_BUF
    return pl.pallas_call(
        functools.partial(_pool_state_kernel, pos0=pos0),
        grid=(ng,),
        in_specs=[pl.BlockSpec((nb, steps, d), lambda gi: (0, 0, 0)),
                  pl.BlockSpec((kept * nb, gw), lambda gi: (0, gi)),
                  pl.BlockSpec((1, d), lambda gi: (0, 0)),
                  pl.BlockSpec((None, gw, gw), lambda gi: (gi, 0, 0)),
                  pl.BlockSpec((1, gw), lambda gi: (0, gi))],
        out_specs=[pl.BlockSpec((nb * steps, gw), lambda gi: (0, gi)),
                   pl.BlockSpec((nb, kept, gw), lambda gi: (0, 0, gi))],
        out_shape=[jax.ShapeDtypeStruct((nb * steps, d), F32), jax.ShapeDtypeStruct((nb, kept, d), F32)],
        scratch_shapes=[pltpu.VMEM((nb * steps, d), F32)],
        compiler_params=_params(1),
        name="pool_mixer_state",
    )(x, halo, g, w, scale)


def _ffn_kernel(*refs, shift, tiles_per_seq, from_state):
    if from_state:
        (x_ref, g_ref, sg_ref, su_ref, wg_ref, wu_ref, cwg_ref, cwu_ref, cbg_ref, cbu_ref, wd_ref,
         o_ref, tg_ref, tu_ref, hs_ref) = refs
        carry_g = carry_u = None
    else:
        (x_hbm, g_ref, wg_ref, wu_ref, cwg_ref, cwu_ref, cbg_ref, cbu_ref, wd_ref,
         o_ref, tg_ref, tu_ref, hs_ref, carry_g, carry_u, x_ref, x_sem) = refs
        sg_ref = su_ref = None
    tm, d = x_ref.shape
    halo = max((CONV_W - 1) * shift, SUBLANES)
    i = pl.program_id(0)
    f = pl.program_id(1)
    first_of_seq = (i % tiles_per_seq) == 0

    wait_x = None if from_state else _tile_fetch(x_hbm, x_ref, x_sem)

    @pl.when(f == 0)
    def _():
        if wait_x is not None:
            wait_x()
        x = x_ref[...]
        hs_ref[...] = _rmsnorm(x, g_ref[...]).astype(BF16)
        o_ref[...] = x

    rb = min(ROW_BLOCK, tm)
    cwg, cwu, cbg, cbu = cwg_ref[...], cwu_ref[...], cbg_ref[...], cbu_ref[...]
    if from_state:
        assert rb == tm
        prev_g, prev_u = _time_major(sg_ref), _time_major(su_ref)
    else:
        prev_g = jnp.where(first_of_seq, 0.0, carry_g[f])
        prev_u = jnp.where(first_of_seq, 0.0, carry_u[f])

    def up(r):
        hs = hs_ref[r:r + rb, :]
        return (jnp.dot(hs, wg_ref[...], preferred_element_type=F32),
                jnp.dot(hs, wu_ref[...], preferred_element_type=F32))

    def conv(prev, u, cw, cb):
        ext = jnp.concatenate([prev, u], axis=0)
        if shift % SUBLANES == 0:
            u1 = ext[halo - shift:halo - shift + rb, :]
            u2 = ext[halo - 2 * shift:halo - 2 * shift + rb, :]
        else:
            u1 = _shift_rows(ext, shift)[halo:, :]
            u2 = _shift_rows(ext, 2 * shift)[halo:, :]
        return cw[0:1, :] * u2 + cw[1:2, :] * u1 + cw[2:3, :] * u + cb

    nxt = up(0)
    for r in range(0, tm, rb):
        ug, uu = nxt
        if r + rb < tm:
            nxt = up(r + rb)
        cg = conv(prev_g, ug, cwg, cbg)
        cu = conv(prev_u, uu, cwu, cbu)
        prev_g, prev_u = ug[rb - halo:, :], uu[rb - halo:, :]
        act = ((cg / (1.0 + jnp.exp(-cg))) * cu).astype(BF16)
        o_ref[r:r + rb, :] += jnp.dot(act, wd_ref[...], preferred_element_type=F32)
    if from_state:
        _store_batch_major(tg_ref, prev_g)
        _store_batch_major(tu_ref, prev_u)
    else:
        carry_g[f] = prev_g
        carry_u[f] = prev_u
        tg_ref[...] = prev_g
        tu_ref[...] = prev_u


def _conv_ffn(x, g, state, w_up, cw, cb, w_down, layer, *, shift, rows_per_seq):
    n, d = x.shape
    f2 = w_up.shape[2]
    ffn = f2 // 2
    tm = min(WIDE_TILE, n)
    fn = FFN_CHUNK
    nf = ffn // fn
    n_tiles = n // tm
    tiles_per_seq = max(rows_per_seq // tm, 1)
    halo = max((CONV_W - 1) * shift, SUBLANES)
    from_state = state is not None

    tile = pl.BlockSpec((tm, d), lambda i, f: (i, 0))
    tile_in = tile if from_state else pl.BlockSpec(memory_space=pl.ANY)
    gate_cols = lambda rows: pl.BlockSpec((rows, fn), lambda i, f: (0, f))
    up_cols = lambda rows: pl.BlockSpec((rows, fn), lambda i, f: (0, f + nf))
    layer_gate = lambda rows: pl.BlockSpec((None, rows, fn), lambda i, f: (layer, 0, f))
    layer_up = lambda rows: pl.BlockSpec((None, rows, fn), lambda i, f: (layer, 0, f + nf))
    in_specs = [tile_in, pl.BlockSpec((1, d), lambda i, f: (0, 0))]
    args = [x, g]
    scratch = [pltpu.VMEM((tm, d), BF16)]
    if from_state:
        nb, steps = state.shape[1], state.shape[2]
        assert n == tm and nb == shift and halo == steps * shift
        in_specs += [pl.BlockSpec((None, nb, steps, fn), lambda i, f: (layer, 0, 0, f)),
                     pl.BlockSpec((None, nb, steps, fn), lambda i, f: (layer, 0, 0, f + nf))]
        args += [state, state]
        tail_specs = [pl.BlockSpec((nb, steps, fn), lambda i, f: (0, 0, f))] * 2
        tail_shapes = [jax.ShapeDtypeStruct((nb, steps, ffn), F32)] * 2
    else:
        tail_specs = [pl.BlockSpec((None, halo, fn), lambda i, f: (i, 0, f))] * 2
        tail_shapes = [jax.ShapeDtypeStruct((n_tiles, halo, ffn), F32)] * 2
        scratch += [pltpu.VMEM((nf, halo, fn), F32), pltpu.VMEM((nf, halo, fn), F32),
                    pltpu.VMEM((tm, d), F32), pltpu.SemaphoreType.DMA(())]
    in_specs += [layer_gate(d), layer_up(d), layer_gate(CONV_W), layer_up(CONV_W), gate_cols(1), up_cols(1),
                 pl.BlockSpec((None, fn, d), lambda i, f: (layer, f, 0))]
    args += [w_up, w_up, cw, cw, cb, cb, w_down]
    return pl.pallas_call(
        functools.partial(_ffn_kernel, shift=shift, tiles_per_seq=tiles_per_seq, from_state=from_state),
        grid=(n_tiles, nf),
        in_specs=in_specs,
        out_specs=[tile] + tail_specs,
        out_shape=[jax.ShapeDtypeStruct((n, d), F32)] + tail_shapes,
        scratch_shapes=scratch,
        compiler_params=_params(2),
        name="conv_ffn_state" if from_state else "conv_ffn",
    )(*args)


def _ple_kernel(*refs, final):
    if final:
        x_ref, p_ref, g_ref, wg_ref, wp_ref, gf_ref, o_ref = refs
    else:
        x_ref, p_ref, g_ref, wg_ref, wp_ref, o_ref = refs
    x = x_ref[...]
    hn = _rmsnorm(x, g_ref[...]).astype(BF16)
    gate = jax.nn.sigmoid(jnp.dot(hn, wg_ref[...], preferred_element_type=F32))
    emb = jnp.dot(p_ref[...].astype(BF16), wp_ref[...], preferred_element_type=F32)
    y = x + gate * emb
    if final:
        y = _rmsnorm(y, gf_ref[...])
    if len(o_ref.shape) == 3:
        _store_batch_major(o_ref, y)
    else:
        o_ref[...] = y


def _ple(x, p, g, w_gate, w_proj, layer, g_final, steps=None):
    n, d = x.shape
    tm = min(TOKEN_TILE, n)
    final = g_final is not None
    if steps is None:
        out_spec = pl.BlockSpec((tm, d), lambda i: (i, 0))
        out_shape = jax.ShapeDtypeStruct((n, d), F32)
    else:
        assert n == tm
        out_spec = pl.BlockSpec((n // steps, steps, d), lambda i: (0, 0, 0))
        out_shape = jax.ShapeDtypeStruct((n // steps, steps, d), F32)
    const = lambda a: pl.BlockSpec(a.shape, lambda i: (0, 0))
    resident = lambda a: pl.BlockSpec((None,) + a.shape[1:], lambda i: (layer, 0, 0),
                                      pipeline_mode=pl.Buffered(1))
    in_specs = [pl.BlockSpec((tm, d), lambda i: (i, 0)),
                pl.BlockSpec((None, tm, p.shape[2]), lambda i: (layer, i, 0)),
                const(g), resident(w_gate), resident(w_proj)]
    args = [x, p, g, w_gate, w_proj]
    if final:
        in_specs.append(const(g_final))
        args.append(g_final)
    return pl.pallas_call(
        functools.partial(_ple_kernel, final=final),
        grid=(n // tm,),
        in_specs=in_specs,
        out_specs=out_spec,
        out_shape=out_shape,
        compiler_params=_params(1),
        name="ple_final" if final else "ple",
    )(*args)


def _ret_in_kernel(x_in, g_ref, w_ref, cos_ref, sin_ref, o_ref, hs_ref, *fetch, dk, layout):
    if fetch:
        x_ref = fetch[0]
        wait_x = _tile_fetch(x_in, *fetch)
    else:
        x_ref, wait_x = x_in, None

    @pl.when(pl.program_id(1) == 0)
    def _():
        if wait_x is not None:
            wait_x()
        hs_ref[...] = _rmsnorm(x_ref[...], g_ref[...]).astype(BF16)

    tm = hs_ref.shape[0]
    rb = min(ROW_BLOCK, tm)
    half = dk // 2
    project = lambda r: jnp.dot(hs_ref[r:r + rb, :], w_ref[...], preferred_element_type=F32)
    nxt = project(0)
    for r in range(0, tm, rb):
        proj = nxt
        if r + rb < tm:
            nxt = project(r + rb)
        cos = cos_ref[r:r + rb, :]
        sin = sin_ref[r:r + rb, :]
        for hh in range(proj.shape[1] // dk):
            x1 = proj[:, hh * dk:hh * dk + half]
            x2 = proj[:, hh * dk + half:(hh + 1) * dk]
            y1 = (x1 * cos - x2 * sin).astype(o_ref.dtype)
            y2 = (x1 * sin + x2 * cos).astype(o_ref.dtype)
            if layout == "heads":
                o_ref[hh, r:r + rb, 0:half] = y1
                o_ref[hh, r:r + rb, half:dk] = y2
            else:
                assert rb == tm
                _store_batch_major(o_ref, y1, slice(hh * dk, hh * dk + half))
                _store_batch_major(o_ref, y2, slice(hh * dk + half, (hh + 1) * dk))


def _ret_in(x, g, w_in, tables, *, qk_width, dk, rows_per_seq, out_dtype, steps=None):
    n, d = x.shape
    e = w_in.shape[1]
    tm = min(PROJ_TILE, n)
    nc = PROJ_CHUNK
    qc = qk_width // nc
    tps = rows_per_seq // tm

    def table_block(i, c):
        return (jnp.where(c < qc, 0, jnp.where(c < 2 * qc, tps, 2 * tps)) + i % tps, 0)

    scratch = [pltpu.VMEM((tm, d), BF16)]
    if steps is None:
        x_spec = pl.BlockSpec(memory_space=pl.ANY)
        w_spec = pl.BlockSpec((d, nc), lambda i, c: (0, c))
        out_spec = pl.BlockSpec((nc // dk, tm, dk), lambda i, c: (c, i, 0))
        out_shape = jax.ShapeDtypeStruct((e // dk, n, dk), out_dtype)
        scratch += [pltpu.VMEM((tm, d), F32), pltpu.SemaphoreType.DMA(())]
    else:
        assert n == tm
        x_spec = pl.BlockSpec((tm, d), lambda i, c: (i, 0))
        w_spec = pl.BlockSpec((d, nc), lambda i, c: (0, c))
        out_spec = pl.BlockSpec((n // steps, steps, nc), lambda i, c: (0, 0, c))
        out_shape = jax.ShapeDtypeStruct((n // steps, steps, e), out_dtype)

    return pl.pallas_call(
        functools.partial(_ret_in_kernel, dk=dk, layout="heads" if steps is None else "batch"),
        grid=(n // tm, e // nc),
        in_specs=[x_spec,
                  pl.BlockSpec((1, d), lambda i, c: (0, 0)),
                  w_spec,
                  pl.BlockSpec((tm, dk // 2), table_block),
                  pl.BlockSpec((tm, dk // 2), table_block)],
        out_specs=out_spec,
        out_shape=out_shape,
        scratch_shapes=scratch,
        compiler_params=_params(2),
        name="ret_in",
    )(x, g, w_in, *tables)


def _groupnorm_gate(o, gate):
    mu = jnp.mean(o, axis=-1, keepdims=True)
    var = jnp.mean(jnp.square(o - mu), axis=-1, keepdims=True)
    on = (o - mu) * lax.rsqrt(var + EPS)
    gf = gate.astype(F32)
    return (gf / (1.0 + jnp.exp(-gf))) * on


def _ret_prompt_kernel(q_ref, k_ref, v_ref, g_ref, intra_ref, cross_ref, kdec_ref, sdec_ref,
                       o_ref, s_out_ref, s_ref, *, chunk):
    t = q_ref.shape[0]
    s_ref[...] = jnp.zeros(s_ref.shape, F32)
    intra = intra_ref[...]
    cross = cross_ref[...]
    kdec = kdec_ref[...]
    sdec = sdec_ref[...]

    wide = lambda ref, rows: jnp.concatenate([ref[i, rows, :] for i in range(ref.shape[0])], axis=-1)

    def body(c, carry):
        rows = pl.ds(pl.multiple_of(c * chunk, chunk), chunk)
        q = q_ref[rows, :]
        k = k_ref[rows, :]
        v = wide(v_ref, rows)
        s = s_ref[...]
        sc = lax.dot_general(q, k, (((1,), (1,)), ((), ())), preferred_element_type=F32) * intra
        o = (jnp.dot(sc.astype(BF16), v, preferred_element_type=F32)
             + jnp.dot(q, s.astype(BF16), preferred_element_type=F32) * cross)
        kd = (k.astype(F32) * kdec).astype(BF16)
        s_ref[...] = sdec * s + lax.dot_general(kd, v, (((0,), (0,)), ((), ())),
                                                preferred_element_type=F32)
        o_ref[rows, :] = _groupnorm_gate(o, wide(g_ref, rows)).astype(o_ref.dtype)
        return carry

    lax.fori_loop(0, t // chunk, body, 0, unroll=2)
    s_out_ref[...] = s_ref[...]


def _ret_prompt(proj, consts, *, batch, seq, heads, dk, dv):
    intra, cross, kdec, sdec = consts
    chunk = intra.shape[1]
    per = dv // dk
    v_off = 2 * heads // per
    g_off = v_off + heads
    head_const = lambda a: pl.BlockSpec((None,) + a.shape[1:], lambda b, h: (h, 0, 0))
    return pl.pallas_call(
        functools.partial(_ret_prompt_kernel, chunk=chunk),
        grid=(batch, heads),
        in_specs=[pl.BlockSpec((None, seq, dk), lambda b, h: (h, b, 0)),
                  pl.BlockSpec((None, seq, dk), lambda b, h: (heads + h, b, 0)),
                  pl.BlockSpec((per, seq, dk), lambda b, h: (v_off + h, b, 0)),
                  pl.BlockSpec((per, seq, dk), lambda b, h: (g_off + h, b, 0)),
                  head_const(intra), head_const(cross), head_const(kdec), head_const(sdec)],
        out_specs=[pl.BlockSpec((None, seq, dv), lambda b, h: (h, b, 0)),
                   pl.BlockSpec((None, None, dk, dv), lambda b, h: (b, h, 0, 0))],
        out_shape=[jax.ShapeDtypeStruct((heads, batch * seq, dv), BF16),
                   jax.ShapeDtypeStruct((batch, heads, dk, dv), F32)],
        scratch_shapes=[pltpu.VMEM((dk, dv), F32)],
        compiler_params=_params(2),
        name="retention_prompt",
    )(proj, proj, proj, proj, intra, cross, kdec, sdec)


def _ret_sample_kernel(p_ref, s_ref, intra_ref, cross_ref, kdec_ref, sdec_ref, o_ref, s_out_ref,
                       *, heads, dk, dv):
    qk = heads * dk
    for n in range(p_ref.shape[0]):
        for h in range(heads):
            q = p_ref[n, :, h * dk:(h + 1) * dk].astype(BF16)
            kf = p_ref[n, :, qk + h * dk:qk + (h + 1) * dk]
            v = p_ref[n, :, 2 * qk + h * dv:2 * qk + (h + 1) * dv].astype(BF16)
            gate = p_ref[n, :, 2 * qk + heads * dv + h * dv:2 * qk + heads * dv + (h + 1) * dv]
            s = s_ref[n, h]
            sc = lax.dot_general(q, kf.astype(BF16), (((1,), (1,)), ((), ())),
                                 preferred_element_type=F32) * intra_ref[h]
            o = (jnp.dot(sc.astype(BF16), v, preferred_element_type=F32)
                 + jnp.dot(q, s.astype(BF16), preferred_element_type=F32) * cross_ref[h])
            kd = (kf * kdec_ref[h]).astype(BF16)
            s_out_ref[n, h] = sdec_ref[h] * s + lax.dot_general(kd, v, (((0,), (0,)), ((), ())),
                                                                preferred_element_type=F32)
            o_ref[n, :, h * dv:(h + 1) * dv] = _groupnorm_gate(o, gate).astype(o_ref.dtype)


def _ret_sample(proj, state, consts, *, heads, dk, dv):
    intra, cross, kdec, sdec = consts
    b, tp, e = proj.shape
    nb = SAMPLE_SEQS
    const = lambda a: pl.BlockSpec(a.shape, lambda i: (0,) * a.ndim)
    return pl.pallas_call(
        functools.partial(_ret_sample_kernel, heads=heads, dk=dk, dv=dv),
        grid=(b // nb,),
        in_specs=[pl.BlockSpec((nb, tp, e), lambda i: (i, 0, 0)),
                  pl.BlockSpec((nb, heads, dk, dv), lambda i: (i, 0, 0, 0)),
                  const(intra), const(cross), const(kdec), const(sdec)],
        out_specs=[pl.BlockSpec((nb, tp, heads * dv), lambda i: (i, 0, 0)),
                   pl.BlockSpec((nb, heads, dk, dv), lambda i: (i, 0, 0, 0))],
        out_shape=[jax.ShapeDtypeStruct((b, tp, heads * dv), F32),
                   jax.ShapeDtypeStruct((b, heads, dk, dv), F32)],
        compiler_params=_params(1),
        name="retention_sample",
    )(proj, state, intra, cross, kdec, sdec)


def _ret_out_kernel(x_ref, a_ref, w_ref, o_ref, *, head_major):
    if head_major:
        dv = a_ref.shape[2]
        y = x_ref[...]
        for h in range(a_ref.shape[0]):
            y = y + jnp.dot(a_ref[h], w_ref[h * dv:(h + 1) * dv, :], preferred_element_type=F32)
        o_ref[...] = y
    else:
        a = _time_major(a_ref).astype(BF16)
        o_ref[...] = x_ref[...] + jnp.dot(a, w_ref[...], preferred_element_type=F32)


def _ret_out(x, a, w_out, *, head_major):
    n, d = x.shape
    tm = min(TOKEN_TILE, n)
    if head_major:
        a_spec = pl.BlockSpec((a.shape[0], tm, a.shape[2]), lambda i: (0, i, 0))
    else:
        assert n == tm
        a_spec = pl.BlockSpec(a.shape, lambda i: (0, 0, 0))
    v = w_out.shape[0]
    return pl.pallas_call(
        functools.partial(_ret_out_kernel, head_major=head_major),
        grid=(n // tm,),
        in_specs=[pl.BlockSpec((tm, d), lambda i: (i, 0)),
                  a_spec,
                  pl.BlockSpec((v, d), lambda i: (0, 0), pipeline_mode=pl.Buffered(1))],
        out_specs=pl.BlockSpec((tm, d), lambda i: (i, 0)),
        out_shape=jax.ShapeDtypeStruct((n, d), F32),
        compiler_params=_params(1),
        name="ret_out",
    )(x, a, w_out)


def _decay_consts(chunk, heads, dk, dv):
    log_g = np.log1p(-(2.0 ** (-5.0 - np.arange(heads, dtype=np.float64))))
    n = np.arange(chunk, dtype=np.float64)
    diff = n[:, None] - n[None, :]
    intra = np.where(diff >= 0, np.exp(log_g[:, None, None] * np.maximum(diff, 0.0)), 0.0)
    cross = np.exp(log_g[:, None] * (n + 1.0))
    kdec = np.exp(log_g[:, None] * (chunk - 1.0 - n))
    sdec = np.exp(log_g * chunk)
    f32 = lambda a: jnp.asarray(np.ascontiguousarray(a), dtype=F32)
    return (f32(intra),
            f32(np.broadcast_to(cross[:, :, None], (heads, chunk, dv))),
            f32(np.broadcast_to(kdec[:, :, None], (heads, chunk, dk))),
            f32(np.broadcast_to(sdec[:, None, None], (heads, 1, dv))))


def _rotary_tables(pos, dk):
    half = dk // 2
    inv = ROPE_BASE ** (-np.arange(half, dtype=np.float64) / half)
    ang = np.asarray(pos, dtype=np.float64)[:, None] * inv[None, :]
    cos, sin = np.cos(ang), np.sin(ang)
    scale = dk ** -0.5
    f32 = lambda a: jnp.asarray(a, dtype=F32)
    return (f32(np.concatenate([cos, cos * scale, np.ones_like(cos)])),
            f32(np.concatenate([sin, sin * scale, np.zeros_like(sin)])))


def kernel(x_prompt, x_sample, p_prompt, p_sample, state_pool, state_ret, state_conv, norm_mix, norm_ffn,
           norm_ple, norm_final, pool_w, pool_scale, ret_w_in, ret_w_out, ffn_w_up, ffn_conv_w,
           ffn_conv_b, ffn_w_down, ple_w_proj, ple_w_gate):
    b, t, d = x_prompt.shape
    bs, ts, _ = x_sample.shape
    depth = norm_mix.shape[0]
    heads = RET_HEADS
    dk = d // heads
    dv = 2 * dk
    qk = heads * dk
    f2 = ffn_w_up.shape[2]
    ffn = f2 // 2
    assert ts >= CONV_W - 1 and PAST_LEN >= POOL_BUF

    row2 = lambda a: a.reshape(1, -1)
    t_major = lambda a: jnp.swapaxes(a, 0, 1).reshape(-1, a.shape[-1])
    b_major = lambda a: jnp.swapaxes(a.reshape(-1, bs, a.shape[-1]), 0, 1)

    xp = x_prompt.reshape(b * t, d)
    xs = t_major(x_sample)
    pp = p_prompt.reshape(depth, b * t, -1)
    ps = jnp.swapaxes(p_sample, 1, 2).reshape(depth, ts * bs, -1)

    tables_p = _rotary_tables(np.arange(t), dk)
    tables_s = _rotary_tables(PAST_LEN + np.repeat(np.arange(ts), bs), dk)
    chunk_p = RET_CHUNK if t % RET_CHUNK == 0 else t
    consts_p = _decay_consts(chunk_p, heads, dk, dv)
    consts_s = _decay_consts(ts, heads, dk, dv)

    new_pool_p, new_pool_s, new_ret_p, new_ret_s, new_conv_p, new_conv_s = [], [], [], [], [], []
    for i in range(depth):
        jm = i // 2
        if i % 2 == 0:
            w = pool_w[jm]
            g = row2(norm_mix[i])
            sc = row2(pool_scale[jm])
            xp, hl = _pool_mixer(xp, None, g, w, sc, shift=1, pos0=0, rows_per_seq=t)
            new_pool_p.append(hl[:, -POOL_BUF:, :])
            xs3 = x_sample if i == 0 else b_major(xs)
            xs, ns = _pool_mixer(xs3, t_major(state_pool[jm]), g, w, sc, shift=bs, pos0=PAST_LEN,
                                 rows_per_seq=ts * bs)
            new_pool_s.append(ns)
        else:
            w_in = ret_w_in[jm]
            w_out = ret_w_out[jm].astype(BF16)
            g = row2(norm_mix[i])
            proj_p = _ret_in(xp, g, w_in, tables_p, qk_width=qk, dk=dk, rows_per_seq=t, out_dtype=BF16)
            a_p, s_p = _ret_prompt(proj_p, consts_p, batch=b, seq=t, heads=heads, dk=dk, dv=dv)
            xp = _ret_out(xp, a_p, w_out, head_major=True)
            new_ret_p.append(s_p)
            proj_s = _ret_in(xs, g, w_in, tables_s, qk_width=qk, dk=dk, rows_per_seq=ts * bs, out_dtype=F32,
                             steps=ts)
            a_s, s_s = _ret_sample(proj_s, state_ret[jm], consts_s, heads=heads, dk=dk, dv=dv)
            xs = _ret_out(xs, a_s, w_out, head_major=False)
            new_ret_s.append(s_s)

        g = row2(norm_ffn[i])
        cb = row2(ffn_conv_b[i])
        xp, tg, tu = _conv_ffn(xp, g, None, ffn_w_up, ffn_conv_w, cb, ffn_w_down, i, shift=1, rows_per_seq=t)
        tiles_per_seq = tg.shape[0] // b
        last = jnp.concatenate([tg, tu], axis=-1)[tiles_per_seq - 1::tiles_per_seq]
        new_conv_p.append(last[:, -(CONV_W - 1):, :])
        xs, tg, tu = _conv_ffn(xs, g, state_conv, ffn_w_up, ffn_conv_w, cb, ffn_w_down, i, shift=bs,
                               rows_per_seq=ts * bs)
        new_conv_s.append(jnp.concatenate([tg, tu], axis=-1))

        g = row2(norm_ple[i])
        last_layer = i == depth - 1
        g_final = row2(norm_final) if last_layer else None
        xp = _ple(xp, pp, g, ple_w_gate, ple_w_proj, i, g_final)
        xs = _ple(xs, ps, g, ple_w_gate, ple_w_proj, i, g_final, steps=ts if last_layer else None)

    return (xp.reshape(b, t, d), xs,
            jnp.stack(new_pool_p), jnp.stack(new_pool_s),
            jnp.stack(new_ret_p), jnp.stack(new_ret_s),
            jnp.stack(new_conv_p), jnp.stack(new_conv_s))
```

```python
import functools

import jax
import jax.numpy as jnp
import numpy as np
from jax import lax
from jax.experimental import pallas as pl
from jax.experimental.pallas import tpu as pltpu

F32 = jnp.float32
BF16 = jnp.bfloat16

EPS = 1e-6
POOL_WINDOWS = (2, 4, 8, 16)
POOL_BUF = max(POOL_WINDOWS) - 1
RET_HEADS = 8
RET_CHUNK = 256
ROPE_BASE = 10000.0
CONV_W = 3
PAST_LEN = 16384

TOKEN_TILE = 512
WIDE_TILE = 1024
PROJ_TILE = 2048
FFN_CHUNK = 512
ROW_BLOCK = 512
PROJ_CHUNK = 1024
SAMPLE_SEQS = 2
SUBLANES = 8
VMEM_LIMIT = 60 * 1024 * 1024


def _params(n_axes):
    return pltpu.CompilerParams(
        dimension_semantics=("arbitrary",) * n_axes, vmem_limit_bytes=VMEM_LIMIT)


def _rmsnorm(x, g):
    return x * lax.rsqrt(jnp.mean(x * x, axis=-1, keepdims=True) + EPS) * g


def _shift_rows(a, rows):
    return pltpu.roll(a, rows, axis=0)


def _tile_fetch(x_hbm, x_ref, sem):
    tm = x_ref.shape[0]
    i = pl.program_id(0)
    f = pl.program_id(1)

    def copy(tile):
        return pltpu.make_async_copy(x_hbm.at[pl.ds(tile * tm, tm), :], x_ref, sem)

    @pl.when(jnp.logical_and(f == 0, i == 0))
    def _():
        copy(0).start()

    @pl.when(jnp.logical_and(f == 1, i + 1 < pl.num_programs(0)))
    def _():
        copy(i + 1).start()

    return lambda: copy(i).wait()


def _time_major(ref, cols=slice(None)):
    return jnp.concatenate([ref[:, t, cols] for t in range(ref.shape[1])], axis=0)


def _store_batch_major(ref, val, cols=slice(None)):
    nb = ref.shape[0]
    for t in range(ref.shape[1]):
        ref[:, t, cols] = val[t * nb:(t + 1) * nb, :]


def _pool_group(h, prev, x, w, sc, pos, win, shift):
    halo = prev.shape[0]
    a = jnp.concatenate([prev, h], axis=0)
    span = 1
    while span < win:
        a = a + _shift_rows(a, span * shift)
        span *= 2
    cnt = jnp.minimum(pos + 1, win).astype(F32)
    dlt = a[halo:, :] / cnt - h
    return x + jnp.dot(dlt.astype(BF16), w, preferred_element_type=F32) * sc


def _pool_prompt_kernel(x_ref, g_ref, w_ref, sc_ref, o_ref, h_ref, carry_ref, *, tiles_per_seq):
    tm, d = x_ref.shape
    gw = d // len(POOL_WINDOWS)
    halo = carry_ref.shape[0]
    j = pl.program_id(0) % tiles_per_seq
    x = x_ref[...]
    h = _rmsnorm(x, g_ref[...])
    last = h[tm - halo:, :]
    h_ref[...] = last
    pos = j * tm + lax.broadcasted_iota(jnp.int32, (tm, 1), 0)
    for gi, win in enumerate(POOL_WINDOWS):
        cols = slice(gi * gw, (gi + 1) * gw)
        prev = jnp.where(j == 0, 0.0, carry_ref[:, cols])
        o_ref[:, cols] = _pool_group(h[:, cols], prev, x[:, cols], w_ref[gi], sc_ref[:, cols], pos, win, 1)
    carry_ref[...] = last


def _pool_state_kernel(x_ref, state_ref, g_ref, w_ref, sc_ref, o_ref, ns_ref, hs_ref, *, pos0):
    shift, steps, d = x_ref.shape
    tm = shift * steps
    gw = o_ref.shape[1]
    kept = ns_ref.shape[1]
    grp = pl.program_id(0)

    @pl.when(grp == 0)
    def _():
        hs_ref[...] = _rmsnorm(_time_major(x_ref), g_ref[...])

    row = lax.broadcasted_iota(jnp.int32, (tm, 1), 0)
    step = jnp.zeros_like(row)
    for k in range(1, steps):
        step = step + (row >= k * shift).astype(jnp.int32)
    pos = pos0 + step

    for gi, win in enumerate(POOL_WINDOWS):
        @pl.when(grp == gi)
        def _(gi=gi, win=win):
            cols = slice(gi * gw, (gi + 1) * gw)
            h = hs_ref[:, cols]
            prev = jnp.concatenate([jnp.zeros((shift, gw), F32), state_ref[...]], axis=0)
            o_ref[...] = _pool_group(h, prev, _time_major(x_ref, cols), w_ref[...], sc_ref[...],
                                     pos, win, shift)
            ext = jnp.concatenate([prev, h], axis=0)
            _store_batch_major(ns_ref, ext[ext.shape[0] - kept * shift:, :])


def _pool_mixer(x, halo, g, w, scale, *, shift, pos0, rows_per_seq):
    d = x.shape[-1]
    ng = len(POOL_WINDOWS)
    gw = d // ng
    if halo is None:
        n = x.shape[0]
        tm = min(WIDE_TILE, n)
        assert shift == 1 and pos0 == 0
        tiles_per_seq = rows_per_seq // tm
        halo_rows = POOL_BUF + 1
        full = lambda a: pl.BlockSpec(a.shape, lambda i: (0,) * a.ndim)
        tile = pl.BlockSpec((tm, d), lambda i: (i, 0))
        return pl.pallas_call(
            functools.partial(_pool_prompt_kernel, tiles_per_seq=tiles_per_seq),
            grid=(n // tm,),
            in_specs=[tile, full(g), full(w), full(scale)],
            out_specs=[tile, pl.BlockSpec((None, halo_rows, d), lambda i: (i // tiles_per_seq, 0, 0))],
            out_shape=[jax.ShapeDtypeStruct((n, d), F32),
                       jax.ShapeDtypeStruct((n // rows_per_seq, halo_rows, d), F32)],
            scratch_shapes=[pltpu.VMEM((halo_rows, d), F32)],
            compiler_params=_params(1),
            name="pool_mixer",
        )(x, g, w, scale)
    nb, steps, _ = x.shape
    kept = halo.shape[0] // nb
    assert shift == nb and kept == POOL_BUF
    return pl.pallas_call(
        functools.partial(_pool_state_kernel, pos0=pos0),
        grid=(ng,),
        in_specs=[pl.BlockSpec((nb, steps, d), lambda gi: (0, 0, 0)),
                  pl.BlockSpec((kept * nb, gw), lambda gi: (0, gi)),
                  pl.BlockSpec((1, d), lambda gi: (0, 0)),
                  pl.BlockSpec((None, gw, gw), lambda gi: (gi, 0, 0)),
                  pl.BlockSpec((1, gw), lambda gi: (0, gi))],
        out_specs=[pl.BlockSpec((nb * steps, gw), lambda gi: (0, gi)),
                   pl.BlockSpec((nb, kept, gw), lambda gi: (0, 0, gi))],
        out_shape=[jax.ShapeDtypeStruct((nb * steps, d), F32), jax.ShapeDtypeStruct((nb, kept, d), F32)],
        scratch_shapes=[pltpu.VMEM((nb * steps, d), F32)],
        compiler_params=_params(1),
        name="pool_mixer_state",
    )(x, halo, g, w, scale)


def _ffn_kernel(*refs, shift, tiles_per_seq, from_state):
    if from_state:
        (x_ref, g_ref, sg_ref, su_ref, wg_ref, wu_ref, cwg_ref, cwu_ref, cbg_ref, cbu_ref, wd_ref,
         o_ref, tg_ref, tu_ref, hs_ref) = refs
        carry_g = carry_u = None
    else:
        (x_hbm, g_ref, wg_ref, wu_ref, cwg_ref, cwu_ref, cbg_ref, cbu_ref, wd_ref,
         o_ref, tg_ref, tu_ref, hs_ref, carry_g, carry_u, x_ref, x_sem) = refs
        sg_ref = su_ref = None
    tm, d = x_ref.shape
    halo = max((CONV_W - 1) * shift, SUBLANES)
    i = pl.program_id(0)
    f = pl.program_id(1)
    first_of_seq = (i % tiles_per_seq) == 0

    wait_x = None if from_state else _tile_fetch(x_hbm, x_ref, x_sem)

    @pl.when(f == 0)
    def _():
        if wait_x is not None:
            wait_x()
        x = x_ref[...]
        hs_ref[...] = _rmsnorm(x, g_ref[...]).astype(BF16)
        o_ref[...] = x

    rb = min(ROW_BLOCK, tm)
    cwg, cwu, cbg, cbu = cwg_ref[...], cwu_ref[...], cbg_ref[...], cbu_ref[...]
    if from_state:
        assert rb == tm
        prev_g, prev_u = _time_major(sg_ref), _time_major(su_ref)
    else:
        prev_g = jnp.where(first_of_seq, 0.0, carry_g[f])
        prev_u = jnp.where(first_of_seq, 0.0, carry_u[f])

    def up(r):
        hs = hs_ref[r:r + rb, :]
        return (jnp.dot(hs, wg_ref[...], preferred_element_type=F32),
                jnp.dot(hs, wu_ref[...], preferred_element_type=F32))

    def conv(prev, u, cw, cb):
        ext = jnp.concatenate([prev, u], axis=0)
        if shift % SUBLANES == 0:
            u1 = ext[halo - shift:halo - shift + rb, :]
            u2 = ext[halo - 2 * shift:halo - 2 * shift + rb, :]
        else:
            u1 = _shift_rows(ext, shift)[halo:, :]
            u2 = _shift_rows(ext, 2 * shift)[halo:, :]
        return cw[0:1, :] * u2 + cw[1:2, :] * u1 + cw[2:3, :] * u + cb

    nxt = up(0)
    for r in range(0, tm, rb):
        ug, uu = nxt
        if r + rb < tm:
            nxt = up(r + rb)
        cg = conv(prev_g, ug, cwg, cbg)
        cu = conv(prev_u, uu, cwu, cbu)
        prev_g, prev_u = ug[rb - halo:, :], uu[rb - halo:, :]
        act = ((cg / (1.0 + jnp.exp(-cg))) * cu).astype(BF16)
        o_ref[r:r + rb, :] += jnp.dot(act, wd_ref[...], preferred_element_type=F32)
    if from_state:
        _store_batch_major(tg_ref, prev_g)
        _store_batch_major(tu_ref, prev_u)
    else:
        carry_g[f] = prev_g
        carry_u[f] = prev_u
        tg_ref[...] = prev_g
        tu_ref[...] = prev_u


def _conv_ffn(x, g, state, w_up, cw, cb, w_down, layer, *, shift, rows_per_seq):
    n, d = x.shape
    f2 = w_up.shape[2]
    ffn = f2 // 2
    tm = min(WIDE_TILE, n)
    fn = FFN_CHUNK
    nf = ffn // fn
    n_tiles = n // tm
    tiles_per_seq = max(rows_per_seq // tm, 1)
    halo = max((CONV_W - 1) * shift, SUBLANES)
    from_state = state is not None

    tile = pl.BlockSpec((tm, d), lambda i, f: (i, 0))
    tile_in = tile if from_state else pl.BlockSpec(memory_space=pl.ANY)
    gate_cols = lambda rows: pl.BlockSpec((rows, fn), lambda i, f: (0, f))
    up_cols = lambda rows: pl.BlockSpec((rows, fn), lambda i, f: (0, f + nf))
    layer_gate = lambda rows: pl.BlockSpec((None, rows, fn), lambda i, f: (layer, 0, f))
    layer_up = lambda rows: pl.BlockSpec((None, rows, fn), lambda i, f: (layer, 0, f + nf))
    in_specs = [tile_in, pl.BlockSpec((1, d), lambda i, f: (0, 0))]
    args = [x, g]
    scratch = [pltpu.VMEM((tm, d), BF16)]
    if from_state:
        nb, steps = state.shape[1], state.shape[2]
        assert n == tm and nb == shift and halo == steps * shift
        in_specs += [pl.BlockSpec((None, nb, steps, fn), lambda i, f: (layer, 0, 0, f)),
                     pl.BlockSpec((None, nb, steps, fn), lambda i, f: (layer, 0, 0, f + nf))]
        args += [state, state]
        tail_specs = [pl.BlockSpec((nb, steps, fn), lambda i, f: (0, 0, f))] * 2
        tail_shapes = [jax.ShapeDtypeStruct((nb, steps, ffn), F32)] * 2
    else:
        tail_specs = [pl.BlockSpec((None, halo, fn), lambda i, f: (i, 0, f))] * 2
        tail_shapes = [jax.ShapeDtypeStruct((n_tiles, halo, ffn), F32)] * 2
        scratch += [pltpu.VMEM((nf, halo, fn), F32), pltpu.VMEM((nf, halo, fn), F32),
                    pltpu.VMEM((tm, d), F32), pltpu.SemaphoreType.DMA(())]
    in_specs += [layer_gate(d), layer_up(d), layer_gate(CONV_W), layer_up(CONV_W), gate_cols(1), up_cols(1),
                 pl.BlockSpec((None, fn, d), lambda i, f: (layer, f, 0))]
    args += [w_up, w_up, cw, cw, cb, cb, w_down]
    return pl.pallas_call(
        functools.partial(_ffn_kernel, shift=shift, tiles_per_seq=tiles_per_seq, from_state=from_state),
        grid=(n_tiles, nf),
        in_specs=in_specs,
        out_specs=[tile] + tail_specs,
        out_shape=[jax.ShapeDtypeStruct((n, d), F32)] + tail_shapes,
        scratch_shapes=scratch,
        compiler_params=_params(2),
        name="conv_ffn_state" if from_state else "conv_ffn",
    )(*args)


def _ple_kernel(*refs, final):
    if final:
        x_ref, p_ref, g_ref, wg_ref, wp_ref, gf_ref, o_ref = refs
    else:
        x_ref, p_ref, g_ref, wg_ref, wp_ref, o_ref = refs
    tm = x_ref.shape[0]
    rb = min(ROW_BLOCK, tm)

    def project(r):
        hn = _rmsnorm(x_ref[r:r + rb, :], g_ref[...]).astype(BF16)
        return (jnp.dot(hn, wg_ref[...], preferred_element_type=F32),
                jnp.dot(p_ref[r:r + rb, :].astype(BF16), wp_ref[...], preferred_element_type=F32))

    nxt = project(0)
    for r in range(0, tm, rb):
        logits, emb = nxt
        if r + rb < tm:
            nxt = project(r + rb)
        y = x_ref[r:r + rb, :] + jax.nn.sigmoid(logits) * emb
        if final:
            y = _rmsnorm(y, gf_ref[...])
        if len(o_ref.shape) == 3:
            assert rb == tm
            _store_batch_major(o_ref, y)
        else:
            o_ref[r:r + rb, :] = y


def _ple(x, p, g, w_gate, w_proj, layer, g_final, steps=None):
    n, d = x.shape
    tm = min(TOKEN_TILE, n)
    final = g_final is not None
    if steps is None:
        out_spec = pl.BlockSpec((tm, d), lambda i: (i, 0))
        out_shape = jax.ShapeDtypeStruct((n, d), F32)
    else:
        assert n == tm
        out_spec = pl.BlockSpec((n // steps, steps, d), lambda i: (0, 0, 0))
        out_shape = jax.ShapeDtypeStruct((n // steps, steps, d), F32)
    const = lambda a: pl.BlockSpec(a.shape, lambda i: (0, 0))
    resident = lambda a: pl.BlockSpec((None,) + a.shape[1:], lambda i: (layer, 0, 0),
                                      pipeline_mode=pl.Buffered(1))
    in_specs = [pl.BlockSpec((tm, d), lambda i: (i, 0)),
                pl.BlockSpec((None, tm, p.shape[2]), lambda i: (layer, i, 0)),
                const(g), resident(w_gate), resident(w_proj)]
    args = [x, p, g, w_gate, w_proj]
    if final:
        in_specs.append(const(g_final))
        args.append(g_final)
    return pl.pallas_call(
        functools.partial(_ple_kernel, final=final),
        grid=(n // tm,),
        in_specs=in_specs,
        out_specs=out_spec,
        out_shape=out_shape,
        compiler_params=_params(1),
        name="ple_final" if final else "ple",
    )(*args)


def _ret_in_kernel(x_in, g_ref, w_ref, cos_ref, sin_ref, o_ref, hs_ref, *fetch, dk, layout):
    if fetch:
        x_ref = fetch[0]
        wait_x = _tile_fetch(x_in, *fetch)
    else:
        x_ref, wait_x = x_in, None

    @pl.when(pl.program_id(1) == 0)
    def _():
        if wait_x is not None:
            wait_x()
        hs_ref[...] = _rmsnorm(x_ref[...], g_ref[...]).astype(BF16)

    tm = hs_ref.shape[0]
    rb = min(ROW_BLOCK, tm)
    half = dk // 2
    project = lambda r: jnp.dot(hs_ref[r:r + rb, :], w_ref[...], preferred_element_type=F32)
    nxt = project(0)
    for r in range(0, tm, rb):
        proj = nxt
        if r + rb < tm:
            nxt = project(r + rb)
        cos = cos_ref[r:r + rb, :]
        sin = sin_ref[r:r + rb, :]
        for hh in range(proj.shape[1] // dk):
            x1 = proj[:, hh * dk:hh * dk + half]
            x2 = proj[:, hh * dk + half:(hh + 1) * dk]
            y1 = (x1 * cos - x2 * sin).astype(o_ref.dtype)
            y2 = (x1 * sin + x2 * cos).astype(o_ref.dtype)
            if layout == "heads":
                o_ref[hh, r:r + rb, 0:half] = y1
                o_ref[hh, r:r + rb, half:dk] = y2
            else:
                assert rb == tm
                _store_batch_major(o_ref, y1, slice(hh * dk, hh * dk + half))
                _store_batch_major(o_ref, y2, slice(hh * dk + half, (hh + 1) * dk))


def _ret_in(x, g, w_in, tables, *, qk_width, dk, rows_per_seq, out_dtype, steps=None):
    n, d = x.shape
    e = w_in.shape[1]
    tm = min(PROJ_TILE, n)
    nc = PROJ_CHUNK
    qc = qk_width // nc
    tps = rows_per_seq // tm

    def table_block(i, c):
        return (jnp.where(c < qc, 0, jnp.where(c < 2 * qc, tps, 2 * tps)) + i % tps, 0)

    scratch = [pltpu.VMEM((tm, d), BF16)]
    if steps is None:
        x_spec = pl.BlockSpec(memory_space=pl.ANY)
        w_spec = pl.BlockSpec((d, nc), lambda i, c: (0, c))
        out_spec = pl.BlockSpec((nc // dk, tm, dk), lambda i, c: (c, i, 0))
        out_shape = jax.ShapeDtypeStruct((e // dk, n, dk), out_dtype)
        scratch += [pltpu.VMEM((tm, d), F32), pltpu.SemaphoreType.DMA(())]
    else:
        assert n == tm
        x_spec = pl.BlockSpec((tm, d), lambda i, c: (i, 0))
        w_spec = pl.BlockSpec((d, nc), lambda i, c: (0, c))
        out_spec = pl.BlockSpec((n // steps, steps, nc), lambda i, c: (0, 0, c))
        out_shape = jax.ShapeDtypeStruct((n // steps, steps, e), out_dtype)

    return pl.pallas_call(
        functools.partial(_ret_in_kernel, dk=dk, layout="heads" if steps is None else "batch"),
        grid=(n // tm, e // nc),
        in_specs=[x_spec,
                  pl.BlockSpec((1, d), lambda i, c: (0, 0)),
                  w_spec,
                  pl.BlockSpec((tm, dk // 2), table_block),
                  pl.BlockSpec((tm, dk // 2), table_block)],
        out_specs=out_spec,
        out_shape=out_shape,
        scratch_shapes=scratch,
        compiler_params=_params(2),
        name="ret_in",
    )(x, g, w_in, *tables)


def _groupnorm_gate(o, gate):
    mu = jnp.mean(o, axis=-1, keepdims=True)
    var = jnp.mean(jnp.square(o - mu), axis=-1, keepdims=True)
    on = (o - mu) * lax.rsqrt(var + EPS)
    gf = gate.astype(F32)
    return (gf / (1.0 + jnp.exp(-gf))) * on


def _ret_prompt_kernel(q_ref, k_ref, v_ref, g_ref, intra_ref, cross_ref, kdec_ref, sdec_ref,
                       o_ref, s_out_ref, s_ref, *, chunk):
    t = q_ref.shape[0]
    s_ref[...] = jnp.zeros(s_ref.shape, F32)
    intra = intra_ref[...]
    cross = cross_ref[...]
    kdec = kdec_ref[...]
    sdec = sdec_ref[...]

    wide = lambda ref, rows: jnp.concatenate([ref[i, rows, :] for i in range(ref.shape[0])], axis=-1)

    def body(c, carry):
        rows = pl.ds(pl.multiple_of(c * chunk, chunk), chunk)
        q = q_ref[rows, :]
        k = k_ref[rows, :]
        v = wide(v_ref, rows)
        s = s_ref[...]
        sc = lax.dot_general(q, k, (((1,), (1,)), ((), ())), preferred_element_type=F32) * intra
        o = (jnp.dot(sc.astype(BF16), v, preferred_element_type=F32)
             + jnp.dot(q, s.astype(BF16), preferred_element_type=F32) * cross)
        kd = (k.astype(F32) * kdec).astype(BF16)
        s_ref[...] = sdec * s + lax.dot_general(kd, v, (((0,), (0,)), ((), ())),
                                                preferred_element_type=F32)
        o_ref[rows, :] = _groupnorm_gate(o, wide(g_ref, rows)).astype(o_ref.dtype)
        return carry

    lax.fori_loop(0, t // chunk, body, 0, unroll=4)
    s_out_ref[...] = s_ref[...]


def _ret_prompt(proj, consts, *, batch, seq, heads, dk, dv):
    intra, cross, kdec, sdec = consts
    chunk = intra.shape[1]
    per = dv // dk
    v_off = 2 * heads // per
    g_off = v_off + heads
    head_const = lambda a: pl.BlockSpec((None,) + a.shape[1:], lambda b, h: (h, 0, 0))
    return pl.pallas_call(
        functools.partial(_ret_prompt_kernel, chunk=chunk),
        grid=(batch, heads),
        in_specs=[pl.BlockSpec((None, seq, dk), lambda b, h: (h, b, 0)),
                  pl.BlockSpec((None, seq, dk), lambda b, h: (heads + h, b, 0)),
                  pl.BlockSpec((per, seq, dk), lambda b, h: (v_off + h, b, 0)),
                  pl.BlockSpec((per, seq, dk), lambda b, h: (g_off + h, b, 0)),
                  head_const(intra), head_const(cross), head_const(kdec), head_const(sdec)],
        out_specs=[pl.BlockSpec((None, seq, dv), lambda b, h: (h, b, 0)),
                   pl.BlockSpec((None, None, dk, dv), lambda b, h: (b, h, 0, 0))],
        out_shape=[jax.ShapeDtypeStruct((heads, batch * seq, dv), BF16),
                   jax.ShapeDtypeStruct((batch, heads, dk, dv), F32)],
        scratch_shapes=[pltpu.VMEM((dk, dv), F32)],
        compiler_params=_params(2),
        name="retention_prompt",
    )(proj, proj, proj, proj, intra, cross, kdec, sdec)


def _ret_sample_kernel(p_ref, s_ref, intra_ref, cross_ref, kdec_ref, sdec_ref, o_ref, s_out_ref,
                       *, heads, dk, dv):
    qk = heads * dk
    for n in range(p_ref.shape[0]):
        for h in range(heads):
            q = p_ref[n, :, h * dk:(h + 1) * dk].astype(BF16)
            kf = p_ref[n, :, qk + h * dk:qk + (h + 1) * dk]
            v = p_ref[n, :, 2 * qk + h * dv:2 * qk + (h + 1) * dv].astype(BF16)
            gate = p_ref[n, :, 2 * qk + heads * dv + h * dv:2 * qk + heads * dv + (h + 1) * dv]
            s = s_ref[n, h]
            sc = lax.dot_general(q, kf.astype(BF16), (((1,), (1,)), ((), ())),
                                 preferred_element_type=F32) * intra_ref[h]
            o = (jnp.dot(sc.astype(BF16), v, preferred_element_type=F32)
                 + jnp.dot(q, s.astype(BF16), preferred_element_type=F32) * cross_ref[h])
            kd = (kf * kdec_ref[h]).astype(BF16)
            s_out_ref[n, h] = sdec_ref[h] * s + lax.dot_general(kd, v, (((0,), (0,)), ((), ())),
                                                                preferred_element_type=F32)
            o_ref[n, :, h * dv:(h + 1) * dv] = _groupnorm_gate(o, gate).astype(o_ref.dtype)


def _ret_sample(proj, state, consts, *, heads, dk, dv):
    intra, cross, kdec, sdec = consts
    b, tp, e = proj.shape
    nb = SAMPLE_SEQS
    const = lambda a: pl.BlockSpec(a.shape, lambda i: (0,) * a.ndim)
    return pl.pallas_call(
        functools.partial(_ret_sample_kernel, heads=heads, dk=dk, dv=dv),
        grid=(b // nb,),
        in_specs=[pl.BlockSpec((nb, tp, e), lambda i: (i, 0, 0)),
                  pl.BlockSpec((nb, heads, dk, dv), lambda i: (i, 0, 0, 0)),
                  const(intra), const(cross), const(kdec), const(sdec)],
        out_specs=[pl.BlockSpec((nb, tp, heads * dv), lambda i: (i, 0, 0)),
                   pl.BlockSpec((nb, heads, dk, dv), lambda i: (i, 0, 0, 0))],
        out_shape=[jax.ShapeDtypeStruct((b, tp, heads * dv), F32),
                   jax.ShapeDtypeStruct((b, heads, dk, dv), F32)],
        compiler_params=_params(1),
        name="retention_sample",
    )(proj, state, intra, cross, kdec, sdec)


def _ret_out_kernel(x_ref, a_ref, w_ref, o_ref, *, head_major):
    if head_major:
        dv = a_ref.shape[2]
        y = x_ref[...]
        for h in range(a_ref.shape[0]):
            y = y + jnp.dot(a_ref[h], w_ref[h * dv:(h + 1) * dv, :], preferred_element_type=F32)
        o_ref[...] = y
    else:
        a = _time_major(a_ref).astype(BF16)
        o_ref[...] = x_ref[...] + jnp.dot(a, w_ref[...], preferred_element_type=F32)


def _ret_out(x, a, w_out, *, head_major):
    n, d = x.shape
    tm = min(TOKEN_TILE, n)
    if head_major:
        a_spec = pl.BlockSpec((a.shape[0], tm, a.shape[2]), lambda i: (0, i, 0))
    else:
        assert n == tm
        a_spec = pl.BlockSpec(a.shape, lambda i: (0, 0, 0))
    v = w_out.shape[0]
    return pl.pallas_call(
        functools.partial(_ret_out_kernel, head_major=head_major),
        grid=(n // tm,),
        in_specs=[pl.BlockSpec((tm, d), lambda i: (i, 0)),
                  a_spec,
                  pl.BlockSpec((v, d), lambda i: (0, 0), pipeline_mode=pl.Buffered(1))],
        out_specs=pl.BlockSpec((tm, d), lambda i: (i, 0)),
        out_shape=jax.ShapeDtypeStruct((n, d), F32),
        compiler_params=_params(1),
        name="ret_out",
    )(x, a, w_out)


def _decay_consts(chunk, heads, dk, dv):
    log_g = np.log1p(-(2.0 ** (-5.0 - np.arange(heads, dtype=np.float64))))
    n = np.arange(chunk, dtype=np.float64)
    diff = n[:, None] - n[None, :]
    intra = np.where(diff >= 0, np.exp(log_g[:, None, None] * np.maximum(diff, 0.0)), 0.0)
    cross = np.exp(log_g[:, None] * (n + 1.0))
    kdec = np.exp(log_g[:, None] * (chunk - 1.0 - n))
    sdec = np.exp(log_g * chunk)
    f32 = lambda a: jnp.asarray(np.ascontiguousarray(a), dtype=F32)
    return (f32(intra),
            f32(np.broadcast_to(cross[:, :, None], (heads, chunk, dv))),
            f32(np.broadcast_to(kdec[:, :, None], (heads, chunk, dk))),
            f32(np.broadcast_to(sdec[:, None, None], (heads, 1, dv))))


def _rotary_tables(pos, dk):
    half = dk // 2
    inv = ROPE_BASE ** (-np.arange(half, dtype=np.float64) / half)
    ang = np.asarray(pos, dtype=np.float64)[:, None] * inv[None, :]
    cos, sin = np.cos(ang), np.sin(ang)
    scale = dk ** -0.5
    f32 = lambda a: jnp.asarray(a, dtype=F32)
    return (f32(np.concatenate([cos, cos * scale, np.ones_like(cos)])),
            f32(np.concatenate([sin, sin * scale, np.zeros_like(sin)])))


def kernel(x_prompt, x_sample, p_prompt, p_sample, state_pool, state_ret, state_conv, norm_mix, norm_ffn,
           norm_ple, norm_final, pool_w, pool_scale, ret_w_in, ret_w_out, ffn_w_up, ffn_conv_w,
           ffn_conv_b, ffn_w_down, ple_w_proj, ple_w_gate):
    b, t, d = x_prompt.shape
    bs, ts, _ = x_sample.shape
    depth = norm_mix.shape[0]
    heads = RET_HEADS
    dk = d // heads
    dv = 2 * dk
    qk = heads * dk
    f2 = ffn_w_up.shape[2]
    ffn = f2 // 2
    assert ts >= CONV_W - 1 and PAST_LEN >= POOL_BUF

    row2 = lambda a: a.reshape(1, -1)
    t_major = lambda a: jnp.swapaxes(a, 0, 1).reshape(-1, a.shape[-1])
    b_major = lambda a: jnp.swapaxes(a.reshape(-1, bs, a.shape[-1]), 0, 1)

    xp = x_prompt.reshape(b * t, d)
    xs = t_major(x_sample)
    pp = p_prompt.reshape(depth, b * t, -1)
    ps = jnp.swapaxes(p_sample, 1, 2).reshape(depth, ts * bs, -1)

    tables_p = _rotary_tables(np.arange(t), dk)
    tables_s = _rotary_tables(PAST_LEN + np.repeat(np.arange(ts), bs), dk)
    chunk_p = RET_CHUNK if t % RET_CHUNK == 0 else t
    consts_p = _decay_consts(chunk_p, heads, dk, dv)
    consts_s = _decay_consts(ts, heads, dk, dv)

    new_pool_p, new_pool_s, new_ret_p, new_ret_s, new_conv_p, new_conv_s = [], [], [], [], [], []
    for i in range(depth):
        jm = i // 2
        if i % 2 == 0:
            w = pool_w[jm]
            g = row2(norm_mix[i])
            sc = row2(pool_scale[jm])
            xp, hl = _pool_mixer(xp, None, g, w, sc, shift=1, pos0=0, rows_per_seq=t)
            new_pool_p.append(hl[:, -POOL_BUF:, :])
            xs3 = x_sample if i == 0 else b_major(xs)
            xs, ns = _pool_mixer(xs3, t_major(state_pool[jm]), g, w, sc, shift=bs, pos0=PAST_LEN,
                                 rows_per_seq=ts * bs)
            new_pool_s.append(ns)
        else:
            w_in = ret_w_in[jm]
            w_out = ret_w_out[jm].astype(BF16)
            g = row2(norm_mix[i])
            proj_p = _ret_in(xp, g, w_in, tables_p, qk_width=qk, dk=dk, rows_per_seq=t, out_dtype=BF16)
            a_p, s_p = _ret_prompt(proj_p, consts_p, batch=b, seq=t, heads=heads, dk=dk, dv=dv)
            xp = _ret_out(xp, a_p, w_out, head_major=True)
            new_ret_p.append(s_p)
            proj_s = _ret_in(xs, g, w_in, tables_s, qk_width=qk, dk=dk, rows_per_seq=ts * bs, out_dtype=F32,
                             steps=ts)
            a_s, s_s = _ret_sample(proj_s, state_ret[jm], consts_s, heads=heads, dk=dk, dv=dv)
            xs = _ret_out(xs, a_s, w_out, head_major=False)
            new_ret_s.append(s_s)

        g = row2(norm_ffn[i])
        cb = row2(ffn_conv_b[i])
        xp, tg, tu = _conv_ffn(xp, g, None, ffn_w_up, ffn_conv_w, cb, ffn_w_down, i, shift=1, rows_per_seq=t)
        tiles_per_seq = tg.shape[0] // b
        last = jnp.concatenate([tg, tu], axis=-1)[tiles_per_seq - 1::tiles_per_seq]
        new_conv_p.append(last[:, -(CONV_W - 1):, :])
        xs, tg, tu = _conv_ffn(xs, g, state_conv, ffn_w_up, ffn_conv_w, cb, ffn_w_down, i, shift=bs,
                               rows_per_seq=ts * bs)
        new_conv_s.append(jnp.concatenate([tg, tu], axis=-1))

        g = row2(norm_ple[i])
        last_layer = i == depth - 1
        g_final = row2(norm_final) if last_layer else None
        xp = _ple(xp, pp, g, ple_w_gate, ple_w_proj, i, g_final)
        xs = _ple(xs, ps, g, ple_w_gate, ple_w_proj, i, g_final, steps=ts if last_layer else None)

    return (xp.reshape(b, t, d), xs,
            jnp.stack(new_pool_p), jnp.stack(new_pool_s),
            jnp.stack(new_ret_p), jnp.stack(new_ret_s),
            jnp.stack(new_conv_p), jnp.stack(new_conv_s))
```

```python
import functools

import jax
import jax.numpy as jnp
import numpy as np
from jax import lax
from jax.experimental import pallas as pl
from jax.experimental.pallas import tpu as pltpu

F32 = jnp.float32
BF16 = jnp.bfloat16

EPS = 1e-6
POOL_WINDOWS = (2, 4, 8, 16)
POOL_BUF = max(POOL_WINDOWS) - 1
RET_HEADS = 8
RET_CHUNK = 256
ROPE_BASE = 10000.0
CONV_W = 3
PAST_LEN = 16384

TOKEN_TILE = 512
WIDE_TILE = 1024
PROJ_TILE = 2048
FFN_CHUNK = 512
ROW_BLOCK = 512
PROJ_CHUNK = 1024
RET_HEAD_GROUP = 2
SAMPLE_SEQS = 2
SUBLANES = 8
VMEM_LIMIT = 60 * 1024 * 1024


def _params(n_axes):
    return pltpu.CompilerParams(
        dimension_semantics=("arbitrary",) * n_axes, vmem_limit_bytes=VMEM_LIMIT)


def _rmsnorm(x, g):
    return x * lax.rsqrt(jnp.mean(x * x, axis=-1, keepdims=True) + EPS) * g


def _shift_rows(a, rows):
    return pltpu.roll(a, rows, axis=0)


def _tile_fetch(x_hbm, x_ref, sem):
    tm = x_ref.shape[0]
    i = pl.program_id(0)
    f = pl.program_id(1)

    def copy(tile):
        return pltpu.make_async_copy(x_hbm.at[pl.ds(tile * tm, tm), :], x_ref, sem)

    @pl.when(jnp.logical_and(f == 0, i == 0))
    def _():
        copy(0).start()

    @pl.when(jnp.logical_and(f == 1, i + 1 < pl.num_programs(0)))
    def _():
        copy(i + 1).start()

    return lambda: copy(i).wait()


def _time_major(ref, cols=slice(None)):
    return jnp.concatenate([ref[:, t, cols] for t in range(ref.shape[1])], axis=0)


def _store_batch_major(ref, val, cols=slice(None)):
    nb = ref.shape[0]
    for t in range(ref.shape[1]):
        ref[:, t, cols] = val[t * nb:(t + 1) * nb, :]


def _pool_group(h, prev, x, w, sc, pos, win, shift):
    halo = prev.shape[0]
    a = jnp.concatenate([prev, h], axis=0)
    span = 1
    while span < win:
        a = a + _shift_rows(a, span * shift)
        span *= 2
    cnt = jnp.minimum(pos + 1, win).astype(F32)
    dlt = a[halo:, :] / cnt - h
    return x + jnp.dot(dlt.astype(BF16), w, preferred_element_type=F32) * sc


def _pool_prompt_kernel(x_ref, g_ref, w_ref, sc_ref, o_ref, h_ref, carry_ref, *, tiles_per_seq):
    tm, d = x_ref.shape
    gw = d // len(POOL_WINDOWS)
    halo = carry_ref.shape[0]
    j = pl.program_id(0) % tiles_per_seq
    x = x_ref[...]
    h = _rmsnorm(x, g_ref[...])
    last = h[tm - halo:, :]
    h_ref[...] = last
    pos = j * tm + lax.broadcasted_iota(jnp.int32, (tm, 1), 0)
    for gi, win in enumerate(POOL_WINDOWS):
        cols = slice(gi * gw, (gi + 1) * gw)
        prev = jnp.where(j == 0, 0.0, carry_ref[:, cols])
        o_ref[:, cols] = _pool_group(h[:, cols], prev, x[:, cols], w_ref[gi], sc_ref[:, cols], pos, win, 1)
    carry_ref[...] = last


def _pool_state_kernel(x_ref, state_ref, g_ref, w_ref, sc_ref, o_ref, ns_ref, hs_ref, *, pos0):
    shift, steps, d = x_ref.shape
    tm = shift * steps
    gw = o_ref.shape[1]
    kept = ns_ref.shape[1]
    grp = pl.program_id(0)

    @pl.when(grp == 0)
    def _():
        hs_ref[...] = _rmsnorm(_time_major(x_ref), g_ref[...])

    row = lax.broadcasted_iota(jnp.int32, (tm, 1), 0)
    step = jnp.zeros_like(row)
    for k in range(1, steps):
        step = step + (row >= k * shift).astype(jnp.int32)
    pos = pos0 + step

    for gi, win in enumerate(POOL_WINDOWS):
        @pl.when(grp == gi)
        def _(gi=gi, win=win):
            cols = slice(gi * gw, (gi + 1) * gw)
            h = hs_ref[:, cols]
            prev = jnp.concatenate([jnp.zeros((shift, gw), F32), state_ref[...]], axis=0)
            o_ref[...] = _pool_group(h, prev, _time_major(x_ref, cols), w_ref[...], sc_ref[...],
                                     pos, win, shift)
            ext = jnp.concatenate([prev, h], axis=0)
            _store_batch_major(ns_ref, ext[ext.shape[0] - kept * shift:, :])


def _pool_mixer(x, halo, g, w, scale, *, shift, pos0, rows_per_seq):
    d = x.shape[-1]
    ng = len(POOL_WINDOWS)
    gw = d // ng
    if halo is None:
        n = x.shape[0]
        tm = min(WIDE_TILE, n)
        assert shift == 1 and pos0 == 0
        tiles_per_seq = rows_per_seq // tm
        halo_rows = POOL_BUF + 1
        full = lambda a: pl.BlockSpec(a.shape, lambda i: (0,) * a.ndim)
        tile = pl.BlockSpec((tm, d), lambda i: (i, 0))
        return pl.pallas_call(
            functools.partial(_pool_prompt_kernel, tiles_per_seq=tiles_per_seq),
            grid=(n // tm,),
            in_specs=[tile, full(g), full(w), full(scale)],
            out_specs=[tile, pl.BlockSpec((None, halo_rows, d), lambda i: (i // tiles_per_seq, 0, 0))],
            out_shape=[jax.ShapeDtypeStruct((n, d), F32),
                       jax.ShapeDtypeStruct((n // rows_per_seq, halo_rows, d), F32)],
            scratch_shapes=[pltpu.VMEM((halo_rows, d), F32)],
            compiler_params=_params(1),
            name="pool_mixer",
        )(x, g, w, scale)
    nb, steps, _ = x.shape
    kept = halo.shape[0] // nb
    assert shift == nb and kept == POOL_BUF
    return pl.pallas_call(
        functools.partial(_pool_state_kernel, pos0=pos0),
        grid=(ng,),
        in_specs=[pl.BlockSpec((nb, steps, d), lambda gi: (0, 0, 0)),
                  pl.BlockSpec((kept * nb, gw), lambda gi: (0, gi)),
                  pl.BlockSpec((1, d), lambda gi: (0, 0)),
                  pl.BlockSpec((None, gw, gw), lambda gi: (gi, 0, 0)),
                  pl.BlockSpec((1, gw), lambda gi: (0, gi))],
        out_specs=[pl.BlockSpec((nb * steps, gw), lambda gi: (0, gi)),
                   pl.BlockSpec((nb, kept, gw), lambda gi: (0, 0, gi))],
        out_shape=[jax.ShapeDtypeStruct((nb * steps, d), F32), jax.ShapeDtypeStruct((nb, kept, d), F32)],
        scratch_shapes=[pltpu.VMEM((nb * steps, d), F32)],
        compiler_params=_params(1),
        name="pool_mixer_state",
    )(x, halo, g, w, scale)


def _ffn_kernel(*refs, shift, tiles_per_seq, from_state):
    if from_state:
        (x_ref, g_ref, sg_ref, su_ref, wg_ref, wu_ref, cwg_ref, cwu_ref, cbg_ref, cbu_ref, wd_ref,
         o_ref, tg_ref, tu_ref, hs_ref) = refs
        carry_g = carry_u = None
    else:
        (x_hbm, g_ref, wg_ref, wu_ref, cwg_ref, cwu_ref, cbg_ref, cbu_ref, wd_ref,
         o_ref, tg_ref, tu_ref, hs_ref, carry_g, carry_u, x_ref, x_sem) = refs
        sg_ref = su_ref = None
    tm, d = x_ref.shape
    halo = max((CONV_W - 1) * shift, SUBLANES)
    i = pl.program_id(0)
    f = pl.program_id(1)
    first_of_seq = (i % tiles_per_seq) == 0

    wait_x = None if from_state else _tile_fetch(x_hbm, x_ref, x_sem)

    @pl.when(f == 0)
    def _():
        if wait_x is not None:
            wait_x()
        x = x_ref[...]
        hs_ref[...] = _rmsnorm(x, g_ref[...]).astype(BF16)
        o_ref[...] = x

    rb = min(ROW_BLOCK, tm)
    cwg, cwu, cbg, cbu = cwg_ref[...], cwu_ref[...], cbg_ref[...], cbu_ref[...]
    if from_state:
        assert rb == tm
        prev_g, prev_u = _time_major(sg_ref), _time_major(su_ref)
    else:
        prev_g = jnp.where(first_of_seq, 0.0, carry_g[f])
        prev_u = jnp.where(first_of_seq, 0.0, carry_u[f])

    def up(r):
        hs = hs_ref[r:r + rb, :]
        return (jnp.dot(hs, wg_ref[...], preferred_element_type=F32),
                jnp.dot(hs, wu_ref[...], preferred_element_type=F32))

    def conv(prev, u, cw, cb):
        ext = jnp.concatenate([prev, u], axis=0)
        if shift % SUBLANES == 0:
            u1 = ext[halo - shift:halo - shift + rb, :]
            u2 = ext[halo - 2 * shift:halo - 2 * shift + rb, :]
        else:
            u1 = _shift_rows(ext, shift)[halo:, :]
            u2 = _shift_rows(ext, 2 * shift)[halo:, :]
        return cw[0:1, :] * u2 + cw[1:2, :] * u1 + cw[2:3, :] * u + cb

    nxt = up(0)
    for r in range(0, tm, rb):
        ug, uu = nxt
        if r + rb < tm:
            nxt = up(r + rb)
        cg = conv(prev_g, ug, cwg, cbg)
        cu = conv(prev_u, uu, cwu, cbu)
        prev_g, prev_u = ug[rb - halo:, :], uu[rb - halo:, :]
        act = ((cg / (1.0 + jnp.exp(-cg))) * cu).astype(BF16)
        o_ref[r:r + rb, :] += jnp.dot(act, wd_ref[...], preferred_element_type=F32)
    if from_state:
        _store_batch_major(tg_ref, prev_g)
        _store_batch_major(tu_ref, prev_u)
    else:
        carry_g[f] = prev_g
        carry_u[f] = prev_u
        tg_ref[...] = prev_g
        tu_ref[...] = prev_u


def _conv_ffn(x, g, state, w_up, cw, cb, w_down, layer, *, shift, rows_per_seq):
    n, d = x.shape
    f2 = w_up.shape[2]
    ffn = f2 // 2
    tm = min(WIDE_TILE, n)
    fn = FFN_CHUNK
    nf = ffn // fn
    n_tiles = n // tm
    tiles_per_seq = max(rows_per_seq // tm, 1)
    halo = max((CONV_W - 1) * shift, SUBLANES)
    from_state = state is not None

    tile = pl.BlockSpec((tm, d), lambda i, f: (i, 0))
    tile_in = tile if from_state else pl.BlockSpec(memory_space=pl.ANY)
    gate_cols = lambda rows: pl.BlockSpec((rows, fn), lambda i, f: (0, f))
    up_cols = lambda rows: pl.BlockSpec((rows, fn), lambda i, f: (0, f + nf))
    layer_gate = lambda rows: pl.BlockSpec((None, rows, fn), lambda i, f: (layer, 0, f))
    layer_up = lambda rows: pl.BlockSpec((None, rows, fn), lambda i, f: (layer, 0, f + nf))
    in_specs = [tile_in, pl.BlockSpec((1, d), lambda i, f: (0, 0))]
    args = [x, g]
    scratch = [pltpu.VMEM((tm, d), BF16)]
    if from_state:
        nb, steps = state.shape[1], state.shape[2]
        assert n == tm and nb == shift and halo == steps * shift
        in_specs += [pl.BlockSpec((None, nb, steps, fn), lambda i, f: (layer, 0, 0, f)),
                     pl.BlockSpec((None, nb, steps, fn), lambda i, f: (layer, 0, 0, f + nf))]
        args += [state, state]
        tail_specs = [pl.BlockSpec((nb, steps, fn), lambda i, f: (0, 0, f))] * 2
        tail_shapes = [jax.ShapeDtypeStruct((nb, steps, ffn), F32)] * 2
    else:
        tail_specs = [pl.BlockSpec((None, halo, fn), lambda i, f: (i, 0, f))] * 2
        tail_shapes = [jax.ShapeDtypeStruct((n_tiles, halo, ffn), F32)] * 2
        scratch += [pltpu.VMEM((nf, halo, fn), F32), pltpu.VMEM((nf, halo, fn), F32),
                    pltpu.VMEM((tm, d), F32), pltpu.SemaphoreType.DMA(())]
    in_specs += [layer_gate(d), layer_up(d), layer_gate(CONV_W), layer_up(CONV_W), gate_cols(1), up_cols(1),
                 pl.BlockSpec((None, fn, d), lambda i, f: (layer, f, 0))]
    args += [w_up, w_up, cw, cw, cb, cb, w_down]
    return pl.pallas_call(
        functools.partial(_ffn_kernel, shift=shift, tiles_per_seq=tiles_per_seq, from_state=from_state),
        grid=(n_tiles, nf),
        in_specs=in_specs,
        out_specs=[tile] + tail_specs,
        out_shape=[jax.ShapeDtypeStruct((n, d), F32)] + tail_shapes,
        scratch_shapes=scratch,
        compiler_params=_params(2),
        name="conv_ffn_state" if from_state else "conv_ffn",
    )(*args)


def _ple_kernel(*refs, final):
    if final:
        x_ref, p_ref, g_ref, wg_ref, wp_ref, gf_ref, o_ref = refs
    else:
        x_ref, p_ref, g_ref, wg_ref, wp_ref, o_ref = refs
    tm = x_ref.shape[0]
    rb = min(ROW_BLOCK, tm)

    def project(r):
        hn = _rmsnorm(x_ref[r:r + rb, :], g_ref[...]).astype(BF16)
        return (jnp.dot(hn, wg_ref[...], preferred_element_type=F32),
                jnp.dot(p_ref[r:r + rb, :].astype(BF16), wp_ref[...], preferred_element_type=F32))

    nxt = project(0)
    for r in range(0, tm, rb):
        logits, emb = nxt
        if r + rb < tm:
            nxt = project(r + rb)
        y = x_ref[r:r + rb, :] + jax.nn.sigmoid(logits) * emb
        if final:
            y = _rmsnorm(y, gf_ref[...])
        if len(o_ref.shape) == 3:
            assert rb == tm
            _store_batch_major(o_ref, y)
        else:
            o_ref[r:r + rb, :] = y


def _ple(x, p, g, w_gate, w_proj, layer, g_final, steps=None):
    n, d = x.shape
    tm = min(TOKEN_TILE, n)
    final = g_final is not None
    if steps is None:
        out_spec = pl.BlockSpec((tm, d), lambda i: (i, 0))
        out_shape = jax.ShapeDtypeStruct((n, d), F32)
    else:
        assert n == tm
        out_spec = pl.BlockSpec((n // steps, steps, d), lambda i: (0, 0, 0))
        out_shape = jax.ShapeDtypeStruct((n // steps, steps, d), F32)
    const = lambda a: pl.BlockSpec(a.shape, lambda i: (0, 0))
    resident = lambda a: pl.BlockSpec((None,) + a.shape[1:], lambda i: (layer, 0, 0),
                                      pipeline_mode=pl.Buffered(1))
    in_specs = [pl.BlockSpec((tm, d), lambda i: (i, 0)),
                pl.BlockSpec((None, tm, p.shape[2]), lambda i: (layer, i, 0)),
                const(g), resident(w_gate), resident(w_proj)]
    args = [x, p, g, w_gate, w_proj]
    if final:
        in_specs.append(const(g_final))
        args.append(g_final)
    return pl.pallas_call(
        functools.partial(_ple_kernel, final=final),
        grid=(n // tm,),
        in_specs=in_specs,
        out_specs=out_spec,
        out_shape=out_shape,
        compiler_params=_params(1),
        name="ple_final" if final else "ple",
    )(*args)


def _ret_in_kernel(x_in, g_ref, w_ref, cos_ref, sin_ref, o_ref, hs_ref, *fetch, dk, layout):
    if fetch:
        x_ref = fetch[0]
        wait_x = _tile_fetch(x_in, *fetch)
    else:
        x_ref, wait_x = x_in, None

    @pl.when(pl.program_id(1) == 0)
    def _():
        if wait_x is not None:
            wait_x()
        hs_ref[...] = _rmsnorm(x_ref[...], g_ref[...]).astype(BF16)

    tm = hs_ref.shape[0]
    rb = min(ROW_BLOCK, tm)
    half = dk // 2
    project = lambda r: jnp.dot(hs_ref[r:r + rb, :], w_ref[...], preferred_element_type=F32)
    nxt = project(0)
    for r in range(0, tm, rb):
        proj = nxt
        if r + rb < tm:
            nxt = project(r + rb)
        cos = cos_ref[r:r + rb, :]
        sin = sin_ref[r:r + rb, :]
        for hh in range(proj.shape[1] // dk):
            x1 = proj[:, hh * dk:hh * dk + half]
            x2 = proj[:, hh * dk + half:(hh + 1) * dk]
            y1 = (x1 * cos - x2 * sin).astype(o_ref.dtype)
            y2 = (x1 * sin + x2 * cos).astype(o_ref.dtype)
            if layout == "heads":
                o_ref[hh, r:r + rb, 0:half] = y1
                o_ref[hh, r:r + rb, half:dk] = y2
            else:
                assert rb == tm
                _store_batch_major(o_ref, y1, slice(hh * dk, hh * dk + half))
                _store_batch_major(o_ref, y2, slice(hh * dk + half, (hh + 1) * dk))


def _ret_in(x, g, w_in, tables, *, qk_width, dk, rows_per_seq, out_dtype, steps=None):
    n, d = x.shape
    e = w_in.shape[1]
    tm = min(PROJ_TILE, n)
    nc = PROJ_CHUNK
    qc = qk_width // nc
    tps = rows_per_seq // tm

    def table_block(i, c):
        return (jnp.where(c < qc, 0, jnp.where(c < 2 * qc, tps, 2 * tps)) + i % tps, 0)

    scratch = [pltpu.VMEM((tm, d), BF16)]
    if steps is None:
        x_spec = pl.BlockSpec(memory_space=pl.ANY)
        w_spec = pl.BlockSpec((d, nc), lambda i, c: (0, c))
        out_spec = pl.BlockSpec((nc // dk, tm, dk), lambda i, c: (c, i, 0))
        out_shape = jax.ShapeDtypeStruct((e // dk, n, dk), out_dtype)
        scratch += [pltpu.VMEM((tm, d), F32), pltpu.SemaphoreType.DMA(())]
    else:
        assert n == tm
        x_spec = pl.BlockSpec((tm, d), lambda i, c: (i, 0))
        w_spec = pl.BlockSpec((d, nc), lambda i, c: (0, c))
        out_spec = pl.BlockSpec((n // steps, steps, nc), lambda i, c: (0, 0, c))
        out_shape = jax.ShapeDtypeStruct((n // steps, steps, e), out_dtype)

    return pl.pallas_call(
        functools.partial(_ret_in_kernel, dk=dk, layout="heads" if steps is None else "batch"),
        grid=(n // tm, e // nc),
        in_specs=[x_spec,
                  pl.BlockSpec((1, d), lambda i, c: (0, 0)),
                  w_spec,
                  pl.BlockSpec((tm, dk // 2), table_block),
                  pl.BlockSpec((tm, dk // 2), table_block)],
        out_specs=out_spec,
        out_shape=out_shape,
        scratch_shapes=scratch,
        compiler_params=_params(2),
        name="ret_in",
    )(x, g, w_in, *tables)


def _groupnorm_gate(o, gate):
    mu = jnp.mean(o, axis=-1, keepdims=True)
    var = jnp.mean(jnp.square(o - mu), axis=-1, keepdims=True)
    on = (o - mu) * lax.rsqrt(var + EPS)
    gf = gate.astype(F32)
    return (gf / (1.0 + jnp.exp(-gf))) * on


def _ret_prompt_kernel(q_ref, k_ref, v_ref, g_ref, intra_ref, cross_ref, kdec_ref, sdec_ref,
                       o_ref, s_out_ref, s_ref, *, chunk):
    hp, t, _ = q_ref.shape
    per = v_ref.shape[0] // hp
    s_ref[...] = jnp.zeros(s_ref.shape, F32)

    def wide(ref, j, rows):
        return jnp.concatenate([ref[j * per + i, rows, :] for i in range(per)], axis=-1)

    def body(c, carry):
        rows = pl.ds(pl.multiple_of(c * chunk, chunk), chunk)
        for j in range(hp):
            q = q_ref[j, rows, :]
            k = k_ref[j, rows, :]
            v = wide(v_ref, j, rows)
            s = s_ref[j]
            sc = lax.dot_general(q, k, (((1,), (1,)), ((), ())), preferred_element_type=F32) * intra_ref[j]
            o = (jnp.dot(sc.astype(BF16), v, preferred_element_type=F32)
                 + jnp.dot(q, s.astype(BF16), preferred_element_type=F32) * cross_ref[j])
            kd = (k.astype(F32) * kdec_ref[j]).astype(BF16)
            s_ref[j] = sdec_ref[j] * s + lax.dot_general(kd, v, (((0,), (0,)), ((), ())),
                                                         preferred_element_type=F32)
            o_ref[j, rows, :] = _groupnorm_gate(o, wide(g_ref, j, rows)).astype(o_ref.dtype)
        return carry

    lax.fori_loop(0, t // chunk, body, 0, unroll=2)
    s_out_ref[...] = s_ref[...]


def _ret_prompt(proj, consts, *, batch, seq, heads, dk, dv):
    intra, cross, kdec, sdec = consts
    chunk = intra.shape[1]
    hp = RET_HEAD_GROUP
    groups = heads // hp
    per = dv // dk
    v_off = 2 * groups // per
    g_off = v_off + groups
    head_const = lambda a: pl.BlockSpec((hp,) + a.shape[1:], lambda b, h: (h, 0, 0))
    return pl.pallas_call(
        functools.partial(_ret_prompt_kernel, chunk=chunk),
        grid=(batch, groups),
        in_specs=[pl.BlockSpec((hp, seq, dk), lambda b, h: (h, b, 0)),
                  pl.BlockSpec((hp, seq, dk), lambda b, h: (groups + h, b, 0)),
                  pl.BlockSpec((hp * per, seq, dk), lambda b, h: (v_off + h, b, 0)),
                  pl.BlockSpec((hp * per, seq, dk), lambda b, h: (g_off + h, b, 0)),
                  head_const(intra), head_const(cross), head_const(kdec), head_const(sdec)],
        out_specs=[pl.BlockSpec((hp, seq, dv), lambda b, h: (h, b, 0)),
                   pl.BlockSpec((None, hp, dk, dv), lambda b, h: (b, h, 0, 0))],
        out_shape=[jax.ShapeDtypeStruct((heads, batch * seq, dv), BF16),
                   jax.ShapeDtypeStruct((batch, heads, dk, dv), F32)],
        scratch_shapes=[pltpu.VMEM((hp, dk, dv), F32)],
        compiler_params=_params(2),
        name="retention_prompt",
    )(proj, proj, proj, proj, intra, cross, kdec, sdec)


def _ret_sample_kernel(p_ref, s_ref, intra_ref, cross_ref, kdec_ref, sdec_ref, o_ref, s_out_ref,
                       *, heads, dk, dv):
    qk = heads * dk
    for n in range(p_ref.shape[0]):
        for h in range(heads):
            q = p_ref[n, :, h * dk:(h + 1) * dk].astype(BF16)
            kf = p_ref[n, :, qk + h * dk:qk + (h + 1) * dk]
            v = p_ref[n, :, 2 * qk + h * dv:2 * qk + (h + 1) * dv].astype(BF16)
            gate = p_ref[n, :, 2 * qk + heads * dv + h * dv:2 * qk + heads * dv + (h + 1) * dv]
            s = s_ref[n, h]
            sc = lax.dot_general(q, kf.astype(BF16), (((1,), (1,)), ((), ())),
                                 preferred_element_type=F32) * intra_ref[h]
            o = (jnp.dot(sc.astype(BF16), v, preferred_element_type=F32)
                 + jnp.dot(q, s.astype(BF16), preferred_element_type=F32) * cross_ref[h])
            kd = (kf * kdec_ref[h]).astype(BF16)
            s_out_ref[n, h] = sdec_ref[h] * s + lax.dot_general(kd, v, (((0,), (0,)), ((), ())),
                                                                preferred_element_type=F32)
            o_ref[n, :, h * dv:(h + 1) * dv] = _groupnorm_gate(o, gate).astype(o_ref.dtype)


def _ret_sample(proj, state, consts, *, heads, dk, dv):
    intra, cross, kdec, sdec = consts
    b, tp, e = proj.shape
    nb = SAMPLE_SEQS
    const = lambda a: pl.BlockSpec(a.shape, lambda i: (0,) * a.ndim)
    return pl.pallas_call(
        functools.partial(_ret_sample_kernel, heads=heads, dk=dk, dv=dv),
        grid=(b // nb,),
        in_specs=[pl.BlockSpec((nb, tp, e), lambda i: (i, 0, 0)),
                  pl.BlockSpec((nb, heads, dk, dv), lambda i: (i, 0, 0, 0)),
                  const(intra), const(cross), const(kdec), const(sdec)],
        out_specs=[pl.BlockSpec((nb, tp, heads * dv), lambda i: (i, 0, 0)),
                   pl.BlockSpec((nb, heads, dk, dv), lambda i: (i, 0, 0, 0))],
        out_shape=[jax.ShapeDtypeStruct((b, tp, heads * dv), F32),
                   jax.ShapeDtypeStruct((b, heads, dk, dv), F32)],
        compiler_params=_params(1),
        name="retention_sample",
    )(proj, state, intra, cross, kdec, sdec)


def _ret_out_kernel(x_ref, a_ref, w_ref, o_ref, *, head_major):
    if head_major:
        dv = a_ref.shape[2]
        y = x_ref[...]
        for h in range(a_ref.shape[0]):
            y = y + jnp.dot(a_ref[h], w_ref[h * dv:(h + 1) * dv, :], preferred_element_type=F32)
        o_ref[...] = y
    else:
        a = _time_major(a_ref).astype(BF16)
        o_ref[...] = x_ref[...] + jnp.dot(a, w_ref[...], preferred_element_type=F32)


def _ret_out(x, a, w_out, *, head_major):
    n, d = x.shape
    tm = min(TOKEN_TILE, n)
    if head_major:
        a_spec = pl.BlockSpec((a.shape[0], tm, a.shape[2]), lambda i: (0, i, 0))
    else:
        assert n == tm
        a_spec = pl.BlockSpec(a.shape, lambda i: (0, 0, 0))
    v = w_out.shape[0]
    return pl.pallas_call(
        functools.partial(_ret_out_kernel, head_major=head_major),
        grid=(n // tm,),
        in_specs=[pl.BlockSpec((tm, d), lambda i: (i, 0)),
                  a_spec,
                  pl.BlockSpec((v, d), lambda i: (0, 0), pipeline_mode=pl.Buffered(1))],
        out_specs=pl.BlockSpec((tm, d), lambda i: (i, 0)),
        out_shape=jax.ShapeDtypeStruct((n, d), F32),
        compiler_params=_params(1),
        name="ret_out",
    )(x, a, w_out)


def _decay_consts(chunk, heads, dk, dv):
    log_g = np.log1p(-(2.0 ** (-5.0 - np.arange(heads, dtype=np.float64))))
    n = np.arange(chunk, dtype=np.float64)
    diff = n[:, None] - n[None, :]
    intra = np.where(diff >= 0, np.exp(log_g[:, None, None] * np.maximum(diff, 0.0)), 0.0)
    cross = np.exp(log_g[:, None] * (n + 1.0))
    kdec = np.exp(log_g[:, None] * (chunk - 1.0 - n))
    sdec = np.exp(log_g * chunk)
    f32 = lambda a: jnp.asarray(np.ascontiguousarray(a), dtype=F32)
    return (f32(intra),
            f32(np.broadcast_to(cross[:, :, None], (heads, chunk, dv))),
            f32(np.broadcast_to(kdec[:, :, None], (heads, chunk, dk))),
            f32(np.broadcast_to(sdec[:, None, None], (heads, 1, dv))))


def _rotary_tables(pos, dk):
    half = dk // 2
    inv = ROPE_BASE ** (-np.arange(half, dtype=np.float64) / half)
    ang = np.asarray(pos, dtype=np.float64)[:, None] * inv[None, :]
    cos, sin = np.cos(ang), np.sin(ang)
    scale = dk ** -0.5
    f32 = lambda a: jnp.asarray(a, dtype=F32)
    return (f32(np.concatenate([cos, cos * scale, np.ones_like(cos)])),
            f32(np.concatenate([sin, sin * scale, np.zeros_like(sin)])))


def kernel(x_prompt, x_sample, p_prompt, p_sample, state_pool, state_ret, state_conv, norm_mix, norm_ffn,
           norm_ple, norm_final, pool_w, pool_scale, ret_w_in, ret_w_out, ffn_w_up, ffn_conv_w,
           ffn_conv_b, ffn_w_down, ple_w_proj, ple_w_gate):
    b, t, d = x_prompt.shape
    bs, ts, _ = x_sample.shape
    depth = norm_mix.shape[0]
    heads = RET_HEADS
    dk = d // heads
    dv = 2 * dk
    qk = heads * dk
    f2 = ffn_w_up.shape[2]
    ffn = f2 // 2
    assert ts >= CONV_W - 1 and PAST_LEN >= POOL_BUF

    row2 = lambda a: a.reshape(1, -1)
    t_major = lambda a: jnp.swapaxes(a, 0, 1).reshape(-1, a.shape[-1])
    b_major = lambda a: jnp.swapaxes(a.reshape(-1, bs, a.shape[-1]), 0, 1)

    xp = x_prompt.reshape(b * t, d)
    xs = t_major(x_sample)
    pp = p_prompt.reshape(depth, b * t, -1)
    ps = jnp.swapaxes(p_sample, 1, 2).reshape(depth, ts * bs, -1)

    tables_p = _rotary_tables(np.arange(t), dk)
    tables_s = _rotary_tables(PAST_LEN + np.repeat(np.arange(ts), bs), dk)
    chunk_p = RET_CHUNK if t % RET_CHUNK == 0 else t
    consts_p = _decay_consts(chunk_p, heads, dk, dv)
    consts_s = _decay_consts(ts, heads, dk, dv)

    new_pool_p, new_pool_s, new_ret_p, new_ret_s, new_conv_p, new_conv_s = [], [], [], [], [], []
    for i in range(depth):
        jm = i // 2
        if i % 2 == 0:
            w = pool_w[jm]
            g = row2(norm_mix[i])
            sc = row2(pool_scale[jm])
            xp, hl = _pool_mixer(xp, None, g, w, sc, shift=1, pos0=0, rows_per_seq=t)
            new_pool_p.append(hl[:, -POOL_BUF:, :])
            xs3 = x_sample if i == 0 else b_major(xs)
            xs, ns = _pool_mixer(xs3, t_major(state_pool[jm]), g, w, sc, shift=bs, pos0=PAST_LEN,
                                 rows_per_seq=ts * bs)
            new_pool_s.append(ns)
        else:
            w_in = ret_w_in[jm]
            w_out = ret_w_out[jm].astype(BF16)
            g = row2(norm_mix[i])
            proj_p = _ret_in(xp, g, w_in, tables_p, qk_width=qk, dk=dk, rows_per_seq=t, out_dtype=BF16)
            a_p, s_p = _ret_prompt(proj_p, consts_p, batch=b, seq=t, heads=heads, dk=dk, dv=dv)
            xp = _ret_out(xp, a_p, w_out, head_major=True)
            new_ret_p.append(s_p)
            proj_s = _ret_in(xs, g, w_in, tables_s, qk_width=qk, dk=dk, rows_per_seq=ts * bs, out_dtype=F32,
                             steps=ts)
            a_s, s_s = _ret_sample(proj_s, state_ret[jm], consts_s, heads=heads, dk=dk, dv=dv)
            xs = _ret_out(xs, a_s, w_out, head_major=False)
            new_ret_s.append(s_s)

        g = row2(norm_ffn[i])
        cb = row2(ffn_conv_b[i])
        xp, tg, tu = _conv_ffn(xp, g, None, ffn_w_up, ffn_conv_w, cb, ffn_w_down, i, shift=1, rows_per_seq=t)
        tiles_per_seq = tg.shape[0] // b
        last = jnp.concatenate([tg, tu], axis=-1)[tiles_per_seq - 1::tiles_per_seq]
        new_conv_p.append(last[:, -(CONV_W - 1):, :])
        xs, tg, tu = _conv_ffn(xs, g, state_conv, ffn_w_up, ffn_conv_w, cb, ffn_w_down, i, shift=bs,
                               rows_per_seq=ts * bs)
        new_conv_s.append(jnp.concatenate([tg, tu], axis=-1))

        g = row2(norm_ple[i])
        last_layer = i == depth - 1
        g_final = row2(norm_final) if last_layer else None
        xp = _ple(xp, pp, g, ple_w_gate, ple_w_proj, i, g_final)
        xs = _ple(xs, ps, g, ple_w_gate, ple_w_proj, i, g_final, steps=ts if last_layer else None)

    return (xp.reshape(b, t, d), xs,
            jnp.stack(new_pool_p), jnp.stack(new_pool_s),
            jnp.stack(new_ret_p), jnp.stack(new_ret_s),
            jnp.stack(new_conv_p), jnp.stack(new_conv_s))
```

```python
import functools

import jax
import jax.numpy as jnp
import numpy as np
from jax import lax
from jax.experimental import pallas as pl
from jax.experimental.pallas import tpu as pltpu

F32 = jnp.float32
BF16 = jnp.bfloat16

EPS = 1e-6
POOL_WINDOWS = (2, 4, 8, 16)
POOL_BUF = max(POOL_WINDOWS) - 1
RET_HEADS = 8
RET_CHUNK = 256
ROPE_BASE = 10000.0
CONV_W = 3
PAST_LEN = 16384

TOKEN_TILE = 512
WIDE_TILE = 1024
PROJ_TILE = 2048
FFN_CHUNK = 512
ROW_BLOCK = 512
PROJ_CHUNK = 1024
RET_HEAD_GROUP = 2
SAMPLE_SEQS = 2
SUBLANES = 8
VMEM_LIMIT = 60 * 1024 * 1024


def _params(n_axes):
    return pltpu.CompilerParams(
        dimension_semantics=("arbitrary",) * n_axes, vmem_limit_bytes=VMEM_LIMIT)


def _rmsnorm(x, g):
    return x * lax.rsqrt(jnp.mean(x * x, axis=-1, keepdims=True) + EPS) * g


def _shift_rows(a, rows):
    return pltpu.roll(a, rows, axis=0)


def _tile_fetch(x_hbm, x_ref, sem):
    tm = x_ref.shape[0]
    i = pl.program_id(0)
    f = pl.program_id(1)

    def copy(tile):
        return pltpu.make_async_copy(x_hbm.at[pl.ds(tile * tm, tm), :], x_ref, sem)

    @pl.when(jnp.logical_and(f == 0, i == 0))
    def _():
        copy(0).start()

    @pl.when(jnp.logical_and(f == 1, i + 1 < pl.num_programs(0)))
    def _():
        copy(i + 1).start()

    return lambda: copy(i).wait()


def _time_major(ref, cols=slice(None)):
    return jnp.concatenate([ref[:, t, cols] for t in range(ref.shape[1])], axis=0)


def _store_batch_major(ref, val, cols=slice(None)):
    nb = ref.shape[0]
    for t in range(ref.shape[1]):
        ref[:, t, cols] = val[t * nb:(t + 1) * nb, :]


def _pool_group(h, prev, x, w, sc, pos, win, shift):
    halo = prev.shape[0]
    a = jnp.concatenate([prev, h], axis=0)
    span = 1
    while span < win:
        a = a + _shift_rows(a, span * shift)
        span *= 2
    cnt = jnp.minimum(pos + 1, win).astype(F32)
    dlt = a[halo:, :] / cnt - h
    return x + jnp.dot(dlt.astype(BF16), w, preferred_element_type=F32) * sc


def _pool_prompt_kernel(x_ref, g_ref, w_ref, sc_ref, o_ref, h_ref, carry_ref, *, tiles_per_seq):
    tm, d = x_ref.shape
    gw = d // len(POOL_WINDOWS)
    halo = carry_ref.shape[0]
    j = pl.program_id(0) % tiles_per_seq
    x = x_ref[...]
    h = _rmsnorm(x, g_ref[...])
    last = h[tm - halo:, :]
    h_ref[...] = last
    pos = j * tm + lax.broadcasted_iota(jnp.int32, (tm, 1), 0)
    for gi, win in enumerate(POOL_WINDOWS):
        cols = slice(gi * gw, (gi + 1) * gw)
        prev = jnp.where(j == 0, 0.0, carry_ref[:, cols])
        o_ref[:, cols] = _pool_group(h[:, cols], prev, x[:, cols], w_ref[gi], sc_ref[:, cols], pos, win, 1)
    carry_ref[...] = last


def _pool_state_kernel(x_ref, state_ref, g_ref, w_ref, sc_ref, o_ref, ns_ref, hs_ref, *, pos0):
    shift, steps, d = x_ref.shape
    tm = shift * steps
    gw = o_ref.shape[1]
    kept = ns_ref.shape[1]
    grp = pl.program_id(0)

    @pl.when(grp == 0)
    def _():
        hs_ref[...] = _rmsnorm(_time_major(x_ref), g_ref[...])

    row = lax.broadcasted_iota(jnp.int32, (tm, 1), 0)
    step = jnp.zeros_like(row)
    for k in range(1, steps):
        step = step + (row >= k * shift).astype(jnp.int32)
    pos = pos0 + step

    for gi, win in enumerate(POOL_WINDOWS):
        @pl.when(grp == gi)
        def _(gi=gi, win=win):
            cols = slice(gi * gw, (gi + 1) * gw)
            h = hs_ref[:, cols]
            prev = jnp.concatenate([jnp.zeros((shift, gw), F32), state_ref[...]], axis=0)
            o_ref[...] = _pool_group(h, prev, _time_major(x_ref, cols), w_ref[...], sc_ref[...],
                                     pos, win, shift)
            ext = jnp.concatenate([prev, h], axis=0)
            _store_batch_major(ns_ref, ext[ext.shape[0] - kept * shift:, :])


def _pool_mixer(x, halo, g, w, scale, *, shift, pos0, rows_per_seq):
    d = x.shape[-1]
    ng = len(POOL_WINDOWS)
    gw = d // ng
    if halo is None:
        n = x.shape[0]
        tm = min(WIDE_TILE, n)
        assert shift == 1 and pos0 == 0
        tiles_per_seq = rows_per_seq // tm
        halo_rows = POOL_BUF + 1
        full = lambda a: pl.BlockSpec(a.shape, lambda i: (0,) * a.ndim)
        tile = pl.BlockSpec((tm, d), lambda i: (i, 0))
        return pl.pallas_call(
            functools.partial(_pool_prompt_kernel, tiles_per_seq=tiles_per_seq),
            grid=(n // tm,),
            in_specs=[tile, full(g), full(w), full(scale)],
            out_specs=[tile, pl.BlockSpec((None, halo_rows, d), lambda i: (i // tiles_per_seq, 0, 0))],
            out_shape=[jax.ShapeDtypeStruct((n, d), F32),
                       jax.ShapeDtypeStruct((n // rows_per_seq, halo_rows, d), F32)],
            scratch_shapes=[pltpu.VMEM((halo_rows, d), F32)],
            compiler_params=_params(1),
            name="pool_mixer",
        )(x, g, w, scale)
    nb, steps, _ = x.shape
    kept = halo.shape[0] // nb
    assert shift == nb and kept == POOL_BUF
    return pl.pallas_call(
        functools.partial(_pool_state_kernel, pos0=pos0),
        grid=(ng,),
        in_specs=[pl.BlockSpec((nb, steps, d), lambda gi: (0, 0, 0)),
                  pl.BlockSpec((kept * nb, gw), lambda gi: (0, gi)),
                  pl.BlockSpec((1, d), lambda gi: (0, 0)),
                  pl.BlockSpec((None, gw, gw), lambda gi: (gi, 0, 0)),
                  pl.BlockSpec((1, gw), lambda gi: (0, gi))],
        out_specs=[pl.BlockSpec((nb * steps, gw), lambda gi: (0, gi)),
                   pl.BlockSpec((nb, kept, gw), lambda gi: (0, 0, gi))],
        out_shape=[jax.ShapeDtypeStruct((nb * steps, d), F32), jax.ShapeDtypeStruct((nb, kept, d), F32)],
        scratch_shapes=[pltpu.VMEM((nb * steps, d), F32)],
        compiler_params=_params(1),
        name="pool_mixer_state",
    )(x, halo, g, w, scale)


def _ffn_kernel(*refs, shift, tiles_per_seq, from_state):
    if from_state:
        (x_ref, g_ref, sg_ref, su_ref, wg_ref, wu_ref, cpg_ref, cpu_ref, wd_ref,
         o_ref, tail_ref, hs_ref) = refs
        carry_g = carry_u = None
    else:
        (x_hbm, g_ref, wg_ref, wu_ref, cpg_ref, cpu_ref, wd_ref,
         o_ref, tail_ref, hs_ref, carry_g, carry_u, x_ref, x_sem) = refs
        sg_ref = su_ref = None
    tm, d = x_ref.shape
    halo = max((CONV_W - 1) * shift, SUBLANES)
    i = pl.program_id(0)
    f = pl.program_id(1)
    first_of_seq = (i % tiles_per_seq) == 0

    wait_x = None if from_state else _tile_fetch(x_hbm, x_ref, x_sem)

    @pl.when(f == 0)
    def _():
        if wait_x is not None:
            wait_x()
        x = x_ref[...]
        hs_ref[...] = _rmsnorm(x, g_ref[...]).astype(BF16)
        o_ref[...] = x

    rb = min(ROW_BLOCK, tm)
    cpg, cpu = cpg_ref[...], cpu_ref[...]
    if from_state:
        assert rb == tm
        prev_g, prev_u = _time_major(sg_ref), _time_major(su_ref)
    else:
        prev_g = jnp.where(first_of_seq, 0.0, carry_g[f])
        prev_u = jnp.where(first_of_seq, 0.0, carry_u[f])

    def up(r):
        hs = hs_ref[r:r + rb, :]
        return (jnp.dot(hs, wg_ref[...], preferred_element_type=F32),
                jnp.dot(hs, wu_ref[...], preferred_element_type=F32))

    def conv(prev, u, cp):
        ext = jnp.concatenate([prev, u], axis=0)
        if shift % SUBLANES == 0:
            u1 = ext[halo - shift:halo - shift + rb, :]
            u2 = ext[halo - 2 * shift:halo - 2 * shift + rb, :]
        else:
            u1 = _shift_rows(ext, shift)[halo:, :]
            u2 = _shift_rows(ext, 2 * shift)[halo:, :]
        return cp[0:1, :] * u2 + cp[1:2, :] * u1 + cp[2:3, :] * u + cp[3:4, :]

    nxt = up(0)
    for r in range(0, tm, rb):
        ug, uu = nxt
        if r + rb < tm:
            nxt = up(r + rb)
        cg = conv(prev_g, ug, cpg)
        cu = conv(prev_u, uu, cpu)
        prev_g, prev_u = ug[rb - halo:, :], uu[rb - halo:, :]
        act = ((cg / (1.0 + jnp.exp(-cg))) * cu).astype(BF16)
        o_ref[r:r + rb, :] += jnp.dot(act, wd_ref[...], preferred_element_type=F32)
    if from_state:
        _store_batch_major(tail_ref.at[0], prev_g)
        _store_batch_major(tail_ref.at[1], prev_u)
    else:
        carry_g[f] = prev_g
        carry_u[f] = prev_u
        tail_ref[0] = prev_g
        tail_ref[1] = prev_u


def _conv_ffn(x, g, state, w_up, conv_params, w_down, layer, *, shift, rows_per_seq):
    n, d = x.shape
    f2 = w_up.shape[2]
    ffn = f2 // 2
    tm = min(WIDE_TILE, n)
    fn = FFN_CHUNK
    nf = ffn // fn
    n_tiles = n // tm
    tiles_per_seq = max(rows_per_seq // tm, 1)
    halo = max((CONV_W - 1) * shift, SUBLANES)
    from_state = state is not None

    tile = pl.BlockSpec((tm, d), lambda i, f: (i, 0))
    tile_in = tile if from_state else pl.BlockSpec(memory_space=pl.ANY)
    layer_gate = lambda rows: pl.BlockSpec((None, rows, fn), lambda i, f: (layer, 0, f))
    layer_up = lambda rows: pl.BlockSpec((None, rows, fn), lambda i, f: (layer, 0, f + nf))
    in_specs = [tile_in, pl.BlockSpec((1, d), lambda i, f: (0, 0))]
    args = [x, g]
    scratch = [pltpu.VMEM((tm, d), BF16)]
    if from_state:
        nb, steps = state.shape[1], state.shape[2]
        assert n == tm and nb == shift and halo == steps * shift
        in_specs += [pl.BlockSpec((None, nb, steps, fn), lambda i, f: (layer, 0, 0, f)),
                     pl.BlockSpec((None, nb, steps, fn), lambda i, f: (layer, 0, 0, f + nf))]
        args += [state, state]
        tail_spec = pl.BlockSpec((2, nb, steps, fn), lambda i, f: (0, 0, 0, f))
        tail_shape = jax.ShapeDtypeStruct((2, nb, steps, ffn), F32)
    else:
        tail_spec = pl.BlockSpec((None, 2, halo, fn), lambda i, f: (i, 0, 0, f))
        tail_shape = jax.ShapeDtypeStruct((n_tiles, 2, halo, ffn), F32)
        scratch += [pltpu.VMEM((nf, halo, fn), F32), pltpu.VMEM((nf, halo, fn), F32),
                    pltpu.VMEM((tm, d), F32), pltpu.SemaphoreType.DMA(())]
    rows = conv_params.shape[1]
    in_specs += [layer_gate(d), layer_up(d), layer_gate(rows), layer_up(rows),
                 pl.BlockSpec((None, fn, d), lambda i, f: (layer, f, 0))]
    args += [w_up, w_up, conv_params, conv_params, w_down]
    return pl.pallas_call(
        functools.partial(_ffn_kernel, shift=shift, tiles_per_seq=tiles_per_seq, from_state=from_state),
        grid=(n_tiles, nf),
        in_specs=in_specs,
        out_specs=[tile, tail_spec],
        out_shape=[jax.ShapeDtypeStruct((n, d), F32), tail_shape],
        scratch_shapes=scratch,
        compiler_params=_params(2),
        name="conv_ffn_state" if from_state else "conv_ffn",
    )(*args)


def _ple_kernel(*refs, final):
    if final:
        x_ref, p_ref, g_ref, wg_ref, wp_ref, gf_ref, o_ref = refs
    else:
        x_ref, p_ref, g_ref, wg_ref, wp_ref, o_ref = refs
    x = x_ref[...]
    hn = _rmsnorm(x, g_ref[...]).astype(BF16)
    gate = jax.nn.sigmoid(jnp.dot(hn, wg_ref[...], preferred_element_type=F32))
    emb = jnp.dot(p_ref[...].astype(BF16), wp_ref[...], preferred_element_type=F32)
    y = x + gate * emb
    if final:
        y = _rmsnorm(y, gf_ref[...])
    if len(o_ref.shape) == 3:
        _store_batch_major(o_ref, y)
    else:
        o_ref[...] = y


def _ple(x, p, g, w_gate, w_proj, layer, g_final, steps=None):
    n, d = x.shape
    tm = min(TOKEN_TILE, n)
    final = g_final is not None
    if steps is None:
        out_spec = pl.BlockSpec((tm, d), lambda i: (i, 0))
        out_shape = jax.ShapeDtypeStruct((n, d), F32)
    else:
        assert n == tm
        out_spec = pl.BlockSpec((n // steps, steps, d), lambda i: (0, 0, 0))
        out_shape = jax.ShapeDtypeStruct((n // steps, steps, d), F32)
    const = lambda a: pl.BlockSpec(a.shape, lambda i: (0, 0))
    resident = lambda a: pl.BlockSpec((None,) + a.shape[1:], lambda i: (layer, 0, 0),
                                      pipeline_mode=pl.Buffered(1))
    in_specs = [pl.BlockSpec((tm, d), lambda i: (i, 0)),
                pl.BlockSpec((None, tm, p.shape[2]), lambda i: (layer, i, 0)),
                const(g), resident(w_gate), resident(w_proj)]
    args = [x, p, g, w_gate, w_proj]
    if final:
        in_specs.append(const(g_final))
        args.append(g_final)
    return pl.pallas_call(
        functools.partial(_ple_kernel, final=final),
        grid=(n // tm,),
        in_specs=in_specs,
        out_specs=out_spec,
        out_shape=out_shape,
        compiler_params=_params(1),
        name="ple_final" if final else "ple",
    )(*args)


def _ret_in_kernel(x_in, g_ref, w_ref, cos_ref, sin_ref, o_ref, hs_ref, *fetch, dk, layout):
    if fetch:
        x_ref = fetch[0]
        wait_x = _tile_fetch(x_in, *fetch)
    else:
        x_ref, wait_x = x_in, None

    @pl.when(pl.program_id(1) == 0)
    def _():
        if wait_x is not None:
            wait_x()
        hs_ref[...] = _rmsnorm(x_ref[...], g_ref[...]).astype(BF16)

    tm = hs_ref.shape[0]
    rb = min(ROW_BLOCK, tm)
    half = dk // 2
    project = lambda r: jnp.dot(hs_ref[r:r + rb, :], w_ref[...], preferred_element_type=F32)
    nxt = project(0)
    for r in range(0, tm, rb):
        proj = nxt
        if r + rb < tm:
            nxt = project(r + rb)
        cos = cos_ref[r:r + rb, :]
        sin = sin_ref[r:r + rb, :]
        for hh in range(proj.shape[1] // dk):
            x1 = proj[:, hh * dk:hh * dk + half]
            x2 = proj[:, hh * dk + half:(hh + 1) * dk]
            y1 = (x1 * cos - x2 * sin).astype(o_ref.dtype)
            y2 = (x1 * sin + x2 * cos).astype(o_ref.dtype)
            if layout == "heads":
                o_ref[hh, r:r + rb, 0:half] = y1
                o_ref[hh, r:r + rb, half:dk] = y2
            else:
                assert rb == tm
                _store_batch_major(o_ref, y1, slice(hh * dk, hh * dk + half))
                _store_batch_major(o_ref, y2, slice(hh * dk + half, (hh + 1) * dk))


def _ret_in(x, g, w_in, tables, *, qk_width, dk, rows_per_seq, out_dtype, steps=None):
    n, d = x.shape
    e = w_in.shape[1]
    tm = min(PROJ_TILE, n)
    nc = PROJ_CHUNK
    qc = qk_width // nc
    tps = rows_per_seq // tm

    def table_block(i, c):
        return (jnp.where(c < qc, 0, jnp.where(c < 2 * qc, tps, 2 * tps)) + i % tps, 0)

    scratch = [pltpu.VMEM((tm, d), BF16)]
    if steps is None:
        x_spec = pl.BlockSpec(memory_space=pl.ANY)
        w_spec = pl.BlockSpec((d, nc), lambda i, c: (0, c))
        out_spec = pl.BlockSpec((nc // dk, tm, dk), lambda i, c: (c, i, 0))
        out_shape = jax.ShapeDtypeStruct((e // dk, n, dk), out_dtype)
        scratch += [pltpu.VMEM((tm, d), F32), pltpu.SemaphoreType.DMA(())]
    else:
        assert n == tm
        x_spec = pl.BlockSpec((tm, d), lambda i, c: (i, 0))
        w_spec = pl.BlockSpec((d, nc), lambda i, c: (0, c))
        out_spec = pl.BlockSpec((n // steps, steps, nc), lambda i, c: (0, 0, c))
        out_shape = jax.ShapeDtypeStruct((n // steps, steps, e), out_dtype)

    return pl.pallas_call(
        functools.partial(_ret_in_kernel, dk=dk, layout="heads" if steps is None else "batch"),
        grid=(n // tm, e // nc),
        in_specs=[x_spec,
                  pl.BlockSpec((1, d), lambda i, c: (0, 0)),
                  w_spec,
                  pl.BlockSpec((tm, dk // 2), table_block),
                  pl.BlockSpec((tm, dk // 2), table_block)],
        out_specs=out_spec,
        out_shape=out_shape,
        scratch_shapes=scratch,
        compiler_params=_params(2),
        name="ret_in",
    )(x, g, w_in, *tables)


def _groupnorm_gate(o, gate):
    mu = jnp.mean(o, axis=-1, keepdims=True)
    var = jnp.mean(jnp.square(o - mu), axis=-1, keepdims=True)
    on = (o - mu) * lax.rsqrt(var + EPS)
    gf = gate.astype(F32)
    return (gf / (1.0 + jnp.exp(-gf))) * on


def _ret_prompt_kernel(q_ref, k_ref, v_ref, g_ref, intra_ref, cross_ref, kdec_ref, sdec_ref,
                       o_ref, s_out_ref, s_ref, *, chunk):
    hp, t, _ = q_ref.shape
    per = v_ref.shape[0] // hp
    s_ref[...] = jnp.zeros(s_ref.shape, F32)

    def wide(ref, j, rows):
        return jnp.concatenate([ref[j * per + i, rows, :] for i in range(per)], axis=-1)

    def body(c, carry):
        rows = pl.ds(pl.multiple_of(c * chunk, chunk), chunk)
        for j in range(hp):
            q = q_ref[j, rows, :]
            k = k_ref[j, rows, :]
            v = wide(v_ref, j, rows)
            s = s_ref[j]
            sc = lax.dot_general(q, k, (((1,), (1,)), ((), ())), preferred_element_type=F32) * intra_ref[j]
            o = (jnp.dot(sc.astype(BF16), v, preferred_element_type=F32)
                 + jnp.dot(q, s.astype(BF16), preferred_element_type=F32) * cross_ref[j])
            kd = (k.astype(F32) * kdec_ref[j]).astype(BF16)
            s_ref[j] = sdec_ref[j] * s + lax.dot_general(kd, v, (((0,), (0,)), ((), ())),
                                                         preferred_element_type=F32)
            o_ref[j, rows, :] = _groupnorm_gate(o, wide(g_ref, j, rows)).astype(o_ref.dtype)
        return carry

    lax.fori_loop(0, t // chunk, body, 0, unroll=2)
    s_out_ref[...] = s_ref[...]


def _ret_prompt(proj, consts, *, batch, seq, heads, dk, dv):
    intra, cross, kdec, sdec = consts
    chunk = intra.shape[1]
    hp = RET_HEAD_GROUP
    groups = heads // hp
    per = dv // dk
    v_off = 2 * groups // per
    g_off = v_off + groups
    head_const = lambda a: pl.BlockSpec((hp,) + a.shape[1:], lambda b, h: (h, 0, 0))
    return pl.pallas_call(
        functools.partial(_ret_prompt_kernel, chunk=chunk),
        grid=(batch, groups),
        in_specs=[pl.BlockSpec((hp, seq, dk), lambda b, h: (h, b, 0)),
                  pl.BlockSpec((hp, seq, dk), lambda b, h: (groups + h, b, 0)),
                  pl.BlockSpec((hp * per, seq, dk), lambda b, h: (v_off + h, b, 0)),
                  pl.BlockSpec((hp * per, seq, dk), lambda b, h: (g_off + h, b, 0)),
                  head_const(intra), head_const(cross), head_const(kdec), head_const(sdec)],
        out_specs=[pl.BlockSpec((hp, seq, dv), lambda b, h: (h, b, 0)),
                   pl.BlockSpec((None, hp, dk, dv), lambda b, h: (b, h, 0, 0))],
        out_shape=[jax.ShapeDtypeStruct((heads, batch * seq, dv), BF16),
                   jax.ShapeDtypeStruct((batch, heads, dk, dv), F32)],
        scratch_shapes=[pltpu.VMEM((hp, dk, dv), F32)],
        compiler_params=_params(2),
        name="retention_prompt",
    )(proj, proj, proj, proj, intra, cross, kdec, sdec)


def _ret_sample_kernel(p_ref, s_ref, intra_ref, cross_ref, kdec_ref, sdec_ref, o_ref, s_out_ref,
                       *, heads, dk, dv):
    qk = heads * dk
    for n in range(p_ref.shape[0]):
        for h in range(heads):
            q = p_ref[n, :, h * dk:(h + 1) * dk].astype(BF16)
            kf = p_ref[n, :, qk + h * dk:qk + (h + 1) * dk]
            v = p_ref[n, :, 2 * qk + h * dv:2 * qk + (h + 1) * dv].astype(BF16)
            gate = p_ref[n, :, 2 * qk + heads * dv + h * dv:2 * qk + heads * dv + (h + 1) * dv]
            s = s_ref[n, h]
            sc = lax.dot_general(q, kf.astype(BF16), (((1,), (1,)), ((), ())),
                                 preferred_element_type=F32) * intra_ref[h]
            o = (jnp.dot(sc.astype(BF16), v, preferred_element_type=F32)
                 + jnp.dot(q, s.astype(BF16), preferred_element_type=F32) * cross_ref[h])
            kd = (kf * kdec_ref[h]).astype(BF16)
            s_out_ref[n, h] = sdec_ref[h] * s + lax.dot_general(kd, v, (((0,), (0,)), ((), ())),
                                                                preferred_element_type=F32)
            o_ref[n, :, h * dv:(h + 1) * dv] = _groupnorm_gate(o, gate).astype(o_ref.dtype)


def _ret_sample(proj, state, consts, *, heads, dk, dv):
    intra, cross, kdec, sdec = consts
    b, tp, e = proj.shape
    nb = SAMPLE_SEQS
    const = lambda a: pl.BlockSpec(a.shape, lambda i: (0,) * a.ndim)
    return pl.pallas_call(
        functools.partial(_ret_sample_kernel, heads=heads, dk=dk, dv=dv),
        grid=(b // nb,),
        in_specs=[pl.BlockSpec((nb, tp, e), lambda i: (i, 0, 0)),
                  pl.BlockSpec((nb, heads, dk, dv), lambda i: (i, 0, 0, 0)),
                  const(intra), const(cross), const(kdec), const(sdec)],
        out_specs=[pl.BlockSpec((nb, tp, heads * dv), lambda i: (i, 0, 0)),
                   pl.BlockSpec((nb, heads, dk, dv), lambda i: (i, 0, 0, 0))],
        out_shape=[jax.ShapeDtypeStruct((b, tp, heads * dv), F32),
                   jax.ShapeDtypeStruct((b, heads, dk, dv), F32)],
        compiler_params=_params(1),
        name="retention_sample",
    )(proj, state, intra, cross, kdec, sdec)


def _ret_out_kernel(x_ref, a_ref, w_ref, o_ref, *, head_major):
    if head_major:
        dv = a_ref.shape[2]
        y = x_ref[...]
        for h in range(a_ref.shape[0]):
            y = y + jnp.dot(a_ref[h], w_ref[h * dv:(h + 1) * dv, :], preferred_element_type=F32)
        o_ref[...] = y
    else:
        a = _time_major(a_ref).astype(BF16)
        o_ref[...] = x_ref[...] + jnp.dot(a, w_ref[...], preferred_element_type=F32)


def _ret_out(x, a, w_out, *, head_major):
    n, d = x.shape
    tm = min(TOKEN_TILE, n)
    if head_major:
        a_spec = pl.BlockSpec((a.shape[0], tm, a.shape[2]), lambda i: (0, i, 0))
    else:
        assert n == tm
        a_spec = pl.BlockSpec(a.shape, lambda i: (0, 0, 0))
    v = w_out.shape[0]
    return pl.pallas_call(
        functools.partial(_ret_out_kernel, head_major=head_major),
        grid=(n // tm,),
        in_specs=[pl.BlockSpec((tm, d), lambda i: (i, 0)),
                  a_spec,
                  pl.BlockSpec((v, d), lambda i: (0, 0), pipeline_mode=pl.Buffered(1))],
        out_specs=pl.BlockSpec((tm, d), lambda i: (i, 0)),
        out_shape=jax.ShapeDtypeStruct((n, d), F32),
        compiler_params=_params(1),
        name="ret_out",
    )(x, a, w_out)


def _decay_consts(chunk, heads, dk, dv):
    log_g = np.log1p(-(2.0 ** (-5.0 - np.arange(heads, dtype=np.float64))))
    n = np.arange(chunk, dtype=np.float64)
    diff = n[:, None] - n[None, :]
    intra = np.where(diff >= 0, np.exp(log_g[:, None, None] * np.maximum(diff, 0.0)), 0.0)
    cross = np.exp(log_g[:, None] * (n + 1.0))
    kdec = np.exp(log_g[:, None] * (chunk - 1.0 - n))
    sdec = np.exp(log_g * chunk)
    f32 = lambda a: jnp.asarray(np.ascontiguousarray(a), dtype=F32)
    return (f32(intra),
            f32(np.broadcast_to(cross[:, :, None], (heads, chunk, dv))),
            f32(np.broadcast_to(kdec[:, :, None], (heads, chunk, dk))),
            f32(np.broadcast_to(sdec[:, None, None], (heads, 1, dv))))


def _rotary_tables(pos, dk):
    half = dk // 2
    inv = ROPE_BASE ** (-np.arange(half, dtype=np.float64) / half)
    ang = np.asarray(pos, dtype=np.float64)[:, None] * inv[None, :]
    cos, sin = np.cos(ang), np.sin(ang)
    scale = dk ** -0.5
    f32 = lambda a: jnp.asarray(a, dtype=F32)
    return (f32(np.concatenate([cos, cos * scale, np.ones_like(cos)])),
            f32(np.concatenate([sin, sin * scale, np.zeros_like(sin)])))


def kernel(x_prompt, x_sample, p_prompt, p_sample, state_pool, state_ret, state_conv, norm_mix, norm_ffn,
           norm_ple, norm_final, pool_w, pool_scale, ret_w_in, ret_w_out, ffn_w_up, ffn_conv_w,
           ffn_conv_b, ffn_w_down, ple_w_proj, ple_w_gate):
    b, t, d = x_prompt.shape
    bs, ts, _ = x_sample.shape
    depth = norm_mix.shape[0]
    heads = RET_HEADS
    dk = d // heads
    dv = 2 * dk
    qk = heads * dk
    f2 = ffn_w_up.shape[2]
    ffn = f2 // 2
    assert ts >= CONV_W - 1 and PAST_LEN >= POOL_BUF

    row2 = lambda a: a.reshape(1, -1)
    t_major = lambda a: jnp.swapaxes(a, 0, 1).reshape(-1, a.shape[-1])
    b_major = lambda a: jnp.swapaxes(a.reshape(-1, bs, a.shape[-1]), 0, 1)

    xp = x_prompt.reshape(b * t, d)
    xs = t_major(x_sample)
    conv_params = jnp.concatenate([ffn_conv_w, ffn_conv_b[:, None, :]], axis=1)
    pp = p_prompt.reshape(depth, b * t, -1)
    ps = jnp.swapaxes(p_sample, 1, 2).reshape(depth, ts * bs, -1)

    tables_p = _rotary_tables(np.arange(t), dk)
    tables_s = _rotary_tables(PAST_LEN + np.repeat(np.arange(ts), bs), dk)
    chunk_p = RET_CHUNK if t % RET_CHUNK == 0 else t
    consts_p = _decay_consts(chunk_p, heads, dk, dv)
    consts_s = _decay_consts(ts, heads, dk, dv)

    new_pool_p, new_pool_s, new_ret_p, new_ret_s, new_conv_p, new_conv_s = [], [], [], [], [], []
    for i in range(depth):
        jm = i // 2
        if i % 2 == 0:
            w = pool_w[jm]
            g = row2(norm_mix[i])
            sc = row2(pool_scale[jm])
            xp, hl = _pool_mixer(xp, None, g, w, sc, shift=1, pos0=0, rows_per_seq=t)
            new_pool_p.append(hl[:, -POOL_BUF:, :])
            xs3 = x_sample if i == 0 else b_major(xs)
            xs, ns = _pool_mixer(xs3, t_major(state_pool[jm]), g, w, sc, shift=bs, pos0=PAST_LEN,
                                 rows_per_seq=ts * bs)
            new_pool_s.append(ns)
        else:
            w_in = ret_w_in[jm]
            w_out = ret_w_out[jm].astype(BF16)
            g = row2(norm_mix[i])
            proj_p = _ret_in(xp, g, w_in, tables_p, qk_width=qk, dk=dk, rows_per_seq=t, out_dtype=BF16)
            a_p, s_p = _ret_prompt(proj_p, consts_p, batch=b, seq=t, heads=heads, dk=dk, dv=dv)
            xp = _ret_out(xp, a_p, w_out, head_major=True)
            new_ret_p.append(s_p)
            proj_s = _ret_in(xs, g, w_in, tables_s, qk_width=qk, dk=dk, rows_per_seq=ts * bs, out_dtype=F32,
                             steps=ts)
            a_s, s_s = _ret_sample(proj_s, state_ret[jm], consts_s, heads=heads, dk=dk, dv=dv)
            xs = _ret_out(xs, a_s, w_out, head_major=False)
            new_ret_s.append(s_s)

        g = row2(norm_ffn[i])
        xp, tails = _conv_ffn(xp, g, None, ffn_w_up, conv_params, ffn_w_down, i, shift=1, rows_per_seq=t)
        tiles_per_seq = tails.shape[0] // b
        last = tails[tiles_per_seq - 1::tiles_per_seq, :, -(CONV_W - 1):, :]
        new_conv_p.append(jnp.concatenate([last[:, 0], last[:, 1]], axis=-1))
        xs, tails = _conv_ffn(xs, g, state_conv, ffn_w_up, conv_params, ffn_w_down, i, shift=bs,
                              rows_per_seq=ts * bs)
        new_conv_s.append(jnp.concatenate([tails[0], tails[1]], axis=-1))

        g = row2(norm_ple[i])
        last_layer = i == depth - 1
        g_final = row2(norm_final) if last_layer else None
        xp = _ple(xp, pp, g, ple_w_gate, ple_w_proj, i, g_final)
        xs = _ple(xs, ps, g, ple_w_gate, ple_w_proj, i, g_final, steps=ts if last_layer else None)

    return (xp.reshape(b, t, d), xs,
            jnp.stack(new_pool_p), jnp.stack(new_pool_s),
            jnp.stack(new_ret_p), jnp.stack(new_ret_s),
            jnp.stack(new_conv_p), jnp.stack(new_conv_s))
```

```python
import functools

import jax
import jax.numpy as jnp
import numpy as np
from jax import lax
from jax.experimental import pallas as pl
from jax.experimental.pallas import tpu as pltpu

F32 = jnp.float32
BF16 = jnp.bfloat16

EPS = 1e-6
POOL_WINDOWS = (2, 4, 8, 16)
POOL_BUF = max(POOL_WINDOWS) - 1
RET_HEADS = 8
RET_CHUNK = 256
ROPE_BASE = 10000.0
CONV_W = 3
PAST_LEN = 16384

TOKEN_TILE = 512
WIDE_TILE = 1024
PROJ_TILE = 2048
FFN_CHUNK = 512
ROW_BLOCK = 512
PROJ_CHUNK = 1024
RET_HEAD_GROUP = 2
SAMPLE_SEQS = 2
SUBLANES = 8
VMEM_LIMIT = 60 * 1024 * 1024


def _params(n_axes):
    return pltpu.CompilerParams(
        dimension_semantics=("arbitrary",) * n_axes, vmem_limit_bytes=VMEM_LIMIT)


def _rmsnorm(x, g):
    return x * lax.rsqrt(jnp.mean(x * x, axis=-1, keepdims=True) + EPS) * g


def _shift_rows(a, rows):
    return pltpu.roll(a, rows, axis=0)


def _tile_fetch(x_hbm, x_ref, sem):
    tm = x_ref.shape[0]
    i = pl.program_id(0)
    f = pl.program_id(1)

    def copy(tile):
        return pltpu.make_async_copy(x_hbm.at[pl.ds(tile * tm, tm), :], x_ref, sem)

    @pl.when(jnp.logical_and(f == 0, i == 0))
    def _():
        copy(0).start()

    @pl.when(jnp.logical_and(f == 1, i + 1 < pl.num_programs(0)))
    def _():
        copy(i + 1).start()

    return lambda: copy(i).wait()


def _time_major(ref, cols=slice(None)):
    return jnp.concatenate([ref[:, t, cols] for t in range(ref.shape[1])], axis=0)


def _store_batch_major(ref, val, cols=slice(None)):
    nb = ref.shape[0]
    for t in range(ref.shape[1]):
        ref[:, t, cols] = val[t * nb:(t + 1) * nb, :]


def _pool_group(h, prev, x, w, sc, pos, win, shift):
    halo = prev.shape[0]
    a = jnp.concatenate([prev, h], axis=0)
    span = 1
    while span < win:
        a = a + _shift_rows(a, span * shift)
        span *= 2
    cnt = jnp.minimum(pos + 1, win).astype(F32)
    dlt = a[halo:, :] / cnt - h
    return x + jnp.dot(dlt.astype(BF16), w, preferred_element_type=F32) * sc


def _pool_prompt_kernel(x_ref, g_ref, w_ref, sc_ref, o_ref, h_ref, carry_ref, *, tiles_per_seq):
    tm, d = x_ref.shape
    gw = d // len(POOL_WINDOWS)
    halo = carry_ref.shape[0]
    j = pl.program_id(0) % tiles_per_seq
    x = x_ref[...]
    h = _rmsnorm(x, g_ref[...])
    last = h[tm - halo:, :]
    h_ref[...] = last
    pos = j * tm + lax.broadcasted_iota(jnp.int32, (tm, 1), 0)
    for gi, win in enumerate(POOL_WINDOWS):
        cols = slice(gi * gw, (gi + 1) * gw)
        prev = jnp.where(j == 0, 0.0, carry_ref[:, cols])
        o_ref[:, cols] = _pool_group(h[:, cols], prev, x[:, cols], w_ref[gi], sc_ref[:, cols], pos, win, 1)
    carry_ref[...] = last


def _pool_state_kernel(x_ref, state_ref, g_ref, w_ref, sc_ref, o_ref, ns_ref, hs_ref, *, pos0):
    shift, steps, d = x_ref.shape
    tm = shift * steps
    gw = o_ref.shape[1]
    kept = ns_ref.shape[1]
    grp = pl.program_id(0)

    @pl.when(grp == 0)
    def _():
        hs_ref[...] = _rmsnorm(_time_major(x_ref), g_ref[...])

    row = lax.broadcasted_iota(jnp.int32, (tm, 1), 0)
    step = jnp.zeros_like(row)
    for k in range(1, steps):
        step = step + (row >= k * shift).astype(jnp.int32)
    pos = pos0 + step

    for gi, win in enumerate(POOL_WINDOWS):
        @pl.when(grp == gi)
        def _(gi=gi, win=win):
            cols = slice(gi * gw, (gi + 1) * gw)
            h = hs_ref[:, cols]
            prev = jnp.concatenate([jnp.zeros((shift, gw), F32), state_ref[...]], axis=0)
            o_ref[...] = _pool_group(h, prev, _time_major(x_ref, cols), w_ref[...], sc_ref[...],
                                     pos, win, shift)
            ext = jnp.concatenate([prev, h], axis=0)
            _store_batch_major(ns_ref, ext[ext.shape[0] - kept * shift:, :])


def _pool_mixer(x, halo, g, w, scale, *, shift, pos0, rows_per_seq):
    d = x.shape[-1]
    ng = len(POOL_WINDOWS)
    gw = d // ng
    if halo is None:
        n = x.shape[0]
        tm = min(WIDE_TILE, n)
        assert shift == 1 and pos0 == 0
        tiles_per_seq = rows_per_seq // tm
        halo_rows = POOL_BUF + 1
        full = lambda a: pl.BlockSpec(a.shape, lambda i: (0,) * a.ndim)
        tile = pl.BlockSpec((tm, d), lambda i: (i, 0))
        return pl.pallas_call(
            functools.partial(_pool_prompt_kernel, tiles_per_seq=tiles_per_seq),
            grid=(n // tm,),
            in_specs=[tile, full(g), full(w), full(scale)],
            out_specs=[tile, pl.BlockSpec((None, halo_rows, d), lambda i: (i // tiles_per_seq, 0, 0))],
            out_shape=[jax.ShapeDtypeStruct((n, d), F32),
                       jax.ShapeDtypeStruct((n // rows_per_seq, halo_rows, d), F32)],
            scratch_shapes=[pltpu.VMEM((halo_rows, d), F32)],
            compiler_params=_params(1),
            name="pool_mixer",
        )(x, g, w, scale)
    nb, steps, _ = x.shape
    kept = halo.shape[0] // nb
    assert shift == nb and kept == POOL_BUF
    return pl.pallas_call(
        functools.partial(_pool_state_kernel, pos0=pos0),
        grid=(ng,),
        in_specs=[pl.BlockSpec((nb, steps, d), lambda gi: (0, 0, 0)),
                  pl.BlockSpec((kept * nb, gw), lambda gi: (0, gi)),
                  pl.BlockSpec((1, d), lambda gi: (0, 0)),
                  pl.BlockSpec((None, gw, gw), lambda gi: (gi, 0, 0)),
                  pl.BlockSpec((1, gw), lambda gi: (0, gi))],
        out_specs=[pl.BlockSpec((nb * steps, gw), lambda gi: (0, gi)),
                   pl.BlockSpec((nb, kept, gw), lambda gi: (0, 0, gi))],
        out_shape=[jax.ShapeDtypeStruct((nb * steps, d), F32), jax.ShapeDtypeStruct((nb, kept, d), F32)],
        scratch_shapes=[pltpu.VMEM((nb * steps, d), F32)],
        compiler_params=_params(1),
        name="pool_mixer_state",
    )(x, halo, g, w, scale)


def _ffn_kernel(*refs, shift, tiles_per_seq, from_state):
    if from_state:
        (x_ref, g_ref, sg_ref, su_ref, wg_ref, wu_ref, cpg_ref, cpu_ref, wd_ref,
         o_ref, tail_ref, hs_ref) = refs
        carry_g = carry_u = None
    else:
        (x_hbm, g_ref, wg_ref, wu_ref, cpg_ref, cpu_ref, wd_ref,
         o_ref, tail_ref, hs_ref, carry_g, carry_u, x_ref, x_sem) = refs
        sg_ref = su_ref = None
    tm, d = x_ref.shape
    halo = max((CONV_W - 1) * shift, SUBLANES)
    i = pl.program_id(0)
    f = pl.program_id(1)
    first_of_seq = (i % tiles_per_seq) == 0

    wait_x = None if from_state else _tile_fetch(x_hbm, x_ref, x_sem)

    @pl.when(f == 0)
    def _():
        if wait_x is not None:
            wait_x()
        x = x_ref[...]
        hs_ref[...] = _rmsnorm(x, g_ref[...]).astype(BF16)
        o_ref[...] = x

    rb = max(min(ROW_BLOCK, tm // 2), halo)
    cpg, cpu = cpg_ref[...], cpu_ref[...]
    if from_state:
        prev_g, prev_u = _time_major(sg_ref), _time_major(su_ref)
    else:
        prev_g = jnp.where(first_of_seq, 0.0, carry_g[f])
        prev_u = jnp.where(first_of_seq, 0.0, carry_u[f])

    def up(r):
        hs = hs_ref[r:r + rb, :]
        return (jnp.dot(hs, wg_ref[...], preferred_element_type=F32),
                jnp.dot(hs, wu_ref[...], preferred_element_type=F32))

    def conv(prev, u, cp):
        ext = jnp.concatenate([prev, u], axis=0)
        if shift % SUBLANES == 0:
            u1 = ext[halo - shift:halo - shift + rb, :]
            u2 = ext[halo - 2 * shift:halo - 2 * shift + rb, :]
        else:
            u1 = _shift_rows(ext, shift)[halo:, :]
            u2 = _shift_rows(ext, 2 * shift)[halo:, :]
        return cp[0:1, :] * u2 + cp[1:2, :] * u1 + cp[2:3, :] * u + cp[3:4, :]

    nxt = up(0)
    for r in range(0, tm, rb):
        ug, uu = nxt
        if r + rb < tm:
            nxt = up(r + rb)
        cg = conv(prev_g, ug, cpg)
        cu = conv(prev_u, uu, cpu)
        prev_g, prev_u = ug[rb - halo:, :], uu[rb - halo:, :]
        act = ((cg / (1.0 + jnp.exp(-cg))) * cu).astype(BF16)
        o_ref[r:r + rb, :] += jnp.dot(act, wd_ref[...], preferred_element_type=F32)
    if from_state:
        _store_batch_major(tail_ref.at[0], prev_g)
        _store_batch_major(tail_ref.at[1], prev_u)
    else:
        carry_g[f] = prev_g
        carry_u[f] = prev_u
        tail_ref[0] = prev_g
        tail_ref[1] = prev_u


def _conv_ffn(x, g, state, w_up, conv_params, w_down, layer, *, shift, rows_per_seq):
    n, d = x.shape
    f2 = w_up.shape[2]
    ffn = f2 // 2
    tm = min(WIDE_TILE, n)
    fn = FFN_CHUNK
    nf = ffn // fn
    n_tiles = n // tm
    tiles_per_seq = max(rows_per_seq // tm, 1)
    halo = max((CONV_W - 1) * shift, SUBLANES)
    from_state = state is not None

    tile = pl.BlockSpec((tm, d), lambda i, f: (i, 0))
    tile_in = tile if from_state else pl.BlockSpec(memory_space=pl.ANY)
    layer_gate = lambda rows: pl.BlockSpec((None, rows, fn), lambda i, f: (layer, 0, f))
    layer_up = lambda rows: pl.BlockSpec((None, rows, fn), lambda i, f: (layer, 0, f + nf))
    in_specs = [tile_in, pl.BlockSpec((1, d), lambda i, f: (0, 0))]
    args = [x, g]
    scratch = [pltpu.VMEM((tm, d), BF16)]
    if from_state:
        nb, steps = state.shape[1], state.shape[2]
        assert n == tm and nb == shift and halo == steps * shift
        in_specs += [pl.BlockSpec((None, nb, steps, fn), lambda i, f: (layer, 0, 0, f)),
                     pl.BlockSpec((None, nb, steps, fn), lambda i, f: (layer, 0, 0, f + nf))]
        args += [state, state]
        tail_spec = pl.BlockSpec((2, nb, steps, fn), lambda i, f: (0, 0, 0, f))
        tail_shape = jax.ShapeDtypeStruct((2, nb, steps, ffn), F32)
    else:
        tail_spec = pl.BlockSpec((None, 2, halo, fn), lambda i, f: (i, 0, 0, f))
        tail_shape = jax.ShapeDtypeStruct((n_tiles, 2, halo, ffn), F32)
        scratch += [pltpu.VMEM((nf, halo, fn), F32), pltpu.VMEM((nf, halo, fn), F32),
                    pltpu.VMEM((tm, d), F32), pltpu.SemaphoreType.DMA(())]
    rows = conv_params.shape[1]
    in_specs += [layer_gate(d), layer_up(d), layer_gate(rows), layer_up(rows),
                 pl.BlockSpec((None, fn, d), lambda i, f: (layer, f, 0))]
    args += [w_up, w_up, conv_params, conv_params, w_down]
    return pl.pallas_call(
        functools.partial(_ffn_kernel, shift=shift, tiles_per_seq=tiles_per_seq, from_state=from_state),
        grid=(n_tiles, nf),
        in_specs=in_specs,
        out_specs=[tile, tail_spec],
        out_shape=[jax.ShapeDtypeStruct((n, d), F32), tail_shape],
        scratch_shapes=scratch,
        compiler_params=_params(2),
        name="conv_ffn_state" if from_state else "conv_ffn",
    )(*args)


def _ple_kernel(*refs, final, casts):
    if casts:
        *refs, cast_out = refs
        x_ref, p_ref, g_ref, wg_ref, wp_ref, cast_in, o_ref = refs
        cast_out[...] = cast_in[...].astype(BF16)
    elif final:
        x_ref, p_ref, g_ref, wg_ref, wp_ref, gf_ref, o_ref = refs
    else:
        x_ref, p_ref, g_ref, wg_ref, wp_ref, o_ref = refs
    x = x_ref[...]
    hn = _rmsnorm(x, g_ref[...]).astype(BF16)
    gate = jax.nn.sigmoid(jnp.dot(hn, wg_ref[...], preferred_element_type=F32))
    emb = jnp.dot(p_ref[...].astype(BF16), wp_ref[...], preferred_element_type=F32)
    y = x + gate * emb
    if final:
        y = _rmsnorm(y, gf_ref[...])
    if len(o_ref.shape) == 3:
        _store_batch_major(o_ref, y)
    else:
        o_ref[...] = y


def _ple(x, p, g, w_gate, w_proj, layer, g_final, steps=None, cast=None):
    n, d = x.shape
    tm = min(TOKEN_TILE, n)
    final = g_final is not None
    assert not (final and cast)
    if steps is None:
        out_spec = pl.BlockSpec((tm, d), lambda i: (i, 0))
        out_shape = jax.ShapeDtypeStruct((n, d), F32)
    else:
        assert n == tm
        out_spec = pl.BlockSpec((n // steps, steps, d), lambda i: (0, 0, 0))
        out_shape = jax.ShapeDtypeStruct((n // steps, steps, d), F32)
    const = lambda a: pl.BlockSpec(a.shape, lambda i: (0, 0))
    resident = lambda a: pl.BlockSpec((None,) + a.shape[1:], lambda i: (layer, 0, 0),
                                      pipeline_mode=pl.Buffered(1))
    in_specs = [pl.BlockSpec((tm, d), lambda i: (i, 0)),
                pl.BlockSpec((None, tm, p.shape[2]), lambda i: (layer, i, 0)),
                const(g), resident(w_gate), resident(w_proj)]
    args = [x, p, g, w_gate, w_proj]
    if final:
        in_specs.append(const(g_final))
        args.append(g_final)
    out_specs, out_shapes = [out_spec], [out_shape]
    if cast:
        w, j = cast
        rows, cols = w.shape[1] // (n // tm), w.shape[2]
        assert rows * (n // tm) == w.shape[1]
        in_specs.append(pl.BlockSpec((None, rows, cols), lambda i: (j, i, 0)))
        args.append(w)
        out_specs.append(pl.BlockSpec((rows, cols), lambda i: (i, 0)))
        out_shapes.append(jax.ShapeDtypeStruct(w.shape[1:], BF16))
    out = pl.pallas_call(
        functools.partial(_ple_kernel, final=final, casts=bool(cast)),
        grid=(n // tm,),
        in_specs=in_specs,
        out_specs=out_specs,
        out_shape=out_shapes,
        compiler_params=_params(1),
        name="ple_final" if final else "ple",
    )(*args)
    return out if cast else out[0]


def _ret_in_kernel(x_in, g_ref, w_ref, cos_ref, sin_ref, o_ref, hs_ref, *fetch, dk, layout):
    if fetch:
        x_ref = fetch[0]
        wait_x = _tile_fetch(x_in, *fetch)
    else:
        x_ref, wait_x = x_in, None

    @pl.when(pl.program_id(1) == 0)
    def _():
        if wait_x is not None:
            wait_x()
        hs_ref[...] = _rmsnorm(x_ref[...], g_ref[...]).astype(BF16)

    tm = hs_ref.shape[0]
    rb = min(ROW_BLOCK, tm)
    half = dk // 2
    project = lambda r: jnp.dot(hs_ref[r:r + rb, :], w_ref[...], preferred_element_type=F32)
    nxt = project(0)
    for r in range(0, tm, rb):
        proj = nxt
        if r + rb < tm:
            nxt = project(r + rb)
        cos = cos_ref[r:r + rb, :]
        sin = sin_ref[r:r + rb, :]
        for hh in range(proj.shape[1] // dk):
            x1 = proj[:, hh * dk:hh * dk + half]
            x2 = proj[:, hh * dk + half:(hh + 1) * dk]
            y1 = (x1 * cos - x2 * sin).astype(o_ref.dtype)
            y2 = (x1 * sin + x2 * cos).astype(o_ref.dtype)
            if layout == "heads":
                o_ref[hh, r:r + rb, 0:half] = y1
                o_ref[hh, r:r + rb, half:dk] = y2
            else:
                assert rb == tm
                _store_batch_major(o_ref, y1, slice(hh * dk, hh * dk + half))
                _store_batch_major(o_ref, y2, slice(hh * dk + half, (hh + 1) * dk))


def _ret_in(x, g, w_in, tables, *, qk_width, dk, rows_per_seq, out_dtype, steps=None):
    n, d = x.shape
    e = w_in.shape[1]
    tm = min(PROJ_TILE, n)
    nc = PROJ_CHUNK
    qc = qk_width // nc
    tps = rows_per_seq // tm

    def table_block(i, c):
        return (jnp.where(c < qc, 0, jnp.where(c < 2 * qc, tps, 2 * tps)) + i % tps, 0)

    scratch = [pltpu.VMEM((tm, d), BF16)]
    if steps is None:
        x_spec = pl.BlockSpec(memory_space=pl.ANY)
        w_spec = pl.BlockSpec((d, nc), lambda i, c: (0, c))
        out_spec = pl.BlockSpec((nc // dk, tm, dk), lambda i, c: (c, i, 0))
        out_shape = jax.ShapeDtypeStruct((e // dk, n, dk), out_dtype)
        scratch += [pltpu.VMEM((tm, d), F32), pltpu.SemaphoreType.DMA(())]
    else:
        assert n == tm
        x_spec = pl.BlockSpec((tm, d), lambda i, c: (i, 0))
        w_spec = pl.BlockSpec((d, nc), lambda i, c: (0, c))
        out_spec = pl.BlockSpec((n // steps, steps, nc), lambda i, c: (0, 0, c))
        out_shape = jax.ShapeDtypeStruct((n // steps, steps, e), out_dtype)

    return pl.pallas_call(
        functools.partial(_ret_in_kernel, dk=dk, layout="heads" if steps is None else "batch"),
        grid=(n // tm, e // nc),
        in_specs=[x_spec,
                  pl.BlockSpec((1, d), lambda i, c: (0, 0)),
                  w_spec,
                  pl.BlockSpec((tm, dk // 2), table_block),
                  pl.BlockSpec((tm, dk // 2), table_block)],
        out_specs=out_spec,
        out_shape=out_shape,
        scratch_shapes=scratch,
        compiler_params=_params(2),
        name="ret_in",
    )(x, g, w_in, *tables)


def _groupnorm_gate(o, gate):
    mu = jnp.mean(o, axis=-1, keepdims=True)
    var = jnp.mean(jnp.square(o - mu), axis=-1, keepdims=True)
    on = (o - mu) * lax.rsqrt(var + EPS)
    gf = gate.astype(F32)
    return (gf / (1.0 + jnp.exp(-gf))) * on


def _ret_prompt_kernel(q_ref, k_ref, v_ref, g_ref, intra_ref, cross_ref, kdec_ref, sdec_ref,
                       o_ref, s_out_ref, s_ref, *, chunk):
    hp, t, _ = q_ref.shape
    per = v_ref.shape[0] // hp
    s_ref[...] = jnp.zeros(s_ref.shape, F32)

    def wide(ref, j, rows):
        return jnp.concatenate([ref[j * per + i, rows, :] for i in range(per)], axis=-1)

    def body(c, carry):
        rows = pl.ds(pl.multiple_of(c * chunk, chunk), chunk)
        for j in range(hp):
            q = q_ref[j, rows, :]
            k = k_ref[j, rows, :]
            v = wide(v_ref, j, rows)
            s = s_ref[j]
            sc = lax.dot_general(q, k, (((1,), (1,)), ((), ())), preferred_element_type=F32) * intra_ref[j]
            o = (jnp.dot(sc.astype(BF16), v, preferred_element_type=F32)
                 + jnp.dot(q, s.astype(BF16), preferred_element_type=F32) * cross_ref[j])
            kd = (k.astype(F32) * kdec_ref[j]).astype(BF16)
            s_ref[j] = sdec_ref[j] * s + lax.dot_general(kd, v, (((0,), (0,)), ((), ())),
                                                         preferred_element_type=F32)
            o_ref[j, rows, :] = _groupnorm_gate(o, wide(g_ref, j, rows)).astype(o_ref.dtype)
        return carry

    lax.fori_loop(0, t // chunk, body, 0, unroll=2)
    s_out_ref[...] = s_ref[...]


def _ret_prompt(proj, consts, *, batch, seq, heads, dk, dv):
    intra, cross, kdec, sdec = consts
    chunk = intra.shape[1]
    hp = RET_HEAD_GROUP
    groups = heads // hp
    per = dv // dk
    v_off = 2 * groups // per
    g_off = v_off + groups
    head_const = lambda a: pl.BlockSpec((hp,) + a.shape[1:], lambda b, h: (h, 0, 0))
    return pl.pallas_call(
        functools.partial(_ret_prompt_kernel, chunk=chunk),
        grid=(batch, groups),
        in_specs=[pl.BlockSpec((hp, seq, dk), lambda b, h: (h, b, 0)),
                  pl.BlockSpec((hp, seq, dk), lambda b, h: (groups + h, b, 0)),
                  pl.BlockSpec((hp * per, seq, dk), lambda b, h: (v_off + h, b, 0)),
                  pl.BlockSpec((hp * per, seq, dk), lambda b, h: (g_off + h, b, 0)),
                  head_const(intra), head_const(cross), head_const(kdec), head_const(sdec)],
        out_specs=[pl.BlockSpec((hp, seq, dv), lambda b, h: (h, b, 0)),
                   pl.BlockSpec((None, hp, dk, dv), lambda b, h: (b, h, 0, 0))],
        out_shape=[jax.ShapeDtypeStruct((heads, batch * seq, dv), BF16),
                   jax.ShapeDtypeStruct((batch, heads, dk, dv), F32)],
        scratch_shapes=[pltpu.VMEM((hp, dk, dv), F32)],
        compiler_params=_params(2),
        name="retention_prompt",
    )(proj, proj, proj, proj, intra, cross, kdec, sdec)


def _ret_sample_kernel(p_ref, s_ref, intra_ref, cross_ref, kdec_ref, sdec_ref, o_ref, s_out_ref,
                       *, heads, dk, dv):
    qk = heads * dk
    for n in range(p_ref.shape[0]):
        for h in range(heads):
            q = p_ref[n, :, h * dk:(h + 1) * dk].astype(BF16)
            kf = p_ref[n, :, qk + h * dk:qk + (h + 1) * dk]
            v = p_ref[n, :, 2 * qk + h * dv:2 * qk + (h + 1) * dv].astype(BF16)
            gate = p_ref[n, :, 2 * qk + heads * dv + h * dv:2 * qk + heads * dv + (h + 1) * dv]
            s = s_ref[n, h]
            sc = lax.dot_general(q, kf.astype(BF16), (((1,), (1,)), ((), ())),
                                 preferred_element_type=F32) * intra_ref[h]
            o = (jnp.dot(sc.astype(BF16), v, preferred_element_type=F32)
                 + jnp.dot(q, s.astype(BF16), preferred_element_type=F32) * cross_ref[h])
            kd = (kf * kdec_ref[h]).astype(BF16)
            s_out_ref[n, h] = sdec_ref[h] * s + lax.dot_general(kd, v, (((0,), (0,)), ((), ())),
                                                                preferred_element_type=F32)
            o_ref[n, :, h * dv:(h + 1) * dv] = _groupnorm_gate(o, gate).astype(o_ref.dtype)


def _ret_sample(proj, state, consts, *, heads, dk, dv):
    intra, cross, kdec, sdec = consts
    b, tp, e = proj.shape
    nb = SAMPLE_SEQS
    const = lambda a: pl.BlockSpec(a.shape, lambda i: (0,) * a.ndim)
    return pl.pallas_call(
        functools.partial(_ret_sample_kernel, heads=heads, dk=dk, dv=dv),
        grid=(b // nb,),
        in_specs=[pl.BlockSpec((nb, tp, e), lambda i: (i, 0, 0)),
                  pl.BlockSpec((nb, heads, dk, dv), lambda i: (i, 0, 0, 0)),
                  const(intra), const(cross), const(kdec), const(sdec)],
        out_specs=[pl.BlockSpec((nb, tp, heads * dv), lambda i: (i, 0, 0)),
                   pl.BlockSpec((nb, heads, dk, dv), lambda i: (i, 0, 0, 0))],
        out_shape=[jax.ShapeDtypeStruct((b, tp, heads * dv), F32),
                   jax.ShapeDtypeStruct((b, heads, dk, dv), F32)],
        compiler_params=_params(1),
        name="retention_sample",
    )(proj, state, intra, cross, kdec, sdec)


def _ret_out_kernel(x_ref, a_ref, w_ref, o_ref, *, head_major):
    if head_major:
        dv = a_ref.shape[2]
        y = x_ref[...]
        for h in range(a_ref.shape[0]):
            y = y + jnp.dot(a_ref[h], w_ref[h * dv:(h + 1) * dv, :], preferred_element_type=F32)
        o_ref[...] = y
    else:
        a = _time_major(a_ref).astype(BF16)
        o_ref[...] = x_ref[...] + jnp.dot(a, w_ref[...], preferred_element_type=F32)


def _ret_out(x, a, w_out, *, head_major):
    n, d = x.shape
    tm = min(TOKEN_TILE, n)
    if head_major:
        a_spec = pl.BlockSpec((a.shape[0], tm, a.shape[2]), lambda i: (0, i, 0))
    else:
        assert n == tm
        a_spec = pl.BlockSpec(a.shape, lambda i: (0, 0, 0))
    v = w_out.shape[0]
    return pl.pallas_call(
        functools.partial(_ret_out_kernel, head_major=head_major),
        grid=(n // tm,),
        in_specs=[pl.BlockSpec((tm, d), lambda i: (i, 0)),
                  a_spec,
                  pl.BlockSpec((v, d), lambda i: (0, 0), pipeline_mode=pl.Buffered(1))],
        out_specs=pl.BlockSpec((tm, d), lambda i: (i, 0)),
        out_shape=jax.ShapeDtypeStruct((n, d), F32),
        compiler_params=_params(1),
        name="ret_out",
    )(x, a, w_out)


def _decay_consts(chunk, heads, dk, dv):
    log_g = np.log1p(-(2.0 ** (-5.0 - np.arange(heads, dtype=np.float64))))
    n = np.arange(chunk, dtype=np.float64)
    diff = n[:, None] - n[None, :]
    intra = np.where(diff >= 0, np.exp(log_g[:, None, None] * np.maximum(diff, 0.0)), 0.0)
    cross = np.exp(log_g[:, None] * (n + 1.0))
    kdec = np.exp(log_g[:, None] * (chunk - 1.0 - n))
    sdec = np.exp(log_g * chunk)
    f32 = lambda a: jnp.asarray(np.ascontiguousarray(a), dtype=F32)
    return (f32(intra),
            f32(np.broadcast_to(cross[:, :, None], (heads, chunk, dv))),
            f32(np.broadcast_to(kdec[:, :, None], (heads, chunk, dk))),
            f32(np.broadcast_to(sdec[:, None, None], (heads, 1, dv))))


def _rotary_tables(pos, dk):
    half = dk // 2
    inv = ROPE_BASE ** (-np.arange(half, dtype=np.float64) / half)
    ang = np.asarray(pos, dtype=np.float64)[:, None] * inv[None, :]
    cos, sin = np.cos(ang), np.sin(ang)
    scale = dk ** -0.5
    f32 = lambda a: jnp.asarray(a, dtype=F32)
    return (f32(np.concatenate([cos, cos * scale, np.ones_like(cos)])),
            f32(np.concatenate([sin, sin * scale, np.zeros_like(sin)])))


def kernel(x_prompt, x_sample, p_prompt, p_sample, state_pool, state_ret, state_conv, norm_mix, norm_ffn,
           norm_ple, norm_final, pool_w, pool_scale, ret_w_in, ret_w_out, ffn_w_up, ffn_conv_w,
           ffn_conv_b, ffn_w_down, ple_w_proj, ple_w_gate):
    b, t, d = x_prompt.shape
    bs, ts, _ = x_sample.shape
    depth = norm_mix.shape[0]
    heads = RET_HEADS
    dk = d // heads
    dv = 2 * dk
    qk = heads * dk
    f2 = ffn_w_up.shape[2]
    ffn = f2 // 2
    assert ts >= CONV_W - 1 and PAST_LEN >= POOL_BUF

    row2 = lambda a: a.reshape(1, -1)
    t_major = lambda a: jnp.swapaxes(a, 0, 1).reshape(-1, a.shape[-1])
    b_major = lambda a: jnp.swapaxes(a.reshape(-1, bs, a.shape[-1]), 0, 1)

    xp = x_prompt.reshape(b * t, d)
    xs = t_major(x_sample)
    conv_params = jnp.concatenate([ffn_conv_w, ffn_conv_b[:, None, :]], axis=1)
    pp = p_prompt.reshape(depth, b * t, -1)
    ps = jnp.swapaxes(p_sample, 1, 2).reshape(depth, ts * bs, -1)

    tables_p = _rotary_tables(np.arange(t), dk)
    tables_s = _rotary_tables(PAST_LEN + np.repeat(np.arange(ts), bs), dk)
    chunk_p = RET_CHUNK if t % RET_CHUNK == 0 else t
    consts_p = _decay_consts(chunk_p, heads, dk, dv)
    consts_s = _decay_consts(ts, heads, dk, dv)

    new_pool_p, new_pool_s, new_ret_p, new_ret_s, new_conv_p, new_conv_s = [], [], [], [], [], []
    w_out_bf16 = None
    for i in range(depth):
        jm = i // 2
        if i % 2 == 0:
            w = pool_w[jm]
            g = row2(norm_mix[i])
            sc = row2(pool_scale[jm])
            xp, hl = _pool_mixer(xp, None, g, w, sc, shift=1, pos0=0, rows_per_seq=t)
            new_pool_p.append(hl[:, -POOL_BUF:, :])
            xs3 = x_sample if i == 0 else b_major(xs)
            xs, ns = _pool_mixer(xs3, t_major(state_pool[jm]), g, w, sc, shift=bs, pos0=PAST_LEN,
                                 rows_per_seq=ts * bs)
            new_pool_s.append(ns)
        else:
            w_in = ret_w_in[jm]
            w_out = w_out_bf16 if w_out_bf16 is not None else ret_w_out[jm].astype(BF16)
            w_out_bf16 = None
            g = row2(norm_mix[i])
            proj_p = _ret_in(xp, g, w_in, tables_p, qk_width=qk, dk=dk, rows_per_seq=t, out_dtype=BF16)
            a_p, s_p = _ret_prompt(proj_p, consts_p, batch=b, seq=t, heads=heads, dk=dk, dv=dv)
            xp = _ret_out(xp, a_p, w_out, head_major=True)
            new_ret_p.append(s_p)
            proj_s = _ret_in(xs, g, w_in, tables_s, qk_width=qk, dk=dk, rows_per_seq=ts * bs, out_dtype=F32,
                             steps=ts)
            a_s, s_s = _ret_sample(proj_s, state_ret[jm], consts_s, heads=heads, dk=dk, dv=dv)
            xs = _ret_out(xs, a_s, w_out, head_major=False)
            new_ret_s.append(s_s)

        g = row2(norm_ffn[i])
        xp, tails = _conv_ffn(xp, g, None, ffn_w_up, conv_params, ffn_w_down, i, shift=1, rows_per_seq=t)
        tiles_per_seq = tails.shape[0] // b
        last = tails[tiles_per_seq - 1::tiles_per_seq, :, -(CONV_W - 1):, :]
        new_conv_p.append(jnp.concatenate([last[:, 0], last[:, 1]], axis=-1))
        xs, tails = _conv_ffn(xs, g, state_conv, ffn_w_up, conv_params, ffn_w_down, i, shift=bs,
                              rows_per_seq=ts * bs)
        new_conv_s.append(jnp.concatenate([tails[0], tails[1]], axis=-1))

        g = row2(norm_ple[i])
        last_layer = i == depth - 1
        g_final = row2(norm_final) if last_layer else None
        if not last_layer and (i + 1) % 2 == 1:
            xp, w_out_bf16 = _ple(xp, pp, g, ple_w_gate, ple_w_proj, i, g_final, cast=(ret_w_out, (i + 1) // 2))
        else:
            xp = _ple(xp, pp, g, ple_w_gate, ple_w_proj, i, g_final)
        xs = _ple(xs, ps, g, ple_w_gate, ple_w_proj, i, g_final, steps=ts if last_layer else None)

    return (xp.reshape(b, t, d), xs,
            jnp.stack(new_pool_p), jnp.stack(new_pool_s),
            jnp.stack(new_ret_p), jnp.stack(new_ret_s),
            jnp.stack(new_conv_p), jnp.stack(new_conv_s))
```

```python
import functools

import jax
import jax.numpy as jnp
import numpy as np
from jax import lax
from jax.experimental import pallas as pl
from jax.experimental.pallas import tpu as pltpu

F32 = jnp.float32
BF16 = jnp.bfloat16

EPS = 1e-6
POOL_WINDOWS = (2, 4, 8, 16)
POOL_BUF = max(POOL_WINDOWS) - 1
RET_HEADS = 8
RET_CHUNK = 256
ROPE_BASE = 10000.0
CONV_W = 3
PAST_LEN = 16384

TOKEN_TILE = 512
WIDE_TILE = 1024
PROJ_TILE = 2048
FFN_CHUNK = 512
ROW_BLOCK = 512
PROJ_CHUNK = 1024
RET_HEAD_GROUP = 2
SAMPLE_SEQS = 2
SUBLANES = 8
VMEM_LIMIT = 60 * 1024 * 1024


def _params(n_axes):
    return pltpu.CompilerParams(
        dimension_semantics=("arbitrary",) * n_axes, vmem_limit_bytes=VMEM_LIMIT)


def _rmsnorm(x, g):
    return x * lax.rsqrt(jnp.mean(x * x, axis=-1, keepdims=True) + EPS) * g


def _shift_rows(a, rows):
    return pltpu.roll(a, rows, axis=0)


def _tile_fetch(x_hbm, x_ref, sem):
    tm = x_ref.shape[0]
    i = pl.program_id(0)
    f = pl.program_id(1)

    def copy(tile):
        return pltpu.make_async_copy(x_hbm.at[pl.ds(tile * tm, tm), :], x_ref, sem)

    @pl.when(jnp.logical_and(f == 0, i == 0))
    def _():
        copy(0).start()

    @pl.when(jnp.logical_and(f == 1, i + 1 < pl.num_programs(0)))
    def _():
        copy(i + 1).start()

    return lambda: copy(i).wait()


def _time_major(ref, cols=slice(None)):
    return jnp.concatenate([ref[:, t, cols] for t in range(ref.shape[1])], axis=0)


def _store_batch_major(ref, val, cols=slice(None)):
    nb = ref.shape[0]
    for t in range(ref.shape[1]):
        ref[:, t, cols] = val[t * nb:(t + 1) * nb, :]


def _pool_group(h, prev, x, w, sc, pos, win, shift):
    halo = prev.shape[0]
    a = jnp.concatenate([prev, h], axis=0)
    span = 1
    while span < win:
        a = a + _shift_rows(a, span * shift)
        span *= 2
    cnt = jnp.minimum(pos + 1, win).astype(F32)
    dlt = a[halo:, :] / cnt - h
    return x + jnp.dot(dlt.astype(BF16), w, preferred_element_type=F32) * sc


def _pool_prompt_kernel(x_ref, g_ref, w_ref, sc_ref, o_ref, h_ref, carry_ref, *, tiles_per_seq):
    tm, d = x_ref.shape
    gw = d // len(POOL_WINDOWS)
    halo = carry_ref.shape[0]
    j = pl.program_id(0) % tiles_per_seq
    x = x_ref[...]
    h = _rmsnorm(x, g_ref[...])
    last = h[tm - halo:, :]
    h_ref[...] = last
    pos = j * tm + lax.broadcasted_iota(jnp.int32, (tm, 1), 0)
    for gi, win in enumerate(POOL_WINDOWS):
        cols = slice(gi * gw, (gi + 1) * gw)
        prev = jnp.where(j == 0, 0.0, carry_ref[:, cols])
        o_ref[:, cols] = _pool_group(h[:, cols], prev, x[:, cols], w_ref[gi], sc_ref[:, cols], pos, win, 1)
    carry_ref[...] = last


def _pool_state_kernel(x_ref, state_ref, g_ref, w_ref, sc_ref, o_ref, ns_ref, hs_ref, *, pos0):
    shift, steps, d = x_ref.shape
    tm = shift * steps
    gw = o_ref.shape[1]
    kept = ns_ref.shape[1]
    grp = pl.program_id(0)

    @pl.when(grp == 0)
    def _():
        hs_ref[...] = _rmsnorm(_time_major(x_ref), g_ref[...])

    row = lax.broadcasted_iota(jnp.int32, (tm, 1), 0)
    step = jnp.zeros_like(row)
    for k in range(1, steps):
        step = step + (row >= k * shift).astype(jnp.int32)
    pos = pos0 + step

    for gi, win in enumerate(POOL_WINDOWS):
        @pl.when(grp == gi)
        def _(gi=gi, win=win):
            cols = slice(gi * gw, (gi + 1) * gw)
            h = hs_ref[:, cols]
            prev = jnp.concatenate([jnp.zeros((shift, gw), F32), state_ref[...]], axis=0)
            o_ref[...] = _pool_group(h, prev, _time_major(x_ref, cols), w_ref[...], sc_ref[...],
                                     pos, win, shift)
            ext = jnp.concatenate([prev, h], axis=0)
            _store_batch_major(ns_ref, ext[ext.shape[0] - kept * shift:, :])


def _pool_mixer(x, halo, g, w, scale, *, shift, pos0, rows_per_seq):
    d = x.shape[-1]
    ng = len(POOL_WINDOWS)
    gw = d // ng
    if halo is None:
        n = x.shape[0]
        tm = min(WIDE_TILE, n)
        assert shift == 1 and pos0 == 0
        tiles_per_seq = rows_per_seq // tm
        halo_rows = POOL_BUF + 1
        full = lambda a: pl.BlockSpec(a.shape, lambda i: (0,) * a.ndim)
        tile = pl.BlockSpec((tm, d), lambda i: (i, 0))
        return pl.pallas_call(
            functools.partial(_pool_prompt_kernel, tiles_per_seq=tiles_per_seq),
            grid=(n // tm,),
            in_specs=[tile, full(g), full(w), full(scale)],
            out_specs=[tile, pl.BlockSpec((None, halo_rows, d), lambda i: (i // tiles_per_seq, 0, 0))],
            out_shape=[jax.ShapeDtypeStruct((n, d), F32),
                       jax.ShapeDtypeStruct((n // rows_per_seq, halo_rows, d), F32)],
            scratch_shapes=[pltpu.VMEM((halo_rows, d), F32)],
            compiler_params=_params(1),
            name="pool_mixer",
        )(x, g, w, scale)
    nb, steps, _ = x.shape
    kept = halo.shape[0] // nb
    assert shift == nb and kept == POOL_BUF
    return pl.pallas_call(
        functools.partial(_pool_state_kernel, pos0=pos0),
        grid=(ng,),
        in_specs=[pl.BlockSpec((nb, steps, d), lambda gi: (0, 0, 0)),
                  pl.BlockSpec((kept * nb, gw), lambda gi: (0, gi)),
                  pl.BlockSpec((1, d), lambda gi: (0, 0)),
                  pl.BlockSpec((None, gw, gw), lambda gi: (gi, 0, 0)),
                  pl.BlockSpec((1, gw), lambda gi: (0, gi))],
        out_specs=[pl.BlockSpec((nb * steps, gw), lambda gi: (0, gi)),
                   pl.BlockSpec((nb, kept, gw), lambda gi: (0, 0, gi))],
        out_shape=[jax.ShapeDtypeStruct((nb * steps, d), F32), jax.ShapeDtypeStruct((nb, kept, d), F32)],
        scratch_shapes=[pltpu.VMEM((nb * steps, d), F32)],
        compiler_params=_params(1),
        name="pool_mixer_state",
    )(x, halo, g, w, scale)


def _ffn_kernel(*refs, shift, tiles_per_seq, from_state):
    if from_state:
        (x_ref, g_ref, sg_ref, su_ref, wg_ref, wu_ref, cpg_ref, cpu_ref, wd_ref,
         o_ref, tail_ref, hs_ref) = refs
        carry_g = carry_u = None
    else:
        (x_hbm, g_ref, wg_ref, wu_ref, cpg_ref, cpu_ref, wd_ref,
         o_ref, tail_ref, hs_ref, carry_g, carry_u, x_ref, x_sem) = refs
        sg_ref = su_ref = None
    tm, d = x_ref.shape
    halo = max((CONV_W - 1) * shift, SUBLANES)
    i = pl.program_id(0)
    f = pl.program_id(1)
    first_of_seq = (i % tiles_per_seq) == 0

    wait_x = None if from_state else _tile_fetch(x_hbm, x_ref, x_sem)

    @pl.when(f == 0)
    def _():
        if wait_x is not None:
            wait_x()
        x = x_ref[...]
        hs_ref[...] = _rmsnorm(x, g_ref[...]).astype(BF16)
        o_ref[...] = x

    rb = max(min(ROW_BLOCK, tm // 2), halo)
    cpg, cpu = cpg_ref[...], cpu_ref[...]
    if from_state:
        prev_g, prev_u = _time_major(sg_ref), _time_major(su_ref)
    else:
        prev_g = jnp.where(first_of_seq, 0.0, carry_g[f])
        prev_u = jnp.where(first_of_seq, 0.0, carry_u[f])

    def up(r):
        hs = hs_ref[r:r + rb, :]
        return (jnp.dot(hs, wg_ref[...], preferred_element_type=F32),
                jnp.dot(hs, wu_ref[...], preferred_element_type=F32))

    def conv(prev, u, cp):
        ext = jnp.concatenate([prev, u], axis=0)
        if shift % SUBLANES == 0:
            u1 = ext[halo - shift:halo - shift + rb, :]
            u2 = ext[halo - 2 * shift:halo - 2 * shift + rb, :]
        else:
            u1 = _shift_rows(ext, shift)[halo:, :]
            u2 = _shift_rows(ext, 2 * shift)[halo:, :]
        return cp[0:1, :] * u2 + cp[1:2, :] * u1 + cp[2:3, :] * u + cp[3:4, :]

    nxt = up(0)
    for r in range(0, tm, rb):
        ug, uu = nxt
        if r + rb < tm:
            nxt = up(r + rb)
        cg = conv(prev_g, ug, cpg)
        cu = conv(prev_u, uu, cpu)
        prev_g, prev_u = ug[rb - halo:, :], uu[rb - halo:, :]
        act = ((cg / (1.0 + jnp.exp(-cg))) * cu).astype(BF16)
        o_ref[r:r + rb, :] += jnp.dot(act, wd_ref[...], preferred_element_type=F32)
    if from_state:
        _store_batch_major(tail_ref.at[0], prev_g)
        _store_batch_major(tail_ref.at[1], prev_u)
    else:
        carry_g[f] = prev_g
        carry_u[f] = prev_u
        tail_ref[0] = prev_g
        tail_ref[1] = prev_u


def _conv_ffn(x, g, state, w_up, conv_params, w_down, layer, *, shift, rows_per_seq):
    n, d = x.shape
    f2 = w_up.shape[2]
    ffn = f2 // 2
    tm = min(WIDE_TILE, n)
    fn = FFN_CHUNK
    nf = ffn // fn
    n_tiles = n // tm
    tiles_per_seq = max(rows_per_seq // tm, 1)
    halo = max((CONV_W - 1) * shift, SUBLANES)
    from_state = state is not None

    tile = pl.BlockSpec((tm, d), lambda i, f: (i, 0))
    tile_in = tile if from_state else pl.BlockSpec(memory_space=pl.ANY)
    layer_gate = lambda rows: pl.BlockSpec((None, rows, fn), lambda i, f: (layer, 0, f))
    layer_up = lambda rows: pl.BlockSpec((None, rows, fn), lambda i, f: (layer, 0, f + nf))
    in_specs = [tile_in, pl.BlockSpec((1, d), lambda i, f: (0, 0))]
    args = [x, g]
    scratch = [pltpu.VMEM((tm, d), BF16)]
    if from_state:
        nb, steps = state.shape[1], state.shape[2]
        assert n == tm and nb == shift and halo == steps * shift
        in_specs += [pl.BlockSpec((None, nb, steps, fn), lambda i, f: (layer, 0, 0, f)),
                     pl.BlockSpec((None, nb, steps, fn), lambda i, f: (layer, 0, 0, f + nf))]
        args += [state, state]
        tail_spec = pl.BlockSpec((2, nb, steps, fn), lambda i, f: (0, 0, 0, f))
        tail_shape = jax.ShapeDtypeStruct((2, nb, steps, ffn), F32)
    else:
        tail_spec = pl.BlockSpec((None, 2, halo, fn), lambda i, f: (i, 0, 0, f))
        tail_shape = jax.ShapeDtypeStruct((n_tiles, 2, halo, ffn), F32)
        scratch += [pltpu.VMEM((nf, halo, fn), F32), pltpu.VMEM((nf, halo, fn), F32),
                    pltpu.VMEM((tm, d), F32), pltpu.SemaphoreType.DMA(())]
    rows = conv_params.shape[1]
    in_specs += [layer_gate(d), layer_up(d), layer_gate(rows), layer_up(rows),
                 pl.BlockSpec((None, fn, d), lambda i, f: (layer, f, 0))]
    args += [w_up, w_up, conv_params, conv_params, w_down]
    return pl.pallas_call(
        functools.partial(_ffn_kernel, shift=shift, tiles_per_seq=tiles_per_seq, from_state=from_state),
        grid=(n_tiles, nf),
        in_specs=in_specs,
        out_specs=[tile, tail_spec],
        out_shape=[jax.ShapeDtypeStruct((n, d), F32), tail_shape],
        scratch_shapes=scratch,
        compiler_params=_params(2),
        name="conv_ffn_state" if from_state else "conv_ffn",
    )(*args)


def _ple_kernel(*refs, final, casts):
    if casts:
        *refs, cast_out = refs
        x_ref, p_ref, g_ref, wg_ref, wp_ref, cast_in, o_ref = refs
        cast_out[...] = cast_in[...].astype(BF16)
    elif final:
        x_ref, p_ref, g_ref, wg_ref, wp_ref, gf_ref, o_ref = refs
    else:
        x_ref, p_ref, g_ref, wg_ref, wp_ref, o_ref = refs
    x = x_ref[...]
    hn = _rmsnorm(x, g_ref[...]).astype(BF16)
    gate = jax.nn.sigmoid(jnp.dot(hn, wg_ref[...], preferred_element_type=F32))
    emb = jnp.dot(p_ref[...].astype(BF16), wp_ref[...], preferred_element_type=F32)
    y = x + gate * emb
    if final:
        y = _rmsnorm(y, gf_ref[...])
    if len(o_ref.shape) == 3:
        _store_batch_major(o_ref, y)
    else:
        o_ref[...] = y


def _ple(x, p, g, w_gate, w_proj, layer, g_final, steps=None, cast=None):
    n, d = x.shape
    tm = min(TOKEN_TILE, n)
    final = g_final is not None
    assert not (final and cast)
    if steps is None:
        out_spec = pl.BlockSpec((tm, d), lambda i: (i, 0))
        out_shape = jax.ShapeDtypeStruct((n, d), F32)
    else:
        assert n == tm
        out_spec = pl.BlockSpec((n // steps, steps, d), lambda i: (0, 0, 0))
        out_shape = jax.ShapeDtypeStruct((n // steps, steps, d), F32)
    const = lambda a: pl.BlockSpec(a.shape, lambda i: (0, 0))
    resident = lambda a: pl.BlockSpec((None,) + a.shape[1:], lambda i: (layer, 0, 0),
                                      pipeline_mode=pl.Buffered(1))
    in_specs = [pl.BlockSpec((tm, d), lambda i: (i, 0)),
                pl.BlockSpec((None, tm, p.shape[2]), lambda i: (layer, i, 0)),
                const(g), resident(w_gate), resident(w_proj)]
    args = [x, p, g, w_gate, w_proj]
    if final:
        in_specs.append(const(g_final))
        args.append(g_final)
    out_specs, out_shapes = [out_spec], [out_shape]
    if cast:
        w, j = cast
        rows, cols = w.shape[1] // (n // tm), w.shape[2]
        assert rows * (n // tm) == w.shape[1]
        in_specs.append(pl.BlockSpec((None, rows, cols), lambda i: (j, i, 0)))
        args.append(w)
        out_specs.append(pl.BlockSpec((rows, cols), lambda i: (i, 0)))
        out_shapes.append(jax.ShapeDtypeStruct(w.shape[1:], BF16))
    out = pl.pallas_call(
        functools.partial(_ple_kernel, final=final, casts=bool(cast)),
        grid=(n // tm,),
        in_specs=in_specs,
        out_specs=out_specs,
        out_shape=out_shapes,
        compiler_params=_params(1),
        name="ple_final" if final else "ple",
    )(*args)
    return out if cast else out[0]


def _ret_in_kernel(x_in, g_ref, w_ref, cos_ref, sin_ref, o_ref, hs_ref, *fetch, dk, layout):
    if fetch:
        x_ref = fetch[0]
        wait_x = _tile_fetch(x_in, *fetch)
    else:
        x_ref, wait_x = x_in, None

    @pl.when(pl.program_id(1) == 0)
    def _():
        if wait_x is not None:
            wait_x()
        hs_ref[...] = _rmsnorm(x_ref[...], g_ref[...]).astype(BF16)

    tm = hs_ref.shape[0]
    rb = min(ROW_BLOCK, tm)
    half = dk // 2
    project = lambda r: jnp.dot(hs_ref[r:r + rb, :], w_ref[...], preferred_element_type=F32)
    nxt = project(0)
    for r in range(0, tm, rb):
        proj = nxt
        if r + rb < tm:
            nxt = project(r + rb)
        cos = cos_ref[r:r + rb, :]
        sin = sin_ref[r:r + rb, :]
        for hh in range(proj.shape[1] // dk):
            x1 = proj[:, hh * dk:hh * dk + half]
            x2 = proj[:, hh * dk + half:(hh + 1) * dk]
            y1 = (x1 * cos - x2 * sin).astype(o_ref.dtype)
            y2 = (x1 * sin + x2 * cos).astype(o_ref.dtype)
            if layout == "heads":
                o_ref[hh, r:r + rb, 0:half] = y1
                o_ref[hh, r:r + rb, half:dk] = y2
            else:
                assert rb == tm
                _store_batch_major(o_ref, y1, slice(hh * dk, hh * dk + half))
                _store_batch_major(o_ref, y2, slice(hh * dk + half, (hh + 1) * dk))


def _ret_in(x, g, w_in, tables, *, qk_width, dk, rows_per_seq, out_dtype, steps=None):
    n, d = x.shape
    e = w_in.shape[1]
    tm = min(PROJ_TILE, n)
    nc = PROJ_CHUNK
    qc = qk_width // nc
    tps = rows_per_seq // tm

    def table_block(i, c):
        return (jnp.where(c < qc, 0, jnp.where(c < 2 * qc, tps, 2 * tps)) + i % tps, 0)

    scratch = [pltpu.VMEM((tm, d), BF16)]
    if steps is None:
        x_spec = pl.BlockSpec(memory_space=pl.ANY)
        w_spec = pl.BlockSpec((d, nc), lambda i, c: (0, c))
        out_spec = pl.BlockSpec((nc // dk, tm, dk), lambda i, c: (c, i, 0))
        out_shape = jax.ShapeDtypeStruct((e // dk, n, dk), out_dtype)
        scratch += [pltpu.VMEM((tm, d), F32), pltpu.SemaphoreType.DMA(())]
    else:
        assert n == tm
        x_spec = pl.BlockSpec((tm, d), lambda i, c: (i, 0))
        w_spec = pl.BlockSpec((d, nc), lambda i, c: (0, c))
        out_spec = pl.BlockSpec((n // steps, steps, nc), lambda i, c: (0, 0, c))
        out_shape = jax.ShapeDtypeStruct((n // steps, steps, e), out_dtype)

    return pl.pallas_call(
        functools.partial(_ret_in_kernel, dk=dk, layout="heads" if steps is None else "batch"),
        grid=(n // tm, e // nc),
        in_specs=[x_spec,
                  pl.BlockSpec((1, d), lambda i, c: (0, 0)),
                  w_spec,
                  pl.BlockSpec((tm, dk // 2), table_block),
                  pl.BlockSpec((tm, dk // 2), table_block)],
        out_specs=out_spec,
        out_shape=out_shape,
        scratch_shapes=scratch,
        compiler_params=_params(2),
        name="ret_in",
    )(x, g, w_in, *tables)


def _groupnorm_gate(o, gate):
    mu = jnp.mean(o, axis=-1, keepdims=True)
    var = jnp.mean(jnp.square(o - mu), axis=-1, keepdims=True)
    on = (o - mu) * lax.rsqrt(var + EPS)
    gf = gate.astype(F32)
    return (gf / (1.0 + jnp.exp(-gf))) * on


def _ret_prompt_kernel(q_ref, k_ref, v_ref, g_ref, intra_ref, cross_ref, kdec_ref, sdec_ref,
                       o_ref, s_out_ref, s_ref, *, chunk):
    hp, t, _ = q_ref.shape
    per = v_ref.shape[0] // hp
    s_ref[...] = jnp.zeros(s_ref.shape, F32)

    def wide(ref, j, rows):
        return jnp.concatenate([ref[j * per + i, rows, :] for i in range(per)], axis=-1)

    def body(c, carry):
        rows = pl.ds(pl.multiple_of(c * chunk, chunk), chunk)
        for j in range(hp):
            q = q_ref[j, rows, :]
            k = k_ref[j, rows, :]
            v = wide(v_ref, j, rows)
            s = s_ref[j]
            sc = lax.dot_general(q, k, (((1,), (1,)), ((), ())), preferred_element_type=F32) * intra_ref[j]
            o = (jnp.dot(sc.astype(BF16), v, preferred_element_type=F32)
                 + jnp.dot(q, s.astype(BF16), preferred_element_type=F32) * cross_ref[j])
            kd = (k.astype(F32) * kdec_ref[j]).astype(BF16)
            s_ref[j] = sdec_ref[j] * s + lax.dot_general(kd, v, (((0,), (0,)), ((), ())),
                                                         preferred_element_type=F32)
            o_ref[j, rows, :] = _groupnorm_gate(o, wide(g_ref, j, rows)).astype(o_ref.dtype)
        return carry

    lax.fori_loop(0, t // chunk, body, 0, unroll=4)
    s_out_ref[...] = s_ref[...]


def _ret_prompt(proj, consts, *, batch, seq, heads, dk, dv):
    intra, cross, kdec, sdec = consts
    chunk = intra.shape[1]
    hp = RET_HEAD_GROUP
    groups = heads // hp
    per = dv // dk
    v_off = 2 * groups // per
    g_off = v_off + groups
    head_const = lambda a: pl.BlockSpec((hp,) + a.shape[1:], lambda b, h: (h, 0, 0))
    return pl.pallas_call(
        functools.partial(_ret_prompt_kernel, chunk=chunk),
        grid=(batch, groups),
        in_specs=[pl.BlockSpec((hp, seq, dk), lambda b, h: (h, b, 0)),
                  pl.BlockSpec((hp, seq, dk), lambda b, h: (groups + h, b, 0)),
                  pl.BlockSpec((hp * per, seq, dk), lambda b, h: (v_off + h, b, 0)),
                  pl.BlockSpec((hp * per, seq, dk), lambda b, h: (g_off + h, b, 0)),
                  head_const(intra), head_const(cross), head_const(kdec), head_const(sdec)],
        out_specs=[pl.BlockSpec((hp, seq, dv), lambda b, h: (h, b, 0)),
                   pl.BlockSpec((None, hp, dk, dv), lambda b, h: (b, h, 0, 0))],
        out_shape=[jax.ShapeDtypeStruct((heads, batch * seq, dv), BF16),
                   jax.ShapeDtypeStruct((batch, heads, dk, dv), F32)],
        scratch_shapes=[pltpu.VMEM((hp, dk, dv), F32)],
        compiler_params=_params(2),
        name="retention_prompt",
    )(proj, proj, proj, proj, intra, cross, kdec, sdec)


def _ret_sample_kernel(p_ref, s_ref, intra_ref, cross_ref, kdec_ref, sdec_ref, o_ref, s_out_ref,
                       *, heads, dk, dv):
    qk = heads * dk
    for n in range(p_ref.shape[0]):
        for h in range(heads):
            q = p_ref[n, :, h * dk:(h + 1) * dk].astype(BF16)
            kf = p_ref[n, :, qk + h * dk:qk + (h + 1) * dk]
            v = p_ref[n, :, 2 * qk + h * dv:2 * qk + (h + 1) * dv].astype(BF16)
            gate = p_ref[n, :, 2 * qk + heads * dv + h * dv:2 * qk + heads * dv + (h + 1) * dv]
            s = s_ref[n, h]
            sc = lax.dot_general(q, kf.astype(BF16), (((1,), (1,)), ((), ())),
                                 preferred_element_type=F32) * intra_ref[h]
            o = (jnp.dot(sc.astype(BF16), v, preferred_element_type=F32)
                 + jnp.dot(q, s.astype(BF16), preferred_element_type=F32) * cross_ref[h])
            kd = (kf * kdec_ref[h]).astype(BF16)
            s_out_ref[n, h] = sdec_ref[h] * s + lax.dot_general(kd, v, (((0,), (0,)), ((), ())),
                                                                preferred_element_type=F32)
            o_ref[n, :, h * dv:(h + 1) * dv] = _groupnorm_gate(o, gate).astype(o_ref.dtype)


def _ret_sample(proj, state, consts, *, heads, dk, dv):
    intra, cross, kdec, sdec = consts
    b, tp, e = proj.shape
    nb = SAMPLE_SEQS
    const = lambda a: pl.BlockSpec(a.shape, lambda i: (0,) * a.ndim)
    return pl.pallas_call(
        functools.partial(_ret_sample_kernel, heads=heads, dk=dk, dv=dv),
        grid=(b // nb,),
        in_specs=[pl.BlockSpec((nb, tp, e), lambda i: (i, 0, 0)),
                  pl.BlockSpec((nb, heads, dk, dv), lambda i: (i, 0, 0, 0)),
                  const(intra), const(cross), const(kdec), const(sdec)],
        out_specs=[pl.BlockSpec((nb, tp, heads * dv), lambda i: (i, 0, 0)),
                   pl.BlockSpec((nb, heads, dk, dv), lambda i: (i, 0, 0, 0))],
        out_shape=[jax.ShapeDtypeStruct((b, tp, heads * dv), F32),
                   jax.ShapeDtypeStruct((b, heads, dk, dv), F32)],
        compiler_params=_params(1),
        name="retention_sample",
    )(proj, state, intra, cross, kdec, sdec)


def _ret_out_kernel(x_ref, a_ref, w_ref, o_ref, *, head_major):
    if head_major:
        dv = a_ref.shape[2]
        y = x_ref[...]
        for h in range(a_ref.shape[0]):
            y = y + jnp.dot(a_ref[h], w_ref[h * dv:(h + 1) * dv, :], preferred_element_type=F32)
        o_ref[...] = y
    else:
        a = _time_major(a_ref).astype(BF16)
        o_ref[...] = x_ref[...] + jnp.dot(a, w_ref[...], preferred_element_type=F32)


def _ret_out(x, a, w_out, *, head_major):
    n, d = x.shape
    tm = min(TOKEN_TILE, n)
    if head_major:
        a_spec = pl.BlockSpec((a.shape[0], tm, a.shape[2]), lambda i: (0, i, 0))
    else:
        assert n == tm
        a_spec = pl.BlockSpec(a.shape, lambda i: (0, 0, 0))
    v = w_out.shape[0]
    return pl.pallas_call(
        functools.partial(_ret_out_kernel, head_major=head_major),
        grid=(n // tm,),
        in_specs=[pl.BlockSpec((tm, d), lambda i: (i, 0)),
                  a_spec,
                  pl.BlockSpec((v, d), lambda i: (0, 0), pipeline_mode=pl.Buffered(1))],
        out_specs=pl.BlockSpec((tm, d), lambda i: (i, 0)),
        out_shape=jax.ShapeDtypeStruct((n, d), F32),
        compiler_params=_params(1),
        name="ret_out",
    )(x, a, w_out)


def _decay_consts(chunk, heads, dk, dv):
    log_g = np.log1p(-(2.0 ** (-5.0 - np.arange(heads, dtype=np.float64))))
    n = np.arange(chunk, dtype=np.float64)
    diff = n[:, None] - n[None, :]
    intra = np.where(diff >= 0, np.exp(log_g[:, None, None] * np.maximum(diff, 0.0)), 0.0)
    cross = np.exp(log_g[:, None] * (n + 1.0))
    kdec = np.exp(log_g[:, None] * (chunk - 1.0 - n))
    sdec = np.exp(log_g * chunk)
    f32 = lambda a: jnp.asarray(np.ascontiguousarray(a), dtype=F32)
    return (f32(intra),
            f32(np.broadcast_to(cross[:, :, None], (heads, chunk, dv))),
            f32(np.broadcast_to(kdec[:, :, None], (heads, chunk, dk))),
            f32(np.broadcast_to(sdec[:, None, None], (heads, 1, dv))))


def _rotary_tables(pos, dk):
    half = dk // 2
    inv = ROPE_BASE ** (-np.arange(half, dtype=np.float64) / half)
    ang = np.asarray(pos, dtype=np.float64)[:, None] * inv[None, :]
    cos, sin = np.cos(ang), np.sin(ang)
    scale = dk ** -0.5
    f32 = lambda a: jnp.asarray(a, dtype=F32)
    return (f32(np.concatenate([cos, cos * scale, np.ones_like(cos)])),
            f32(np.concatenate([sin, sin * scale, np.zeros_like(sin)])))


def kernel(x_prompt, x_sample, p_prompt, p_sample, state_pool, state_ret, state_conv, norm_mix, norm_ffn,
           norm_ple, norm_final, pool_w, pool_scale, ret_w_in, ret_w_out, ffn_w_up, ffn_conv_w,
           ffn_conv_b, ffn_w_down, ple_w_proj, ple_w_gate):
    b, t, d = x_prompt.shape
    bs, ts, _ = x_sample.shape
    depth = norm_mix.shape[0]
    heads = RET_HEADS
    dk = d // heads
    dv = 2 * dk
    qk = heads * dk
    f2 = ffn_w_up.shape[2]
    ffn = f2 // 2
    assert ts >= CONV_W - 1 and PAST_LEN >= POOL_BUF

    row2 = lambda a: a.reshape(1, -1)
    t_major = lambda a: jnp.swapaxes(a, 0, 1).reshape(-1, a.shape[-1])
    b_major = lambda a: jnp.swapaxes(a.reshape(-1, bs, a.shape[-1]), 0, 1)

    xp = x_prompt.reshape(b * t, d)
    xs = t_major(x_sample)
    conv_params = jnp.concatenate([ffn_conv_w, ffn_conv_b[:, None, :]], axis=1)
    pp = p_prompt.reshape(depth, b * t, -1)
    ps = jnp.swapaxes(p_sample, 1, 2).reshape(depth, ts * bs, -1)

    tables_p = _rotary_tables(np.arange(t), dk)
    tables_s = _rotary_tables(PAST_LEN + np.repeat(np.arange(ts), bs), dk)
    chunk_p = RET_CHUNK if t % RET_CHUNK == 0 else t
    consts_p = _decay_consts(chunk_p, heads, dk, dv)
    consts_s = _decay_consts(ts, heads, dk, dv)

    new_pool_p, new_pool_s, new_ret_p, new_ret_s, new_conv_p, new_conv_s = [], [], [], [], [], []
    w_out_bf16 = None
    for i in range(depth):
        jm = i // 2
        if i % 2 == 0:
            w = pool_w[jm]
            g = row2(norm_mix[i])
            sc = row2(pool_scale[jm])
            xp, hl = _pool_mixer(xp, None, g, w, sc, shift=1, pos0=0, rows_per_seq=t)
            new_pool_p.append(hl[:, -POOL_BUF:, :])
            xs3 = x_sample if i == 0 else b_major(xs)
            xs, ns = _pool_mixer(xs3, t_major(state_pool[jm]), g, w, sc, shift=bs, pos0=PAST_LEN,
                                 rows_per_seq=ts * bs)
            new_pool_s.append(ns)
        else:
            w_in = ret_w_in[jm]
            w_out = w_out_bf16 if w_out_bf16 is not None else ret_w_out[jm].astype(BF16)
            w_out_bf16 = None
            g = row2(norm_mix[i])
            proj_p = _ret_in(xp, g, w_in, tables_p, qk_width=qk, dk=dk, rows_per_seq=t, out_dtype=BF16)
            a_p, s_p = _ret_prompt(proj_p, consts_p, batch=b, seq=t, heads=heads, dk=dk, dv=dv)
            xp = _ret_out(xp, a_p, w_out, head_major=True)
            new_ret_p.append(s_p)
            proj_s = _ret_in(xs, g, w_in, tables_s, qk_width=qk, dk=dk, rows_per_seq=ts * bs, out_dtype=F32,
                             steps=ts)
            a_s, s_s = _ret_sample(proj_s, state_ret[jm], consts_s, heads=heads, dk=dk, dv=dv)
            xs = _ret_out(xs, a_s, w_out, head_major=False)
            new_ret_s.append(s_s)

        g = row2(norm_ffn[i])
        xp, tails = _conv_ffn(xp, g, None, ffn_w_up, conv_params, ffn_w_down, i, shift=1, rows_per_seq=t)
        tiles_per_seq = tails.shape[0] // b
        last = tails[tiles_per_seq - 1::tiles_per_seq, :, -(CONV_W - 1):, :]
        new_conv_p.append(jnp.concatenate([last[:, 0], last[:, 1]], axis=-1))
        xs, tails = _conv_ffn(xs, g, state_conv, ffn_w_up, conv_params, ffn_w_down, i, shift=bs,
                              rows_per_seq=ts * bs)
        new_conv_s.append(jnp.concatenate([tails[0], tails[1]], axis=-1))

        g = row2(norm_ple[i])
        last_layer = i == depth - 1
        g_final = row2(norm_final) if last_layer else None
        if not last_layer and (i + 1) % 2 == 1:
            xp, w_out_bf16 = _ple(xp, pp, g, ple_w_gate, ple_w_proj, i, g_final, cast=(ret_w_out, (i + 1) // 2))
        else:
            xp = _ple(xp, pp, g, ple_w_gate, ple_w_proj, i, g_final)
        xs = _ple(xs, ps, g, ple_w_gate, ple_w_proj, i, g_final, steps=ts if last_layer else None)

    return (xp.reshape(b, t, d), xs,
            jnp.stack(new_pool_p), jnp.stack(new_pool_s),
            jnp.stack(new_ret_p), jnp.stack(new_ret_s),
            jnp.stack(new_conv_p), jnp.stack(new_conv_s))
```
